```python
import math
import jax, jax.numpy as jnp
from jax import lax
import numpy as np

D_MODEL = 1024
BATCH = 1
SEQ = 16384
DEPTH = 2
DEC_BATCH = 8
DEC_SEQ = 16
PAST_LEN = 1024

CHUNK = 64
N_A = DEPTH // 2
N_B = DEPTH - N_A
D_RNN = D_MODEL
N_LRU_BLOCKS = 8
LRU_BLOCK = D_RNN // N_LRU_BLOCKS
CONV_W = 4
LRU_C = 8.0
N_HEADS = 8
QK_NOPE = 128
QK_ROPE = 64
V_HEAD = 128
Q_LORA = 512
KV_LORA = 256
ROPE_THETA = 10000.0
Q_BLOCK = 128
SM_SCALE = (QK_NOPE + QK_ROPE) ** -0.5
N_EXPERTS = 32
TOP_K = 4
D_EXPERT = D_MODEL
SWIGLU_ALPHA = 1.702
SWIGLU_LIMIT = 7.0
DN_ALPHA = (2.0 * DEPTH) ** 0.25
DN_BETA = (8.0 * DEPTH) ** -0.25
LN_EPS = 1e-5
RMS_EPS = 1e-6

kernel_name = 'yoco_rglru_mla_moe_stream_step'


def layer_norm(x, g, b):
    xf = x.astype(jnp.float32)
    mu = jnp.mean(xf, -1, keepdims=True)
    var = jnp.mean(jnp.square(xf - mu), -1, keepdims=True)
    y = (xf - mu) * lax.rsqrt(var + LN_EPS)
    return (y * g.astype(jnp.float32) + b.astype(jnp.float32)).astype(x.dtype)


def rms_norm(x, g):
    xf = x.astype(jnp.float32)
    y = xf * lax.rsqrt(jnp.mean(jnp.square(xf), -1, keepdims=True) + RMS_EPS)
    return (y * g.astype(jnp.float32)).astype(x.dtype)


def ada_mod(c, w, b, n):
    m = jax.nn.silu(c) @ w + b
    return [t[:, None, :] for t in jnp.split(m, n, axis=-1)]


def rope(x, pos):
    half = QK_ROPE // 2
    inv = ROPE_THETA ** (-2.0 * jnp.arange(half, dtype=jnp.float32) / QK_ROPE)
    ang = pos.astype(jnp.float32)[:, None] * inv[None, :]
    cos = jnp.cos(ang)[None, :, None, :]
    sin = jnp.sin(ang)[None, :, None, :]
    xf = x.astype(jnp.float32)
    x1, x2 = xf[..., :half], xf[..., half:]
    return jnp.concatenate([x1 * cos - x2 * sin, x1 * sin + x2 * cos], -1).astype(x.dtype)


def causal_conv(u, buf, w, b):
    S = u.shape[1]
    up = jnp.concatenate([buf, u], axis=1)
    y = b + up[:, 0:S] * w[0]
    for k in range(1, CONV_W):
        y = y + up[:, k:k + S] * w[k]
    return y, up[:, -(CONV_W - 1):]


def rglru(u, h0, w_a, b_a, w_i, b_i, lam):
    Bsz, S, _ = u.shape
    ub = u.reshape(Bsz, S, N_LRU_BLOCKS, LRU_BLOCK)
    r = jax.nn.sigmoid(jnp.einsum('bsnd,nde->bsne', ub, w_a) + b_a).reshape(Bsz, S, D_RNN)
    i = jax.nn.sigmoid(jnp.einsum('bsnd,nde->bsne', ub, w_i) + b_i).reshape(Bsz, S, D_RNN)
    log_a = -LRU_C * r.astype(jnp.float32) * jax.nn.softplus(-lam.astype(jnp.float32))
    a = jnp.exp(log_a)
    bx = jnp.sqrt(-jnp.expm1(2.0 * log_a)) * (i * u).astype(jnp.float32)

    def combine(lhs, rhs):
        a1, b1 = lhs
        a2, b2 = rhs
        return a1 * a2, a2 * b1 + b2

    a_cum, b_cum = lax.associative_scan(combine, (a, bx), axis=1)
    h = a_cum * h0.astype(jnp.float32)[:, None, :] + b_cum
    return h.astype(u.dtype), h[:, -1].astype(u.dtype)


def recurrent_block(h, conv_buf, h0, w_in, b_in, conv_w, conv_b, w_a, b_a, w_i, b_i, lam, w_out, b_out):
    proj = h @ w_in + b_in
    gate, u = jnp.split(proj, 2, axis=-1)
    u, conv_new = causal_conv(u, conv_buf, conv_w, conv_b)
    y, h_last = rglru(u, h0, w_a, b_a, w_i, b_i, lam)
    out = (jax.nn.gelu(gate) * y) @ w_out + b_out
    return out, conv_new, h_last


def mla_kv(x, c, pos, w_ada, b_ada, w_dkv, g_kv):
    shift, scale = ada_mod(c, w_ada, b_ada, 2)
    kv = (x * (1.0 + scale) + shift) @ w_dkv
    ckv = rms_norm(kv[..., :KV_LORA], g_kv)
    kpe = rope(kv[..., None, KV_LORA:], pos)[:, :, 0]
    return ckv, kpe


def mla_query(h, pos, w_dq, g_q, w_uq, w_uk):
    Bsz, S, _ = h.shape
    q = (rms_norm(h @ w_dq, g_q) @ w_uq).reshape(Bsz, S, N_HEADS, QK_NOPE + QK_ROPE)
    q_lat = jnp.einsum('bshd,hdc->bshc', q[..., :QK_NOPE], w_uk)
    q_pe = rope(q[..., QK_NOPE:], pos)
    return q_lat, q_pe


def attend_latent(q_lat, q_pe, ckv, kpe, mask):
    s = jnp.einsum('bqhc,bkc->bhqk', q_lat, ckv) + jnp.einsum('bqhr,bkr->bhqk', q_pe, kpe)
    s = jnp.where(mask, s.astype(jnp.float32) * SM_SCALE, -jnp.inf)
    p = jax.nn.softmax(s, axis=-1).astype(ckv.dtype)
    return jnp.einsum('bhqk,bkc->bqhc', p, ckv)


def mla_prompt_attention(q_lat, q_pe, ckv, kpe):
    Bsz, S = q_lat.shape[0], q_lat.shape[1]
    nblk = S // Q_BLOCK
    key_chunk = jnp.arange(S) // CHUNK
    ql = q_lat.reshape(Bsz, nblk, Q_BLOCK, N_HEADS, KV_LORA).swapaxes(0, 1)
    qp = q_pe.reshape(Bsz, nblk, Q_BLOCK, N_HEADS, QK_ROPE).swapaxes(0, 1)
    q_chunk = key_chunk.reshape(nblk, Q_BLOCK)

    def one_block(args):
        ql_b, qp_b, qc_b = args
        return attend_latent(ql_b, qp_b, ckv, kpe, key_chunk[None, :] <= qc_b[:, None])

    o = lax.map(one_block, (ql, qp, q_chunk))
    return o.swapaxes(0, 1).reshape(Bsz, S, N_HEADS, KV_LORA)


def mla_out(o_lat, w_uv, w_o):
    o = jnp.einsum('bshc,hcv->bshv', o_lat, w_uv)
    return o.reshape(o.shape[0], o.shape[1], N_HEADS * V_HEAD) @ w_o


def moe(h, w_r, b_r, w_gu, b_gu, w_dn, b_dn):
    Bsz, S, D = h.shape
    t = h.reshape(-1, D)
    logits = (t @ w_r + b_r).astype(jnp.float32)
    top_val, top_idx = lax.top_k(logits, TOP_K)
    top_w = jax.nn.softmax(top_val, axis=-1)
    gates = jnp.einsum('nk,nke->ne', top_w, jax.nn.one_hot(top_idx, N_EXPERTS, dtype=jnp.float32)).astype(t.dtype)
    out = jnp.zeros_like(t)
    for e in range(N_EXPERTS):
        gu = t @ w_gu[e] + b_gu[e]
        g = jnp.minimum(gu[:, :D_EXPERT], SWIGLU_LIMIT)
        u = jnp.clip(gu[:, D_EXPERT:], -SWIGLU_LIMIT, SWIGLU_LIMIT)
        act = (u + 1.0) * (g * jax.nn.sigmoid(SWIGLU_ALPHA * g))
        out = out + gates[:, e:e + 1] * (act @ w_dn[e] + b_dn[e])
    return out.reshape(Bsz, S, D)


def trunk(x, c, pos, conv_state, lru_state, past_ckv, past_kpe, P, is_prompt):
    new_conv, new_h = [], []
    ckv_new = kpe_new = ckv_all = kpe_all = None
    for l in range(DEPTH):
        shift, scale, gate = ada_mod(c, P['w_ada'][l, 0], P['b_ada'][l, 0], 3)
        h = x * (1.0 + scale) + shift
        if l < N_A:
            m, conv_l, h_l = recurrent_block(
                h, conv_state[l], lru_state[l], P['lru_w_in'][l], P['lru_b_in'][l],
                P['lru_conv_w'][l], P['lru_conv_b'][l], P['lru_w_a'][l], P['lru_b_a'][l],
                P['lru_w_i'][l], P['lru_b_i'][l], P['lru_lambda'][l], P['lru_w_out'][l], P['lru_b_out'][l])
            new_conv.append(conv_l)
            new_h.append(h_l)
        else:
            j = l - N_A
            q_lat, q_pe = mla_query(h, pos, P['mla_w_dq'][j], P['mla_g_q'][j], P['mla_w_uq'][j], P['mla_w_uk'])
            if is_prompt:
                o_lat = mla_prompt_attention(q_lat, q_pe, ckv_all, kpe_all)
            else:
                visible = jnp.ones((q_lat.shape[1], ckv_all.shape[1]), dtype=bool)
                o_lat = attend_latent(q_lat, q_pe, ckv_all, kpe_all, visible)
            m = mla_out(o_lat, P['mla_w_uv'], P['mla_w_o'][j])
        x = layer_norm(DN_ALPHA * x + (1.0 + gate) * m, P['ln_g'][l, 0], P['ln_b'][l, 0])
        shift, scale, gate = ada_mod(c, P['w_ada'][l, 1], P['b_ada'][l, 1], 3)
        m = moe(x * (1.0 + scale) + shift, P['moe_w_r'][l], P['moe_b_r'][l], P['moe_w_gu'][l],
                P['moe_b_gu'][l], P['moe_w_dn'][l], P['moe_b_dn'][l])
        x = layer_norm(DN_ALPHA * x + (1.0 + gate) * m, P['ln_g'][l, 1], P['ln_b'][l, 1])
        if l == N_A - 1:
            ckv_new, kpe_new = mla_kv(x, c, pos, P['kv_w_ada'], P['kv_b_ada'], P['mla_w_dkv'], P['mla_g_kv'])
            if is_prompt:
                ckv_all, kpe_all = ckv_new, kpe_new
            else:
                ckv_all = jnp.concatenate([past_ckv, ckv_new], axis=1)
                kpe_all = jnp.concatenate([past_kpe, kpe_new], axis=1)
    return x, jnp.stack(new_conv), jnp.stack(new_h), ckv_new, kpe_new


def setup_inputs(seed: int = 0) -> dict:
    key = jax.random.key(seed)
    ks = iter(jax.random.split(key, 64))

    def nrm(shape, s):
        return jax.random.normal(next(ks), shape, jnp.float32) * s

    D = D_MODEL
    a0 = jax.random.uniform(next(ks), (N_A, D_RNN), jnp.float32, 0.9, 0.999)
    s0 = a0 ** (1.0 / LRU_C)
    lru_lambda = jnp.log(s0) - jnp.log1p(-s0)
    return {
        'x_prompt': nrm((BATCH, SEQ, D), 1.0),
        'x_sample': nrm((DEC_BATCH, DEC_SEQ, D), 1.0),
        'state_conv': nrm((N_A, DEC_BATCH, CONV_W - 1, D_RNN), 1.0),
        'state_rglru': nrm((N_A, DEC_BATCH, D_RNN), 0.5),
        'cache_ckv': nrm((DEC_BATCH, PAST_LEN, KV_LORA), 1.0),
        'cache_kpe': nrm((DEC_BATCH, PAST_LEN, QK_ROPE), 1.0),
        'c_prompt': nrm((BATCH, D), 1.0),
        'c_sample': nrm((DEC_BATCH, D), 1.0),
        'w_ada': nrm((DEPTH, 2, D, 3 * D), 0.2 * D ** -0.5),
        'b_ada': nrm((DEPTH, 2, 3 * D), 0.01),
        'ln_g': 1.0 + nrm((DEPTH, 2, D), 0.01),
        'ln_b': nrm((DEPTH, 2, D), 0.01),
        'lru_w_in': nrm((N_A, D, 2 * D_RNN), D ** -0.5),
        'lru_b_in': nrm((N_A, 2 * D_RNN), 0.01),
        'lru_conv_w': nrm((N_A, CONV_W, D_RNN), CONV_W ** -0.5),
        'lru_conv_b': nrm((N_A, D_RNN), 0.01),
        'lru_w_a': nrm((N_A, N_LRU_BLOCKS, LRU_BLOCK, LRU_BLOCK), LRU_BLOCK ** -0.5),
        'lru_b_a': nrm((N_A, N_LRU_BLOCKS, LRU_BLOCK), 0.01),
        'lru_w_i': nrm((N_A, N_LRU_BLOCKS, LRU_BLOCK, LRU_BLOCK), LRU_BLOCK ** -0.5),
        'lru_b_i': nrm((N_A, N_LRU_BLOCKS, LRU_BLOCK), 0.01),
        'lru_lambda': lru_lambda,
        'lru_w_out': nrm((N_A, D_RNN, D), DN_BETA * D_RNN ** -0.5),
        'lru_b_out': nrm((N_A, D), 0.01),
        'kv_w_ada': nrm((D, 2 * D), 0.2 * D ** -0.5),
        'kv_b_ada': nrm((2 * D,), 0.01),
        'mla_w_dkv': nrm((D, KV_LORA + QK_ROPE), D ** -0.5),
        'mla_g_kv': 1.0 + nrm((KV_LORA,), 0.01),
        'mla_w_uk': nrm((N_HEADS, QK_NOPE, KV_LORA), KV_LORA ** -0.5),
        'mla_w_uv': nrm((N_HEADS, KV_LORA, V_HEAD), KV_LORA ** -0.5),
        'mla_w_dq': nrm((N_B, D, Q_LORA), D ** -0.5),
        'mla_g_q': 1.0 + nrm((N_B, Q_LORA), 0.01),
        'mla_w_uq': nrm((N_B, Q_LORA, N_HEADS * (QK_NOPE + QK_ROPE)), Q_LORA ** -0.5),
        'mla_w_o': nrm((N_B, N_HEADS * V_HEAD, D), DN_BETA * (N_HEADS * V_HEAD) ** -0.5),
        'moe_w_r': nrm((DEPTH, D, N_EXPERTS), D ** -0.5),
        'moe_b_r': nrm((DEPTH, N_EXPERTS), 0.01),
        'moe_w_gu': nrm((DEPTH, N_EXPERTS, D, 2 * D_EXPERT), D ** -0.5),
        'moe_b_gu': nrm((DEPTH, N_EXPERTS, 2 * D_EXPERT), 0.01),
        'moe_w_dn': nrm((DEPTH, N_EXPERTS, D_EXPERT, D), DN_BETA * D_EXPERT ** -0.5),
        'moe_b_dn': nrm((DEPTH, N_EXPERTS, D), 0.01),
    }


def reference(x_prompt, x_sample, state_conv, state_rglru, cache_ckv, cache_kpe, c_prompt, c_sample,
              w_ada, b_ada, ln_g, ln_b, lru_w_in, lru_b_in, lru_conv_w, lru_conv_b, lru_w_a, lru_b_a,
              lru_w_i, lru_b_i, lru_lambda, lru_w_out, lru_b_out, kv_w_ada, kv_b_ada, mla_w_dkv, mla_g_kv,
              mla_w_uk, mla_w_uv, mla_w_dq, mla_g_q, mla_w_uq, mla_w_o, moe_w_r, moe_b_r, moe_w_gu,
              moe_b_gu, moe_w_dn, moe_b_dn):
    P = dict(w_ada=w_ada, b_ada=b_ada, ln_g=ln_g, ln_b=ln_b, lru_w_in=lru_w_in, lru_b_in=lru_b_in,
             lru_conv_w=lru_conv_w, lru_conv_b=lru_conv_b, lru_w_a=lru_w_a, lru_b_a=lru_b_a,
             lru_w_i=lru_w_i, lru_b_i=lru_b_i, lru_lambda=lru_lambda, lru_w_out=lru_w_out,
             lru_b_out=lru_b_out, kv_w_ada=kv_w_ada, kv_b_ada=kv_b_ada, mla_w_dkv=mla_w_dkv,
             mla_g_kv=mla_g_kv, mla_w_uk=mla_w_uk, mla_w_uv=mla_w_uv, mla_w_dq=mla_w_dq,
             mla_g_q=mla_g_q, mla_w_uq=mla_w_uq, mla_w_o=mla_w_o, moe_w_r=moe_w_r, moe_b_r=moe_b_r,
             moe_w_gu=moe_w_gu, moe_b_gu=moe_b_gu, moe_w_dn=moe_w_dn, moe_b_dn=moe_b_dn)
    Bp, Sp = x_prompt.shape[0], x_prompt.shape[1]
    conv0 = jnp.zeros((N_A, Bp, CONV_W - 1, D_RNN), x_prompt.dtype)
    h0 = jnp.zeros((N_A, Bp, D_RNN), x_prompt.dtype)
    y_prompt, conv_p, h_p, ckv_p, kpe_p = trunk(
        x_prompt, c_prompt, jnp.arange(Sp), conv0, h0, None, None, P, True)
    Ss = x_sample.shape[1]
    y_sample, conv_s, h_s, ckv_s, kpe_s = trunk(
        x_sample, c_sample, PAST_LEN + jnp.arange(Ss), state_conv, state_rglru,
        cache_ckv, cache_kpe, P, False)
    return (y_prompt, y_sample, conv_p, h_p, ckv_p, kpe_p, conv_s, h_s, ckv_s, kpe_s)
```

```python
import functools

import numpy as np
import jax
import jax.numpy as jnp
from jax import lax
from jax.experimental import pallas as pl
from jax.experimental.pallas import tpu as pltpu

F32 = jnp.float32
BF16 = jnp.bfloat16

CHUNK = 64
N_LRU_BLOCKS = 8
LRU_C = 8.0
ROPE_THETA = 10000.0
TOP_K = 4
SWIGLU_ALPHA = 1.702
SWIGLU_LIMIT = 7.0
LN_EPS = 1e-5
RMS_EPS = 1e-6

LANES = 128
SUBLANES = 8
VMEM_LIMIT = 56 * 1024 * 1024

EXPERT_TM = 256
ROUTER_TM = 384
PERM_TM = 128
ATT_TQ = 256
ATT_TK = 512


def _sigmoid(x):
    return 1.0 / (1.0 + jnp.exp(-x))


def _layer_norm(v, g, b):
    mu = jnp.mean(v, axis=-1, keepdims=True)
    d = v - mu
    var = jnp.mean(d * d, axis=-1, keepdims=True)
    return d * lax.rsqrt(var + LN_EPS) * g + b


def _full(shape):
    nd = len(shape)
    return pl.BlockSpec(shape, lambda *_: (0,) * nd)


def _params(sem, vmem=VMEM_LIMIT):
    return pltpu.CompilerParams(dimension_semantics=sem, vmem_limit_bytes=vmem)


def _ada_kernel(c_ref, w_ref, b_ref, o_ref):
    c = c_ref[...]
    s = (c * _sigmoid(c)).astype(BF16)
    o_ref[...] = jnp.dot(s, w_ref[...].astype(BF16), preferred_element_type=F32) + b_ref[...]


def _ada(c_rows, w, b, tn=1024):
    g, d, n = w.shape
    r = c_rows.shape[0]
    return pl.pallas_call(
        _ada_kernel,
        grid=(g, n // tn),
        in_specs=[_full((r, d)),
                  pl.BlockSpec((None, d, tn), lambda i, j: (i, 0, j)),
                  pl.BlockSpec((None, 1, tn), lambda i, j: (i, 0, j))],
        out_specs=pl.BlockSpec((None, r, tn), lambda i, j: (i, 0, j)),
        out_shape=jax.ShapeDtypeStruct((g, r, n), F32),
        compiler_params=_params(("arbitrary", "arbitrary")),
        name="ada_mod",
    )(c_rows, w, b)


def _l0_kernel(x_ref, mod_ref, mod2_ref, conv0_ref, h0_ref, win_ref, bin_ref, cw_ref, cb_ref,
               wa_ref, ba_ref, wi_ref, bi_ref, lam_ref, wout_ref, bout_ref, lng_ref, lnb_ref,
               x1_ref, hm_ref, convn_ref, hlast_ref,
               ubuf, a_s, b_s, h_s, *, tb, d, alpha):
    t = pl.program_id(1)
    pad = SUBLANES
    hist = convn_ref.shape[0]

    @pl.when(t == 0)
    def _():
        ubuf[pad - hist:pad, :] = conv0_ref[...]
        h_s[...] = h0_ref[...]

    x = x_ref[...]
    mod = mod_ref[...]
    shift, scale, gate = mod[:, :d], mod[:, d:2 * d], mod[:, 2 * d:]
    h = (x * (1.0 + scale) + shift).astype(BF16)
    proj = jnp.dot(h, win_ref[...], preferred_element_type=F32) + bin_ref[...]
    gate_b = proj[:, :d]
    ubuf[pad:pad + tb, :] = proj[:, d:]

    cw = cw_ref[...]
    u = cb_ref[...]
    for k in range(hist + 1):
        u = u + ubuf[pad - hist + k:pad - hist + k + tb, :] * cw[k:k + 1, :]
    ubuf[pad - hist:pad, :] = ubuf[pad + tb - hist:pad + tb, :]

    ub = u.astype(BF16)
    blk = d // N_LRU_BLOCKS
    lam = lam_ref[...]
    neg = -lam
    softplus = jnp.maximum(neg, 0.0) + jnp.log1p(jnp.exp(-jnp.abs(neg)))
    for n in range(N_LRU_BLOCKS):
        sl = slice(n * blk, (n + 1) * blk)
        un = ub[:, sl]
        r = _sigmoid(jnp.dot(un, wa_ref[n], preferred_element_type=F32) + ba_ref[:, sl])
        ig = _sigmoid(jnp.dot(un, wi_ref[n], preferred_element_type=F32) + bi_ref[:, sl])
        log_a = -LRU_C * r * softplus[:, sl]
        one_m_a2 = -jnp.tanh(log_a) * (jnp.exp(2.0 * log_a) + 1.0)
        a_s[:, sl] = jnp.exp(log_a)
        b_s[:, sl] = jnp.sqrt(one_m_a2) * (ig * u[:, sl])

    row = lax.broadcasted_iota(jnp.int32, (SUBLANES, d), 0)

    def group(g, hprev):
        r0 = pl.multiple_of(g * SUBLANES, SUBLANES)
        a = a_s[pl.ds(r0, SUBLANES), :]
        b = b_s[pl.ds(r0, SUBLANES), :]
        for sh in (1, 2, 4):
            keep = row >= sh
            a_sh = pltpu.roll(a, sh, axis=0)
            b_sh = pltpu.roll(b, sh, axis=0)
            b = jnp.where(keep, a * b_sh + b, b)
            a = jnp.where(keep, a * a_sh, a)
        hh = a * hprev + b
        b_s[pl.ds(r0, SUBLANES), :] = hh
        return hh[SUBLANES - 1:SUBLANES, :]

    h_s[...] = lax.fori_loop(0, tb // SUBLANES, group, h_s[...])

    y = b_s[...]
    gl = 0.5 * gate_b * (1.0 + jnp.tanh(0.7978845608028654 * (gate_b + 0.044715 * gate_b * gate_b * gate_b)))
    out = jnp.dot((gl * y).astype(BF16), wout_ref[...], preferred_element_type=F32) + bout_ref[...]
    x1 = _layer_norm(alpha * x + (1.0 + gate) * out, lng_ref[...], lnb_ref[...])
    x1_ref[...] = x1
    mod2 = mod2_ref[...]
    hm_ref[...] = (x1 * (1.0 + mod2[:, d:2 * d]) + mod2[:, :d]).astype(BF16)

    @pl.when(t == pl.num_programs(1) - 1)
    def _():
        convn_ref[...] = ubuf[pad - hist:pad, :]
        hlast_ref[...] = h_s[...]


def _l0_mixer(x, mod, mod2, conv0, h0, w, tb, alpha):
    bsz, s, d = x.shape
    hist = conv0.shape[1]
    tok = lambda width: pl.BlockSpec((None, tb, width), lambda b, t: (b, t, 0))
    per_b = lambda rows, width: pl.BlockSpec((None, rows, width), lambda b, t: (b, 0, 0))
    kern = functools.partial(_l0_kernel, tb=tb, d=d, alpha=alpha)
    weights = [w["w_in"], w["b_in"], w["conv_w"], w["conv_b"], w["w_a"], w["b_a"], w["w_i"], w["b_i"],
               w["lam"], w["w_out"], w["b_out"], w["ln_g"], w["ln_b"]]
    return pl.pallas_call(
        kern,
        grid=(bsz, s // tb),
        in_specs=[tok(d), per_b(1, 3 * d), per_b(1, 3 * d), per_b(hist, d), per_b(1, d)]
                 + [_full(a.shape) for a in weights],
        out_specs=[tok(d), tok(d), per_b(hist, d), per_b(1, d)],
        out_shape=[jax.ShapeDtypeStruct((bsz, s, d), F32), jax.ShapeDtypeStruct((bsz, s, d), BF16),
                   jax.ShapeDtypeStruct((bsz, hist, d), F32), jax.ShapeDtypeStruct((bsz, 1, d), F32)],
        scratch_shapes=[pltpu.VMEM((tb + SUBLANES, d), F32), pltpu.VMEM((tb, d), F32),
                        pltpu.VMEM((tb, d), F32), pltpu.VMEM((1, d), F32)],
        compiler_params=_params(("arbitrary", "arbitrary")),
        name="l0_mixer",
    )(x, mod, mod2, conv0, h0, *weights)


def _router_kernel(hm_ref, wr_ref, br_ref, eid_ref, tw_ref, rank_ref, cnt_ref, base, *, tm, ne):
    i = pl.program_id(0)

    @pl.when(i == 0)
    def _():
        base[...] = jnp.zeros_like(base)

    logits = jnp.dot(hm_ref[...], wr_ref[...], preferred_element_type=F32) + br_ref[...]
    col = lax.broadcasted_iota(jnp.int32, (tm, ne), 1).astype(F32)
    l = logits
    vals, idxs = [], []
    for _ in range(TOP_K):
        m = jnp.max(l, axis=-1, keepdims=True)
        idx = jnp.min(jnp.where(l == m, col, float(ne)), axis=-1, keepdims=True)
        vals.append(m)
        idxs.append(idx)
        l = jnp.where(col == idx, -jnp.inf, l)
    es = [jnp.exp(v - vals[0]) for v in vals]
    tot = es[0] + es[1] + es[2] + es[3]

    ri = lax.broadcasted_iota(jnp.int32, (tm, tm), 0)
    ci = lax.broadcasted_iota(jnp.int32, (tm, tm), 1)
    tri = (ci < ri).astype(BF16)
    run = base[...]
    lane = lax.broadcasted_iota(jnp.int32, (tm, LANES), 1)
    eid_o = jnp.zeros((tm, LANES), F32)
    tw_o = jnp.zeros((tm, LANES), F32)
    rk_o = jnp.zeros((tm, LANES), F32)
    for k in range(TOP_K):
        oh = col == idxs[k]
        ohf = oh.astype(F32)
        before = jnp.dot(tri, ohf.astype(BF16), preferred_element_type=F32)
        rank = jnp.sum(jnp.where(oh, before + run, 0.0), axis=-1, keepdims=True)
        run = run + jnp.sum(ohf, axis=0, keepdims=True)
        sel = lane == k
        eid_o = jnp.where(sel, idxs[k], eid_o)
        tw_o = jnp.where(sel, es[k] / tot, tw_o)
        rk_o = jnp.where(sel, rank, rk_o)
    base[...] = run
    eid_ref[...] = eid_o.astype(jnp.int32)
    tw_ref[...] = tw_o
    rank_ref[...] = rk_o.astype(jnp.int32)
    cnt_ref[...] = run


def _router(hm, w_r, b_r, tm=ROUTER_TM):
    n, d = hm.shape
    ne = w_r.shape[1]
    kern = functools.partial(_router_kernel, tm=tm, ne=ne)
    tokrow = pl.BlockSpec((tm, LANES), lambda i: (i, 0))
    return pl.pallas_call(
        kern,
        grid=(n // tm,),
        in_specs=[pl.BlockSpec((tm, d), lambda i: (i, 0)), _full((d, ne)), _full((1, ne))],
        out_specs=[tokrow, tokrow, tokrow, _full((1, ne))],
        out_shape=[jax.ShapeDtypeStruct((n, LANES), jnp.int32), jax.ShapeDtypeStruct((n, LANES), F32),
                   jax.ShapeDtypeStruct((n, LANES), jnp.int32), jax.ShapeDtypeStruct((1, ne), F32)],
        scratch_shapes=[pltpu.VMEM((1, ne), F32)],
        compiler_params=_params(("arbitrary",)),
        name="moe_router",
    )(hm, w_r, b_r)


def _permute_kernel(pos_ref, hm_ref, init_ref, out_ref, sem, *, tm):
    del init_ref

    def copy(t, k):
        return pltpu.make_async_copy(hm_ref.at[t], out_ref.at[pos_ref[t * TOP_K + k]], sem)

    def issue(t, c):
        for k in range(TOP_K):
            copy(t, k).start()
        return c

    def drain(t, c):
        for k in range(TOP_K):
            copy(t, k).wait()
        return c

    lax.fori_loop(0, tm, issue, 0)
    lax.fori_loop(0, tm, drain, 0)


def _permute(hm3, pos_flat, m_pad, tm=PERM_TM):
    n, sub, lanes = hm3.shape
    init = jnp.zeros((m_pad, sub, lanes), hm3.dtype)
    return pl.pallas_call(
        functools.partial(_permute_kernel, tm=tm),
        grid=(n // tm,),
        in_specs=[pl.BlockSpec((tm * TOP_K,), lambda i: (i,), memory_space=pltpu.SMEM),
                  pl.BlockSpec((tm, sub, lanes), lambda i: (i, 0, 0)),
                  pl.BlockSpec(memory_space=pl.ANY)],
        out_specs=pl.BlockSpec(memory_space=pl.ANY),
        out_shape=jax.ShapeDtypeStruct((m_pad, sub, lanes), hm3.dtype),
        scratch_shapes=[pltpu.SemaphoreType.DMA],
        input_output_aliases={2: 0},
        compiler_params=_params(("arbitrary",)),
        name="moe_permute",
    )(pos_flat, hm3, init)


def _unpermute_kernel(pos_ref, ys_ref, out_ref, sem, *, tm):
    def copy(t, k):
        return pltpu.make_async_copy(ys_ref.at[pos_ref[t * TOP_K + k]], out_ref.at[k, t], sem)

    def issue(t, c):
        for k in range(TOP_K):
            copy(t, k).start()
        return c

    def drain(t, c):
        for k in range(TOP_K):
            copy(t, k).wait()
        return c

    lax.fori_loop(0, tm, issue, 0)
    lax.fori_loop(0, tm, drain, 0)


def _unpermute(ys3, pos_flat, n, tm=PERM_TM):
    _, sub, lanes = ys3.shape
    return pl.pallas_call(
        functools.partial(_unpermute_kernel, tm=tm),
        grid=(n // tm,),
        in_specs=[pl.BlockSpec((tm * TOP_K,), lambda i: (i,), memory_space=pltpu.SMEM),
                  pl.BlockSpec(memory_space=pl.ANY)],
        out_specs=pl.BlockSpec((TOP_K, tm, sub, lanes), lambda i: (0, i, 0, 0)),
        out_shape=jax.ShapeDtypeStruct((TOP_K, n, sub, lanes), ys3.dtype),
        scratch_shapes=[pltpu.SemaphoreType.DMA],
        compiler_params=_params(("arbitrary",)),
        name="moe_unpermute",
    )(pos_flat, ys3)


def _expert_kernel(te_ref, nu_ref, xs_ref, wgu_ref, bgu_ref, wdn_ref, bdn_ref, ys_ref, wgu_bf, wdn_bf, *, d):
    j = pl.program_id(0)
    e = te_ref[j]
    prev = te_ref[jnp.maximum(j - 1, 0)]

    @pl.when(jnp.logical_or(j == 0, e != prev))
    def _():
        rows = 128

        def cast(c, carry):
            r0 = pl.multiple_of(c * rows, rows)
            wgu_bf[pl.ds(r0, rows), :] = wgu_ref[pl.ds(r0, rows), :].astype(BF16)
            wdn_bf[pl.ds(r0, rows), :] = wdn_ref[pl.ds(r0, rows), :].astype(BF16)
            return carry

        lax.fori_loop(0, d // rows, cast, 0)

    @pl.when(j < nu_ref[0])
    def _():
        gu = jnp.dot(xs_ref[...], wgu_bf[...], preferred_element_type=F32) + bgu_ref[...]
        g = jnp.minimum(gu[:, :d], SWIGLU_LIMIT)
        u = jnp.clip(gu[:, d:], -SWIGLU_LIMIT, SWIGLU_LIMIT)
        act = (u + 1.0) * (g * _sigmoid(SWIGLU_ALPHA * g))
        y = jnp.dot(act.astype(BF16), wdn_bf[...], preferred_element_type=F32) + bdn_ref[...]
        ys_ref[...] = y.astype(BF16)

    @pl.when(j >= nu_ref[0])
    def _():
        ys_ref[...] = jnp.zeros_like(ys_ref)


def _experts(xs, tile_expert, n_used, w_gu, b_gu, w_dn, b_dn, tm=EXPERT_TM):
    m_pad, d = xs.shape
    ne = w_gu.shape[0]
    rows = lambda j, te, nu: (jnp.minimum(j, nu[0] - 1), 0)
    wsel = lambda j, te, nu: (te[j], 0, 0)
    grid_spec = pltpu.PrefetchScalarGridSpec(
        num_scalar_prefetch=2,
        grid=(m_pad // tm,),
        in_specs=[pl.BlockSpec((tm, d), rows),
                  pl.BlockSpec((None, d, 2 * d), wsel),
                  pl.BlockSpec((None, 1, 2 * d), wsel),
                  pl.BlockSpec((None, d, d), wsel),
                  pl.BlockSpec((None, 1, d), wsel)],
        out_specs=pl.BlockSpec((tm, d), lambda j, te, nu: (j, 0)),
        scratch_shapes=[pltpu.VMEM((d, 2 * d), BF16), pltpu.VMEM((d, d), BF16)],
    )
    return pl.pallas_call(
        functools.partial(_expert_kernel, d=d),
        grid_spec=grid_spec,
        out_shape=jax.ShapeDtypeStruct((m_pad, d), BF16),
        compiler_params=_params(("arbitrary",)),
        name="moe_experts",
    )(tile_expert, n_used, xs, w_gu, b_gu.reshape(ne, 1, 2 * d), w_dn, b_dn.reshape(ne, 1, d))


def _combine_kernel(x_ref, y0_ref, y1_ref, y2_ref, y3_ref, tw_ref, mod_ref, lng_ref, lnb_ref, o_ref, *, d, alpha):
    tw = tw_ref[...]
    m = tw[:, 0:1] * y0_ref[...].astype(F32)
    m = m + tw[:, 1:2] * y1_ref[...].astype(F32)
    m = m + tw[:, 2:3] * y2_ref[...].astype(F32)
    m = m + tw[:, 3:4] * y3_ref[...].astype(F32)
    gate = mod_ref[...][:, 2 * d:]
    o_ref[...] = _layer_norm(alpha * x_ref[...] + (1.0 + gate) * m, lng_ref[...], lnb_ref[...])


def _combine(x, y4, tw, mod, ln_g, ln_b, tm, row0, alpha):
    bsz, s, d = x.shape
    nt = s // tm
    off = row0 // tm
    tok = pl.BlockSpec((None, tm, d), lambda b, t: (b, t, 0))
    ysp = lambda k: pl.BlockSpec((None, tm, d), lambda b, t, k=k: (k, off + b * nt + t, 0))
    return pl.pallas_call(
        functools.partial(_combine_kernel, d=d, alpha=alpha),
        grid=(bsz, nt),
        in_specs=[tok, ysp(0), ysp(1), ysp(2), ysp(3),
                  pl.BlockSpec((tm, LANES), lambda b, t: (off + b * nt + t, 0)),
                  pl.BlockSpec((None, 1, 3 * d), lambda b, t: (b, 0, 0)),
                  _full((1, d)), _full((1, d))],
        out_specs=tok,
        out_shape=jax.ShapeDtypeStruct((bsz, s, d), F32),
        compiler_params=_params(("arbitrary", "arbitrary")),
        name="moe_combine",
    )(x, y4, y4, y4, y4, tw, mod, ln_g, ln_b)


def _moe(hm_p, hm_s, w_r, b_r, w_gu, b_gu, w_dn, b_dn):
    d = hm_p.shape[-1]
    hm = jnp.concatenate([hm_p.reshape(-1, d), hm_s.reshape(-1, d)], axis=0)
    n = hm.shape[0]
    ne = w_r.shape[1]
    eid, tw, rank, cnt = _router(hm, w_r.astype(BF16), b_r.reshape(1, ne))
    tm = EXPERT_TM
    cnt = cnt.reshape(ne).astype(jnp.int32)
    gsz = ((cnt + tm - 1) // tm) * tm
    ends = jnp.cumsum(gsz)
    offs = ends - gsz
    pos = (offs[eid[:, :TOP_K]] + rank[:, :TOP_K]).reshape(-1)
    n_tiles = (n * TOP_K + ne * (tm - 1)) // tm + 1
    m_pad = n_tiles * tm
    tile_start = jnp.arange(n_tiles, dtype=jnp.int32) * tm
    tile_expert = jnp.minimum(jnp.searchsorted(ends, tile_start, side="right"), ne - 1).astype(jnp.int32)
    n_used = (ends[-1] // tm).astype(jnp.int32).reshape(1)
    xs = _permute(hm.reshape(n, d // LANES, LANES), pos, m_pad)
    ys = _experts(xs.reshape(m_pad, d), tile_expert, n_used, w_gu, b_gu, w_dn, b_dn)
    y4 = _unpermute(ys.reshape(m_pad, d // LANES, LANES), pos, n)
    return y4.reshape(TOP_K, n, d), tw


def _qkv_kernel(x_ref, modk_ref, modq_ref, cos_ref, sin_ref, wdkv_ref, gkv_ref, wdq_ref, gq_ref, wuq_ref, wuk_ref,
                ckv_ref, kpe_ref, kc_ref, kp_ref, ql_ref, qp_ref, *, d, kvl, rope, nh, nope, sm_scale):
    x = x_ref[...]
    cos = cos_ref[...]
    sin = sin_ref[...]
    modk = modk_ref[...]
    hk = (x * (1.0 + modk[:, d:2 * d]) + modk[:, :d]).astype(BF16)
    kv = jnp.dot(hk, wdkv_ref[...], preferred_element_type=F32)
    c = kv[:, :kvl]
    ckv = c * lax.rsqrt(jnp.mean(c * c, axis=-1, keepdims=True) + RMS_EPS) * gkv_ref[...]
    kp = kv[:, kvl:kvl + LANES] * cos + kv[:, kvl + LANES:kvl + 2 * LANES] * sin
    ckv_ref[...] = ckv
    kpe_ref[...] = kp[:, :rope]
    kc_ref[...] = ckv.astype(BF16)
    kp_ref[...] = kp[:, :rope].astype(BF16)

    modq = modq_ref[...]
    hq = (x * (1.0 + modq[:, d:2 * d]) + modq[:, :d]).astype(BF16)
    qd = jnp.dot(hq, wdq_ref[...], preferred_element_type=F32)
    qn = (qd * lax.rsqrt(jnp.mean(qd * qd, axis=-1, keepdims=True) + RMS_EPS) * gq_ref[...]).astype(BF16)
    q = jnp.dot(qn, wuq_ref[...], preferred_element_type=F32)
    for h in range(nh):
        qh = q[:, h * nope:(h + 1) * nope].astype(BF16)
        ql = jnp.dot(qh, wuk_ref[h], preferred_element_type=F32)
        ql_ref[h] = (ql * sm_scale).astype(BF16)
    pe0 = nh * nope
    sw0 = pe0 + nh * rope
    per = LANES // rope
    for j in range(nh // per):
        r2 = (q[:, pe0 + j * LANES:pe0 + (j + 1) * LANES] * cos
              + q[:, sw0 + j * LANES:sw0 + (j + 1) * LANES] * sin) * sm_scale
        for i in range(per):
            qp_ref[j * per + i] = r2[:, i * rope:(i + 1) * rope].astype(BF16)


def _qkv(x, modk, modq, cos_t, sin_t, w, tm, dims):
    bsz, s, d = x.shape
    kvl, rope, nh, nope = dims["kvl"], dims["rope"], dims["nh"], dims["nope"]
    tok = lambda width: pl.BlockSpec((None, tm, width), lambda b, t: (b, t, 0))
    head = lambda width: pl.BlockSpec((None, nh, tm, width), lambda b, t: (b, 0, t, 0))
    tab = pl.BlockSpec((tm, LANES), lambda b, t: (t, 0))
    weights = [w["w_dkv"], w["g_kv"], w["w_dq"], w["g_q"], w["w_uq"], w["w_uk"]]
    kern = functools.partial(_qkv_kernel, d=d, kvl=kvl, rope=rope, nh=nh, nope=nope, sm_scale=dims["sm_scale"])
    return pl.pallas_call(
        kern,
        grid=(bsz, s // tm),
        in_specs=[tok(d),
                  pl.BlockSpec((None, 1, 2 * d), lambda b, t: (b, 0, 0)),
                  pl.BlockSpec((None, 1, 3 * d), lambda b, t: (b, 0, 0)),
                  tab, tab] + [_full(a.shape) for a in weights],
        out_specs=[tok(kvl), tok(rope), tok(kvl), tok(rope), head(kvl), head(rope)],
        out_shape=[jax.ShapeDtypeStruct((bsz, s, kvl), F32), jax.ShapeDtypeStruct((bsz, s, rope), F32),
                   jax.ShapeDtypeStruct((bsz, s, kvl), BF16), jax.ShapeDtypeStruct((bsz, s, rope), BF16),
                   jax.ShapeDtypeStruct((bsz, nh, s, kvl), BF16), jax.ShapeDtypeStruct((bsz, nh, s, rope), BF16)],
        compiler_params=_params(("arbitrary", "arbitrary")),
        name="mla_qkv",
    )(x, modk, modq, cos_t, sin_t, *weights)


def _attn_kernel(qi_ref, kj_ref, last_ref, ql_ref, qp_ref, kc_ref, kp_ref, o_ref, m_s, l_s, acc_s,
                 *, nh, tq, tk, kvl, causal):
    s_id = pl.program_id(1)
    qi = qi_ref[s_id]
    kj = kj_ref[s_id]

    @pl.when(kj == 0)
    def _():
        m_s[...] = jnp.full_like(m_s, -jnp.inf)
        l_s[...] = jnp.zeros_like(l_s)
        acc_s[...] = jnp.zeros_like(acc_s)

    ql = ql_ref[...].reshape(nh * tq, kvl)
    qp = qp_ref[...].reshape(nh * tq, qp_ref.shape[-1])
    kc = kc_ref[...]
    nt = (((1,), (1,)), ((), ()))
    s = (lax.dot_general(ql, kc, nt, preferred_element_type=F32)
         + lax.dot_general(qp, kp_ref[...], nt, preferred_element_type=F32))
    if causal:
        qc = (qi * tq + lax.broadcasted_iota(jnp.int32, (tq, tk), 0)) // CHUNK
        kcn = (kj * tk + lax.broadcasted_iota(jnp.int32, (tq, tk), 1)) // CHUNK
        s = jnp.where((kcn <= qc)[None], s.reshape(nh, tq, tk), -jnp.inf).reshape(nh * tq, tk)
    m_old = m_s[...]
    m_new = jnp.maximum(m_old, jnp.max(s, axis=-1, keepdims=True))
    alpha = jnp.exp(m_old - m_new)
    p = jnp.exp(s - m_new)
    l_s[...] = alpha * l_s[...] + jnp.sum(p, axis=-1, keepdims=True)
    acc_s[...] = alpha * acc_s[...] + jnp.dot(p.astype(BF16), kc, preferred_element_type=F32)
    m_s[...] = m_new

    @pl.when(last_ref[s_id] == 1)
    def _():
        o_ref[...] = (acc_s[...] / l_s[...]).reshape(nh, tq, kvl).astype(BF16)


def _attention(ql, qp, kc, kp, tq, tk, causal):
    bsz, nh, s, kvl = ql.shape
    rope = qp.shape[-1]
    skv = kc.shape[1]
    qi, kj, last = [], [], []
    for i in range(s // tq):
        hi = ((i * tq + tq - 1) // tk) if causal else (skv // tk - 1)
        for j in range(hi + 1):
            qi.append(i)
            kj.append(j)
            last.append(1 if j == hi else 0)
    tabs = [jnp.asarray(np.array(a, np.int32)) for a in (qi, kj, last)]
    qspec = lambda width: pl.BlockSpec((None, nh, tq, width), lambda b, t, qi, kj, last: (b, 0, qi[t], 0))
    kspec = lambda width: pl.BlockSpec((None, tk, width), lambda b, t, qi, kj, last: (b, kj[t], 0))
    grid_spec = pltpu.PrefetchScalarGridSpec(
        num_scalar_prefetch=3,
        grid=(bsz, len(qi)),
        in_specs=[qspec(kvl), qspec(rope), kspec(kvl), kspec(rope)],
        out_specs=qspec(kvl),
        scratch_shapes=[pltpu.VMEM((nh * tq, 1), F32), pltpu.VMEM((nh * tq, 1), F32),
                        pltpu.VMEM((nh * tq, kvl), F32)],
    )
    kern = functools.partial(_attn_kernel, nh=nh, tq=tq, tk=tk, kvl=kvl, causal=causal)
    return pl.pallas_call(
        kern,
        grid_spec=grid_spec,
        out_shape=jax.ShapeDtypeStruct((bsz, nh, s, kvl), BF16),
        compiler_params=_params(("arbitrary", "arbitrary")),
        name="mla_attention",
    )(*tabs, ql, qp, kc, kp)


def _attn_out_kernel(o_ref, x_ref, mod_ref, mod2_ref, wuv_ref, wo_ref, lng_ref, lnb_ref, x3_ref, hm_ref,
                     *, d, nh, alpha):
    parts = [jnp.dot(o_ref[h], wuv_ref[h], preferred_element_type=F32) for h in range(nh)]
    o = jnp.concatenate(parts, axis=1).astype(BF16)
    m = jnp.dot(o, wo_ref[...], preferred_element_type=F32)
    gate = mod_ref[...][:, 2 * d:]
    x3 = _layer_norm(alpha * x_ref[...] + (1.0 + gate) * m, lng_ref[...], lnb_ref[...])
    x3_ref[...] = x3
    mod2 = mod2_ref[...]
    hm_ref[...] = (x3 * (1.0 + mod2[:, d:2 * d]) + mod2[:, :d]).astype(BF16)


def _attn_out(o_lat, x, mod, mod2, w_uv, w_o, ln_g, ln_b, tm, alpha):
    bsz, s, d = x.shape
    nh, kvl = o_lat.shape[1], o_lat.shape[3]
    tok = pl.BlockSpec((None, tm, d), lambda b, t: (b, t, 0))
    modspec = pl.BlockSpec((None, 1, 3 * d), lambda b, t: (b, 0, 0))
    return pl.pallas_call(
        functools.partial(_attn_out_kernel, d=d, nh=nh, alpha=alpha),
        grid=(bsz, s // tm),
        in_specs=[pl.BlockSpec((None, nh, tm, kvl), lambda b, t: (b, 0, t, 0)), tok, modspec, modspec,
                  _full(w_uv.shape), _full(w_o.shape), _full((1, d)), _full((1, d))],
        out_specs=[tok, tok],
        out_shape=[jax.ShapeDtypeStruct((bsz, s, d), F32), jax.ShapeDtypeStruct((bsz, s, d), BF16)],
        compiler_params=_params(("arbitrary", "arbitrary")),
        name="mla_out",
    )(o_lat, x, mod, mod2, w_uv, w_o, ln_g, ln_b)


def _rope_tables(pos, rope):
    half = rope // 2
    inv = ROPE_THETA ** (-2.0 * jnp.arange(half, dtype=F32) / rope)
    ang = pos.astype(F32)[:, None] * inv[None, :]
    cos, sin = jnp.cos(ang), jnp.sin(ang)
    rep = LANES // rope
    return (jnp.concatenate([cos, cos] * rep, axis=1), jnp.concatenate([-sin, sin] * rep, axis=1))


def kernel(x_prompt, x_sample, state_conv, state_rglru, cache_ckv, cache_kpe, c_prompt, c_sample, w_ada, b_ada, ln_g, ln_b, lru_w_in, lru_b_in, lru_conv_w, lru_conv_b, lru_w_a, lru_b_a, lru_w_i, lru_b_i, lru_lambda, lru_w_out, lru_b_out, kv_w_ada, kv_b_ada, mla_w_dkv, mla_g_kv, mla_w_uk, mla_w_uv, mla_w_dq, mla_g_q, mla_w_uq, mla_w_o, moe_w_r, moe_b_r, moe_w_gu, moe_b_gu, moe_w_dn, moe_b_dn):
    bp, sp, d = x_prompt.shape
    bs, ss, _ = x_sample.shape
    depth = w_ada.shape[0]
    alpha = float((2.0 * depth) ** 0.25)
    nh, nope, kvl = mla_w_uk.shape
    rope = cache_kpe.shape[-1]
    past = cache_ckv.shape[1]
    hist = state_conv.shape[2]
    dims = dict(kvl=kvl, rope=rope, nh=nh, nope=nope, sm_scale=float((nope + rope) ** -0.5))
    row = lambda v: v.reshape(1, -1)

    nrow = bp + bs
    rpad = -nrow % (2 * SUBLANES)
    c_rows = jnp.concatenate([c_prompt, c_sample, jnp.zeros((rpad, d), F32)], axis=0)
    mods = _ada(c_rows, w_ada.reshape(depth * 2, d, 3 * d), b_ada.reshape(depth * 2, 1, 3 * d))
    modkv = _ada(c_rows, kv_w_ada.reshape(1, d, 2 * d), kv_b_ada.reshape(1, 1, 2 * d))[0]
    mod_p = lambda i: mods[i, :bp].reshape(bp, 1, 3 * d)
    mod_s = lambda i: mods[i, bp:nrow].reshape(bs, 1, 3 * d)

    l0w = dict(w_in=lru_w_in[0].astype(BF16), b_in=row(lru_b_in[0]), conv_w=lru_conv_w[0], conv_b=row(lru_conv_b[0]),
               w_a=lru_w_a[0].astype(BF16), b_a=row(lru_b_a[0]), w_i=lru_w_i[0].astype(BF16), b_i=row(lru_b_i[0]),
               lam=row(lru_lambda[0]), w_out=lru_w_out[0].astype(BF16), b_out=row(lru_b_out[0]),
               ln_g=row(ln_g[0, 0]), ln_b=row(ln_b[0, 0]))
    x1p, hmp, conv_p, h_p = _l0_mixer(x_prompt, mod_p(0), mod_p(1), jnp.zeros((bp, hist, d), F32),
                                      jnp.zeros((bp, 1, d), F32), l0w, 256, alpha)
    x1s, hms, conv_s, h_s = _l0_mixer(x_sample, mod_s(0), mod_s(1), state_conv[0], state_rglru[0].reshape(bs, 1, d),
                                      l0w, ss, alpha)

    y4, tw = _moe(hmp, hms, moe_w_r[0], moe_b_r[0], moe_w_gu[0], moe_b_gu[0], moe_w_dn[0], moe_b_dn[0])
    x2p = _combine(x1p, y4, tw, mod_p(1), row(ln_g[0, 1]), row(ln_b[0, 1]), 512, 0, alpha)
    x2s = _combine(x1s, y4, tw, mod_s(1), row(ln_g[0, 1]), row(ln_b[0, 1]), ss, bp * sp, alpha)

    pe_cols = jnp.arange(rope)
    sw_cols = jnp.concatenate([pe_cols[rope // 2:], pe_cols[:rope // 2]])
    zpad = jnp.zeros((d, LANES - rope), F32)
    w_dkv_ext = jnp.concatenate([mla_w_dkv[:, :kvl], mla_w_dkv[:, kvl:], zpad,
                                 mla_w_dkv[:, kvl:][:, sw_cols], zpad], axis=1).astype(BF16)
    wq = mla_w_uq[0].reshape(-1, nh, nope + rope)
    w_uq_ext = jnp.concatenate([wq[:, :, :nope].reshape(-1, nh * nope), wq[:, :, nope:].reshape(-1, nh * rope),
                                wq[:, :, nope:][:, :, sw_cols].reshape(-1, nh * rope)], axis=1).astype(BF16)
    qw = dict(w_dkv=w_dkv_ext, g_kv=row(mla_g_kv), w_dq=mla_w_dq[0].astype(BF16), g_q=row(mla_g_q[0]),
              w_uq=w_uq_ext, w_uk=mla_w_uk.astype(BF16))
    cos_p, sin_p = _rope_tables(jnp.arange(sp), rope)
    cos_s, sin_s = _rope_tables(past + jnp.arange(ss), rope)
    modkv_p = modkv[:bp].reshape(bp, 1, 2 * d)
    modkv_s = modkv[bp:nrow].reshape(bs, 1, 2 * d)
    ckv_p, kpe_p, kc_p, kp_p, ql_p, qp_p = _qkv(x2p, modkv_p, mod_p(2), cos_p, sin_p, qw, 512, dims)
    ckv_s, kpe_s, kc_s, kp_s, ql_s, qp_s = _qkv(x2s, modkv_s, mod_s(2), cos_s, sin_s, qw, ss, dims)

    o_p = _attention(ql_p, qp_p, kc_p, kp_p, ATT_TQ, ATT_TK, True)
    kc_all = jnp.concatenate([cache_ckv.astype(BF16), kc_s], axis=1)
    kp_all = jnp.concatenate([cache_kpe.astype(BF16), kp_s], axis=1)
    o_s = _attention(ql_s, qp_s, kc_all, kp_all, ss, past + ss, False)
    w_uv = mla_w_uv.astype(BF16)
    w_o = mla_w_o[0].astype(BF16)
    x3p, hm3p = _attn_out(o_p, x2p, mod_p(2), mod_p(3), w_uv, w_o, row(ln_g[1, 0]), row(ln_b[1, 0]), 512, alpha)
    x3s, hm3s = _attn_out(o_s, x2s, mod_s(2), mod_s(3), w_uv, w_o, row(ln_g[1, 0]), row(ln_b[1, 0]), ss, alpha)

    y4b, twb = _moe(hm3p, hm3s, moe_w_r[1], moe_b_r[1], moe_w_gu[1], moe_b_gu[1], moe_w_dn[1], moe_b_dn[1])
    y_p = _combine(x3p, y4b, twb, mod_p(3), row(ln_g[1, 1]), row(ln_b[1, 1]), 512, 0, alpha)
    y_s = _combine(x3s, y4b, twb, mod_s(3), row(ln_g[1, 1]), row(ln_b[1, 1]), ss, bp * sp, alpha)

    return (y_p, y_s, conv_p[None], h_p.reshape(1, bp, d), ckv_p, kpe_p,
            conv_s[None], h_s.reshape(1, bs, d), ckv_s, kpe_s)
```

```python
import functools

import numpy as np
import jax
import jax.numpy as jnp
from jax import lax
from jax.experimental import pallas as pl
from jax.experimental.pallas import tpu as pltpu

F32 = jnp.float32
BF16 = jnp.bfloat16
U32 = jnp.uint32

CHUNK = 64
N_LRU_BLOCKS = 8
LRU_C = 8.0
ROPE_THETA = 10000.0
TOP_K = 4
SWIGLU_ALPHA = 1.702
SWIGLU_LIMIT = 7.0
LN_EPS = 1e-5
RMS_EPS = 1e-6
LOG2E = 1.4426950408889634

LANES = 128
SUBLANES = 8
VMEM_LIMIT = 56 * 1024 * 1024

PROMPT_TM = 512
L0_TB = 256
EXPERT_TM = 256
PERM_TM = 256
ATT_TQ = 256
ATT_TK = 1024


def _sigmoid(x):
    return 1.0 / (1.0 + jnp.exp(-x))


def _layer_norm(v, g, b):
    mu = jnp.mean(v, axis=-1, keepdims=True)
    d = v - mu
    var = jnp.mean(d * d, axis=-1, keepdims=True)
    return d * lax.rsqrt(var + LN_EPS) * g + b


def _pack_pairs(x):
    w = x.shape[1] // 2
    hi = lax.bitcast_convert_type(x[:, :w].astype(BF16).astype(F32), U32)
    lo = lax.bitcast_convert_type(x[:, w:].astype(BF16).astype(F32), U32)
    return hi | (lo >> 16)


def _unpack_pairs(words):
    hi = lax.bitcast_convert_type(words & jnp.uint32(0xFFFF0000), F32)
    lo = lax.bitcast_convert_type(words << 16, F32)
    return hi, lo


def _unpack_bf16(words):
    hi, lo = _unpack_pairs(words)
    return jnp.concatenate([hi.astype(BF16), lo.astype(BF16)], axis=1)


def _full(shape):
    nd = len(shape)
    return pl.BlockSpec(shape, lambda *_: (0,) * nd)


def _params(sem, vmem=VMEM_LIMIT):
    return pltpu.CompilerParams(dimension_semantics=sem, vmem_limit_bytes=vmem)


def _ada_kernel(c_ref, w_ref, b_ref, o_ref):
    c = c_ref[...]
    s = (c * _sigmoid(c)).astype(BF16)
    o_ref[...] = jnp.dot(s, w_ref[...].astype(BF16), preferred_element_type=F32) + b_ref[...]


def _ada(c_rows, w, b, tn=1024):
    g, d, n = w.shape
    r = c_rows.shape[0]
    return pl.pallas_call(
        _ada_kernel,
        grid=(g, n // tn),
        in_specs=[_full((r, d)),
                  pl.BlockSpec((None, d, tn), lambda i, j: (i, 0, j)),
                  pl.BlockSpec((None, 1, tn), lambda i, j: (i, 0, j))],
        out_specs=pl.BlockSpec((None, r, tn), lambda i, j: (i, 0, j)),
        out_shape=jax.ShapeDtypeStruct((g, r, n), F32),
        compiler_params=_params(("arbitrary", "arbitrary")),
        name="ada_mod",
    )(c_rows, w, b)


def _l0_kernel(x_ref, mod_ref, mod2_ref, conv0_ref, h0_ref, win_ref, bin_ref, cw_ref, cb_ref,
               wa_ref, ba_ref, wi_ref, bi_ref, lam_ref, wout_ref, bout_ref, lng_ref, lnb_ref,
               x1_ref, hm_ref, convn_ref, hlast_ref,
               ubuf, a_s, b_s, h_s, *, tb, d, alpha):
    t = pl.program_id(1)
    pad = SUBLANES
    hist = convn_ref.shape[0]

    @pl.when(t == 0)
    def _():
        ubuf[pad - hist:pad, :] = conv0_ref[...]
        h_s[...] = h0_ref[...]

    x = x_ref[...]
    mod = mod_ref[...]
    shift, scale, gate = mod[:, :d], mod[:, d:2 * d], mod[:, 2 * d:]
    h = (x * (1.0 + scale) + shift).astype(BF16)
    proj = jnp.dot(h, win_ref[...], preferred_element_type=F32) + bin_ref[...]
    gate_b = proj[:, :d]
    ubuf[pad:pad + tb, :] = proj[:, d:]

    cw = cw_ref[...]
    u = cb_ref[...]
    for k in range(hist + 1):
        u = u + ubuf[pad - hist + k:pad - hist + k + tb, :] * cw[k:k + 1, :]
    ubuf[pad - hist:pad, :] = ubuf[pad + tb - hist:pad + tb, :]

    ub = u.astype(BF16)
    blk = d // N_LRU_BLOCKS
    lam = lam_ref[...]
    neg = -lam
    softplus = jnp.maximum(neg, 0.0) + jnp.log1p(jnp.exp(-jnp.abs(neg)))
    for n in range(N_LRU_BLOCKS):
        sl = slice(n * blk, (n + 1) * blk)
        un = ub[:, sl]
        r = _sigmoid(jnp.dot(un, wa_ref[n], preferred_element_type=F32) + ba_ref[:, sl])
        ig = _sigmoid(jnp.dot(un, wi_ref[n], preferred_element_type=F32) + bi_ref[:, sl])
        log_a = -LRU_C * r * softplus[:, sl]
        one_m_a2 = -jnp.tanh(log_a) * (jnp.exp(2.0 * log_a) + 1.0)
        a_s[:, sl] = jnp.exp(log_a)
        b_s[:, sl] = jnp.sqrt(one_m_a2) * (ig * u[:, sl])

    row = lax.broadcasted_iota(jnp.int32, (SUBLANES, d), 0)

    def group(g, hprev):
        r0 = pl.multiple_of(g * SUBLANES, SUBLANES)
        a = a_s[pl.ds(r0, SUBLANES), :]
        b = b_s[pl.ds(r0, SUBLANES), :]
        for sh in (1, 2, 4):
            keep = row >= sh
            a_sh = pltpu.roll(a, sh, axis=0)
            b_sh = pltpu.roll(b, sh, axis=0)
            b = jnp.where(keep, a * b_sh + b, b)
            a = jnp.where(keep, a * a_sh, a)
        hh = a * hprev + b
        b_s[pl.ds(r0, SUBLANES), :] = hh
        return hh[SUBLANES - 1:SUBLANES, :]

    h_s[...] = lax.fori_loop(0, tb // SUBLANES, group, h_s[...])

    y = b_s[...]
    gl = 0.5 * gate_b * (1.0 + jnp.tanh(0.7978845608028654 * (gate_b + 0.044715 * gate_b * gate_b * gate_b)))
    out = jnp.dot((gl * y).astype(BF16), wout_ref[...], preferred_element_type=F32) + bout_ref[...]
    x1 = _layer_norm(alpha * x + (1.0 + gate) * out, lng_ref[...], lnb_ref[...])
    x1_ref[...] = x1
    mod2 = mod2_ref[...]
    hm_ref[...] = _pack_pairs(x1 * (1.0 + mod2[:, d:2 * d]) + mod2[:, :d])

    @pl.when(t == pl.num_programs(1) - 1)
    def _():
        convn_ref[...] = ubuf[pad - hist:pad, :]
        hlast_ref[...] = h_s[...]


def _l0_mixer(x, mod, mod2, conv0, h0, w, tb, alpha):
    bsz, s, d = x.shape
    hist = conv0.shape[1]
    tok = lambda width: pl.BlockSpec((None, tb, width), lambda b, t: (b, t, 0))
    per_b = lambda rows, width: pl.BlockSpec((None, rows, width), lambda b, t: (b, 0, 0))
    kern = functools.partial(_l0_kernel, tb=tb, d=d, alpha=alpha)
    weights = [w["w_in"], w["b_in"], w["conv_w"], w["conv_b"], w["w_a"], w["b_a"], w["w_i"], w["b_i"],
               w["lam"], w["w_out"], w["b_out"], w["ln_g"], w["ln_b"]]
    return pl.pallas_call(
        kern,
        grid=(bsz, s // tb),
        in_specs=[tok(d), per_b(1, 3 * d), per_b(1, 3 * d), per_b(hist, d), per_b(1, d)]
                 + [_full(a.shape) for a in weights],
        out_specs=[tok(d), tok(d // 2), per_b(hist, d), per_b(1, d)],
        out_shape=[jax.ShapeDtypeStruct((bsz, s, d), F32), jax.ShapeDtypeStruct((bsz, s, d // 2), U32),
                   jax.ShapeDtypeStruct((bsz, hist, d), F32), jax.ShapeDtypeStruct((bsz, 1, d), F32)],
        scratch_shapes=[pltpu.VMEM((tb + SUBLANES, d), F32), pltpu.VMEM((tb, d), F32),
                        pltpu.VMEM((tb, d), F32), pltpu.VMEM((1, d), F32)],
        compiler_params=_params(("arbitrary", "arbitrary")),
        name="l0_mixer",
    )(x, mod, mod2, conv0, h0, *weights)


def _router_kernel(hm_ref, wr_ref, br_ref, base0_ref, eid_ref, tw_ref, rank_ref, cnt_ref, base, *, tm, ne):
    i = pl.program_id(0)

    @pl.when(i == 0)
    def _():
        base[...] = base0_ref[...]

    logits = jnp.dot(_unpack_bf16(hm_ref[...]), wr_ref[...], preferred_element_type=F32) + br_ref[...]
    col = lax.broadcasted_iota(jnp.int32, (tm, ne), 1).astype(F32)
    l = logits
    vals, idxs = [], []
    for _ in range(TOP_K):
        m = jnp.max(l, axis=-1, keepdims=True)
        idx = jnp.min(jnp.where(l == m, col, float(ne)), axis=-1, keepdims=True)
        vals.append(m)
        idxs.append(idx)
        l = jnp.where(col == idx, -jnp.inf, l)
    es = [jnp.exp(v - vals[0]) for v in vals]
    tot = es[0] + es[1] + es[2] + es[3]

    ri = lax.broadcasted_iota(jnp.int32, (tm, tm), 0)
    ci = lax.broadcasted_iota(jnp.int32, (tm, tm), 1)
    tri = (ci < ri).astype(BF16)
    run = base[...]
    lane = lax.broadcasted_iota(jnp.int32, (tm, LANES), 1)
    eid_o = jnp.zeros((tm, LANES), F32)
    tw_o = jnp.zeros((tm, LANES), F32)
    rk_o = jnp.zeros((tm, LANES), F32)
    for k in range(TOP_K):
        oh = col == idxs[k]
        ohf = oh.astype(F32)
        before = jnp.dot(tri, ohf.astype(BF16), preferred_element_type=F32)
        rank = jnp.sum(jnp.where(oh, before + run, 0.0), axis=-1, keepdims=True)
        run = run + jnp.sum(ohf, axis=0, keepdims=True)
        sel = lane == k
        eid_o = jnp.where(sel, idxs[k], eid_o)
        tw_o = jnp.where(sel, es[k] / tot, tw_o)
        rk_o = jnp.where(sel, rank, rk_o)
    base[...] = run
    eid_ref[...] = eid_o.astype(jnp.int32)
    tw_ref[...] = tw_o
    rank_ref[...] = rk_o.astype(jnp.int32)
    cnt_ref[...] = run


def _router(hm, w_r, b_r, base0, tm):
    n, half = hm.shape
    ne = w_r.shape[1]
    kern = functools.partial(_router_kernel, tm=tm, ne=ne)
    tokrow = pl.BlockSpec((tm, LANES), lambda i: (i, 0))
    return pl.pallas_call(
        kern,
        grid=(n // tm,),
        in_specs=[pl.BlockSpec((tm, half), lambda i: (i, 0)), _full((2 * half, ne)), _full((1, ne)), _full((1, ne))],
        out_specs=[tokrow, tokrow, tokrow, _full((1, ne))],
        out_shape=[jax.ShapeDtypeStruct((n, LANES), jnp.int32), jax.ShapeDtypeStruct((n, LANES), F32),
                   jax.ShapeDtypeStruct((n, LANES), jnp.int32), jax.ShapeDtypeStruct((1, ne), F32)],
        scratch_shapes=[pltpu.VMEM((1, ne), F32)],
        compiler_params=_params(("arbitrary",)),
        name="moe_router",
    )(hm, w_r, b_r, base0)


def _permute_kernel(pos_ref, hm_ref, init_ref, out_ref, sem, *, tm):
    del init_ref

    def copy(t, k):
        return pltpu.make_async_copy(hm_ref.at[pl.ds(t, 1)], out_ref.at[pl.ds(pos_ref[t * TOP_K + k], 1)], sem)

    def issue(t, c):
        for k in range(TOP_K):
            copy(t, k).start()
        return c

    def drain(t, c):
        for k in range(TOP_K):
            copy(t, k).wait()
        return c

    lax.fori_loop(0, tm, issue, 0)
    lax.fori_loop(0, tm, drain, 0)


def _permute(hm, pos_flat, dest, tm):
    n, w = hm.shape
    return pl.pallas_call(
        functools.partial(_permute_kernel, tm=tm),
        grid=(n // tm,),
        in_specs=[pl.BlockSpec((tm * TOP_K,), lambda i: (i,), memory_space=pltpu.SMEM),
                  pl.BlockSpec((tm, w), lambda i: (i, 0)),
                  pl.BlockSpec(memory_space=pl.ANY)],
        out_specs=pl.BlockSpec(memory_space=pl.ANY),
        out_shape=jax.ShapeDtypeStruct(dest.shape, dest.dtype),
        scratch_shapes=[pltpu.SemaphoreType.DMA],
        input_output_aliases={2: 0},
        compiler_params=_params(("arbitrary",)),
        name="moe_permute",
    )(pos_flat, hm, dest)


def _unpermute_kernel(pos_ref, ys_ref, out_ref, sem, *, tm):
    def copy(t, k):
        return pltpu.make_async_copy(ys_ref.at[pl.ds(pos_ref[t * TOP_K + k], 1)], out_ref.at[k, pl.ds(t, 1)], sem)

    def issue(t, c):
        for k in range(TOP_K):
            copy(t, k).start()
        return c

    def drain(t, c):
        for k in range(TOP_K):
            copy(t, k).wait()
        return c

    lax.fori_loop(0, tm, issue, 0)
    lax.fori_loop(0, tm, drain, 0)


def _unpermute(ys, pos_flat, n, tm):
    _, w = ys.shape
    return pl.pallas_call(
        functools.partial(_unpermute_kernel, tm=tm),
        grid=(n // tm,),
        in_specs=[pl.BlockSpec((tm * TOP_K,), lambda i: (i,), memory_space=pltpu.SMEM),
                  pl.BlockSpec(memory_space=pl.ANY)],
        out_specs=pl.BlockSpec((TOP_K, tm, w), lambda i: (0, i, 0)),
        out_shape=jax.ShapeDtypeStruct((TOP_K, n, w), ys.dtype),
        scratch_shapes=[pltpu.SemaphoreType.DMA],
        compiler_params=_params(("arbitrary",)),
        name="moe_unpermute",
    )(pos_flat, ys)


def _expert_kernel(te_ref, nu_ref, xs_ref, wgu_ref, bgu_ref, wdn_ref, bdn_ref, ys_ref, wgu_bf, wdn_bf, *, d):
    j = pl.program_id(0)
    e = te_ref[j]
    prev = te_ref[jnp.maximum(j - 1, 0)]

    @pl.when(jnp.logical_or(j == 0, e != prev))
    def _():
        rows = 128

        def cast(c, carry):
            r0 = pl.multiple_of(c * rows, rows)
            wgu_bf[pl.ds(r0, rows), :] = wgu_ref[pl.ds(r0, rows), :].astype(BF16)
            wdn_bf[pl.ds(r0, rows), :] = wdn_ref[pl.ds(r0, rows), :].astype(BF16)
            return carry

        lax.fori_loop(0, d // rows, cast, 0)

    @pl.when(j < nu_ref[0])
    def _():
        gu = jnp.dot(_unpack_bf16(xs_ref[...]), wgu_bf[...], preferred_element_type=F32) + bgu_ref[...]
        g = jnp.minimum(gu[:, :d], SWIGLU_LIMIT)
        u = jnp.clip(gu[:, d:], -SWIGLU_LIMIT, SWIGLU_LIMIT)
        act = (u + 1.0) * (g * _sigmoid(SWIGLU_ALPHA * g))
        y = jnp.dot(act.astype(BF16), wdn_bf[...], preferred_element_type=F32) + bdn_ref[...]
        ys_ref[...] = _pack_pairs(y)

    @pl.when(j >= nu_ref[0])
    def _():
        ys_ref[...] = jnp.zeros_like(ys_ref)


def _experts(xs, tile_expert, n_used, w_gu, b_gu, w_dn, b_dn, tm=EXPERT_TM):
    m_pad, half = xs.shape
    d = 2 * half
    ne = w_gu.shape[0]
    rows = lambda j, te, nu: (jnp.minimum(j, nu[0] - 1), 0)
    wsel = lambda j, te, nu: (te[j], 0, 0)
    grid_spec = pltpu.PrefetchScalarGridSpec(
        num_scalar_prefetch=2,
        grid=(m_pad // tm,),
        in_specs=[pl.BlockSpec((tm, half), rows),
                  pl.BlockSpec((None, d, 2 * d), wsel),
                  pl.BlockSpec((None, 1, 2 * d), wsel),
                  pl.BlockSpec((None, d, d), wsel),
                  pl.BlockSpec((None, 1, d), wsel)],
        out_specs=pl.BlockSpec((tm, half), lambda j, te, nu: (j, 0)),
        scratch_shapes=[pltpu.VMEM((d, 2 * d), BF16), pltpu.VMEM((d, d), BF16)],
    )
    return pl.pallas_call(
        functools.partial(_expert_kernel, d=d),
        grid_spec=grid_spec,
        out_shape=jax.ShapeDtypeStruct((m_pad, half), U32),
        compiler_params=_params(("arbitrary",)),
        name="moe_experts",
    )(tile_expert, n_used, xs, w_gu, b_gu.reshape(ne, 1, 2 * d), w_dn, b_dn.reshape(ne, 1, d))


def _combine_kernel(x_ref, y_ref, tw_ref, mod_ref, lng_ref, lnb_ref, o_ref, *, d, alpha):
    tw = tw_ref[...]
    m_hi = None
    m_lo = None
    for k in range(TOP_K):
        hi, lo = _unpack_pairs(y_ref[k])
        wk = tw[:, k:k + 1]
        m_hi = wk * hi if m_hi is None else m_hi + wk * hi
        m_lo = wk * lo if m_lo is None else m_lo + wk * lo
    m = jnp.concatenate([m_hi, m_lo], axis=1)
    gate = mod_ref[...][:, 2 * d:]
    o_ref[...] = _layer_norm(alpha * x_ref[...] + (1.0 + gate) * m, lng_ref[...], lnb_ref[...])


def _combine(x, y4, tw, mod, ln_g, ln_b, tm, alpha):
    bsz, s, d = x.shape
    nt = s // tm
    tok = pl.BlockSpec((None, tm, d), lambda b, t: (b, t, 0))
    return pl.pallas_call(
        functools.partial(_combine_kernel, d=d, alpha=alpha),
        grid=(bsz, nt),
        in_specs=[tok,
                  pl.BlockSpec((TOP_K, tm, d // 2), lambda b, t: (0, b * nt + t, 0)),
                  pl.BlockSpec((tm, LANES), lambda b, t: (b * nt + t, 0)),
                  pl.BlockSpec((None, 1, 3 * d), lambda b, t: (b, 0, 0)),
                  _full((1, d)), _full((1, d))],
        out_specs=tok,
        out_shape=jax.ShapeDtypeStruct((bsz, s, d), F32),
        compiler_params=_params(("arbitrary", "arbitrary")),
        name="moe_combine",
    )(x, y4, tw, mod, ln_g, ln_b)


def _moe(hm_p, hm_s, w_r, b_r, w_gu, b_gu, w_dn, b_dn):
    half = hm_p.shape[-1]
    hm_p = hm_p.reshape(-1, half)
    hm_s = hm_s.reshape(-1, half)
    n_p, n_s = hm_p.shape[0], hm_s.shape[0]
    ne = w_r.shape[1]
    w_r = w_r.astype(BF16)
    b_r = b_r.reshape(1, ne)
    eid_p, tw_p, rank_p, cnt_p = _router(hm_p, w_r, b_r, jnp.zeros((1, ne), F32), PROMPT_TM)
    eid_s, tw_s, rank_s, cnt = _router(hm_s, w_r, b_r, cnt_p, n_s)
    tm = EXPERT_TM
    cnt = cnt.reshape(ne).astype(jnp.int32)
    gsz = ((cnt + tm - 1) // tm) * tm
    ends = jnp.cumsum(gsz)
    offs = ends - gsz
    pos_p = (offs[eid_p[:, :TOP_K]] + rank_p[:, :TOP_K]).reshape(-1)
    pos_s = (offs[eid_s[:, :TOP_K]] + rank_s[:, :TOP_K]).reshape(-1)
    n_tiles = ((n_p + n_s) * TOP_K + ne * (tm - 1)) // tm + 1
    m_pad = n_tiles * tm
    tile_start = jnp.arange(n_tiles, dtype=jnp.int32) * tm
    tile_expert = jnp.minimum(jnp.sum((tile_start[:, None] >= ends[None, :]).astype(jnp.int32), axis=1), ne - 1)
    n_used = (ends[-1] // tm).astype(jnp.int32).reshape(1)
    xs = _permute(hm_p, pos_p, jnp.zeros((m_pad, half), U32), PERM_TM)
    xs = _permute(hm_s, pos_s, xs, n_s)
    ys = _experts(xs, tile_expert, n_used, w_gu, b_gu, w_dn, b_dn)
    return (_unpermute(ys, pos_p, n_p, PERM_TM), tw_p), (_unpermute(ys, pos_s, n_s, n_s), tw_s)


def _qkv_kernel(x_ref, modk_ref, modq_ref, cos_ref, sin_ref, wdkv_ref, gkv_ref, wdq_ref, gq_ref, wuq_ref, wuk_ref,
                ckv_ref, kpe_ref, kc_ref, kp_ref, ql_ref, qp_ref, *, d, kvl, rope, nh, nope, q_scale):
    x = x_ref[...]
    cos = cos_ref[...]
    sin = sin_ref[...]
    modk = modk_ref[...]
    hk = (x * (1.0 + modk[:, d:2 * d]) + modk[:, :d]).astype(BF16)
    kv = jnp.dot(hk, wdkv_ref[...], preferred_element_type=F32)
    c = kv[:, :kvl]
    ckv = c * lax.rsqrt(jnp.mean(c * c, axis=-1, keepdims=True) + RMS_EPS) * gkv_ref[...]
    kp = kv[:, kvl:kvl + LANES] * cos + kv[:, kvl + LANES:kvl + 2 * LANES] * sin
    ckv_ref[...] = ckv
    kpe_ref[...] = kp[:, :rope]
    kc_ref[...] = ckv.astype(BF16)
    kp_ref[...] = kp[:, :rope].astype(BF16)

    modq = modq_ref[...]
    hq = (x * (1.0 + modq[:, d:2 * d]) + modq[:, :d]).astype(BF16)
    qd = jnp.dot(hq, wdq_ref[...], preferred_element_type=F32)
    qn = (qd * lax.rsqrt(jnp.mean(qd * qd, axis=-1, keepdims=True) + RMS_EPS) * gq_ref[...]).astype(BF16)
    q = jnp.dot(qn, wuq_ref[...], preferred_element_type=F32)
    for h in range(nh):
        qh = q[:, h * nope:(h + 1) * nope].astype(BF16)
        ql = jnp.dot(qh, wuk_ref[h], preferred_element_type=F32)
        ql_ref[h] = (ql * q_scale).astype(BF16)
    pe0 = nh * nope
    sw0 = pe0 + nh * rope
    per = LANES // rope
    for j in range(nh // per):
        r2 = (q[:, pe0 + j * LANES:pe0 + (j + 1) * LANES] * cos
              + q[:, sw0 + j * LANES:sw0 + (j + 1) * LANES] * sin) * q_scale
        for i in range(per):
            qp_ref[j * per + i] = r2[:, i * rope:(i + 1) * rope].astype(BF16)


def _qkv(x, modk, modq, cos_t, sin_t, w, tm, dims):
    bsz, s, d = x.shape
    kvl, rope, nh, nope = dims["kvl"], dims["rope"], dims["nh"], dims["nope"]
    tok = lambda width: pl.BlockSpec((None, tm, width), lambda b, t: (b, t, 0))
    head = lambda width: pl.BlockSpec((None, nh, tm, width), lambda b, t: (b, 0, t, 0))
    tab = pl.BlockSpec((tm, LANES), lambda b, t: (t, 0))
    weights = [w["w_dkv"], w["g_kv"], w["w_dq"], w["g_q"], w["w_uq"], w["w_uk"]]
    kern = functools.partial(_qkv_kernel, d=d, kvl=kvl, rope=rope, nh=nh, nope=nope, q_scale=dims["q_scale"])
    return pl.pallas_call(
        kern,
        grid=(bsz, s // tm),
        in_specs=[tok(d),
                  pl.BlockSpec((None, 1, 2 * d), lambda b, t: (b, 0, 0)),
                  pl.BlockSpec((None, 1, 3 * d), lambda b, t: (b, 0, 0)),
                  tab, tab] + [_full(a.shape) for a in weights],
        out_specs=[tok(kvl), tok(rope), tok(kvl), tok(rope), head(kvl), head(rope)],
        out_shape=[jax.ShapeDtypeStruct((bsz, s, kvl), F32), jax.ShapeDtypeStruct((bsz, s, rope), F32),
                   jax.ShapeDtypeStruct((bsz, s, kvl), BF16), jax.ShapeDtypeStruct((bsz, s, rope), BF16),
                   jax.ShapeDtypeStruct((bsz, nh, s, kvl), BF16), jax.ShapeDtypeStruct((bsz, nh, s, rope), BF16)],
        compiler_params=_params(("arbitrary", "arbitrary")),
        name="mla_qkv",
    )(x, modk, modq, cos_t, sin_t, *weights)


def _attn_kernel(qi_ref, kj_ref, last_ref, edge_ref, ql_ref, qp_ref, kc_ref, kp_ref, o_ref, m_s, l_s, acc_s,
                 *, nh, tq, tk, kvl, causal, valid):
    s_id = pl.program_id(1)
    qi = qi_ref[s_id]
    kj = kj_ref[s_id]
    nt = (((1,), (1,)), ((), ()))

    @pl.when(kj == 0)
    def _():
        m_s[...] = jnp.full_like(m_s, -jnp.inf)
        l_s[...] = jnp.zeros_like(l_s)
        acc_s[...] = jnp.zeros_like(acc_s)

    def sweep(masked):
        kc = kc_ref[...]
        kp = kp_ref[...]
        if masked:
            kidx = kj * tk + lax.broadcasted_iota(jnp.int32, (tq, tk), 1)
            if causal:
                qidx = qi * tq + lax.broadcasted_iota(jnp.int32, (tq, tk), 0)
                visible = (kidx // CHUNK) <= (qidx // CHUNK)
            else:
                visible = kidx < valid
        for h in range(nh):
            s = (lax.dot_general(ql_ref[h], kc, nt, preferred_element_type=F32)
                 + lax.dot_general(qp_ref[h], kp, nt, preferred_element_type=F32))
            if masked:
                s = jnp.where(visible, s, -jnp.inf)
            slabs = [s[:, j * LANES:(j + 1) * LANES] for j in range(tk // LANES)]
            mx = slabs[0]
            for sj in slabs[1:]:
                mx = jnp.maximum(mx, sj)
            m_prev = m_s[h]
            m_new = jnp.maximum(m_prev, jnp.max(mx, axis=1, keepdims=True))
            scale = jnp.exp2(m_prev - m_new)
            ps = []
            rsum = None
            for sj in slabs:
                pj = jnp.exp2(sj - m_new)
                ps.append(pj.astype(BF16))
                rsum = pj if rsum is None else rsum + pj
            p = jnp.concatenate(ps, axis=1)
            l_s[h] = scale * l_s[h] + jnp.sum(rsum, axis=1, keepdims=True)
            pv = jnp.dot(p, kc, preferred_element_type=F32)
            acc_s[h] = acc_s[h] * jnp.concatenate([scale] * (kvl // LANES), axis=1) + pv
            m_s[h] = m_new

    edge = edge_ref[s_id]

    @pl.when(edge == 1)
    def _():
        sweep(True)

    @pl.when(edge == 0)
    def _():
        sweep(False)

    @pl.when(last_ref[s_id] == 1)
    def _():
        for h in range(nh):
            inv = jnp.concatenate([l_s[h]] * (kvl // LANES), axis=1)
            o_ref[h] = (acc_s[h] / inv).astype(BF16)


def _attention(ql, qp, kc, kp, tq, tk, causal, valid):
    bsz, nh, s, kvl = ql.shape
    rope = qp.shape[-1]
    skv = kc.shape[1]
    qi, kj, last, edge = [], [], [], []
    for i in range(s // tq):
        hi = ((i * tq + tq - 1) // tk) if causal else (skv // tk - 1)
        for j in range(hi + 1):
            qi.append(i)
            kj.append(j)
            last.append(1 if j == hi else 0)
            if causal:
                edge.append(1 if ((j + 1) * tk - 1) // CHUNK > (i * tq) // CHUNK else 0)
            else:
                edge.append(1 if (j + 1) * tk > valid else 0)
    tabs = [jnp.asarray(np.array(a, np.int32)) for a in (qi, kj, last, edge)]
    qspec = lambda width: pl.BlockSpec((None, nh, tq, width), lambda b, t, qi, kj, last, edge: (b, 0, qi[t], 0))
    kspec = lambda width: pl.BlockSpec((None, tk, width), lambda b, t, qi, kj, last, edge: (b, kj[t], 0))
    grid_spec = pltpu.PrefetchScalarGridSpec(
        num_scalar_prefetch=4,
        grid=(bsz, len(qi)),
        in_specs=[qspec(kvl), qspec(rope), kspec(kvl), kspec(rope)],
        out_specs=qspec(kvl),
        scratch_shapes=[pltpu.VMEM((nh, tq, LANES), F32), pltpu.VMEM((nh, tq, LANES), F32),
                        pltpu.VMEM((nh, tq, kvl), F32)],
    )
    kern = functools.partial(_attn_kernel, nh=nh, tq=tq, tk=tk, kvl=kvl, causal=causal, valid=valid)
    return pl.pallas_call(
        kern,
        grid_spec=grid_spec,
        out_shape=jax.ShapeDtypeStruct((bsz, nh, s, kvl), BF16),
        compiler_params=_params(("arbitrary", "arbitrary")),
        name="mla_attention",
    )(*tabs, ql, qp, kc, kp)


def _attn_out_kernel(o_ref, x_ref, mod_ref, mod2_ref, wuv_ref, wo_ref, lng_ref, lnb_ref, x3_ref, hm_ref,
                     *, d, nh, alpha):
    parts = [jnp.dot(o_ref[h], wuv_ref[h], preferred_element_type=F32) for h in range(nh)]
    o = jnp.concatenate(parts, axis=1).astype(BF16)
    m = jnp.dot(o, wo_ref[...], preferred_element_type=F32)
    gate = mod_ref[...][:, 2 * d:]
    x3 = _layer_norm(alpha * x_ref[...] + (1.0 + gate) * m, lng_ref[...], lnb_ref[...])
    x3_ref[...] = x3
    mod2 = mod2_ref[...]
    hm_ref[...] = _pack_pairs(x3 * (1.0 + mod2[:, d:2 * d]) + mod2[:, :d])


def _attn_out(o_lat, x, mod, mod2, w_uv, w_o, ln_g, ln_b, tm, alpha):
    bsz, s, d = x.shape
    nh, kvl = o_lat.shape[1], o_lat.shape[3]
    tok = lambda width: pl.BlockSpec((None, tm, width), lambda b, t: (b, t, 0))
    modspec = pl.BlockSpec((None, 1, 3 * d), lambda b, t: (b, 0, 0))
    return pl.pallas_call(
        functools.partial(_attn_out_kernel, d=d, nh=nh, alpha=alpha),
        grid=(bsz, s // tm),
        in_specs=[pl.BlockSpec((None, nh, tm, kvl), lambda b, t: (b, 0, t, 0)), tok(d), modspec, modspec,
                  _full(w_uv.shape), _full(w_o.shape), _full((1, d)), _full((1, d))],
        out_specs=[tok(d), tok(d // 2)],
        out_shape=[jax.ShapeDtypeStruct((bsz, s, d), F32), jax.ShapeDtypeStruct((bsz, s, d // 2), U32)],
        compiler_params=_params(("arbitrary", "arbitrary")),
        name="mla_out",
    )(o_lat, x, mod, mod2, w_uv, w_o, ln_g, ln_b)


def _rope_tables(pos, rope):
    half = rope // 2
    inv = ROPE_THETA ** (-2.0 * jnp.arange(half, dtype=F32) / rope)
    ang = pos.astype(F32)[:, None] * inv[None, :]
    cos, sin = jnp.cos(ang), jnp.sin(ang)
    rep = LANES // rope
    return (jnp.concatenate([cos, cos] * rep, axis=1), jnp.concatenate([-sin, sin] * rep, axis=1))


def kernel(x_prompt, x_sample, state_conv, state_rglru, cache_ckv, cache_kpe, c_prompt, c_sample, w_ada, b_ada, ln_g, ln_b, lru_w_in, lru_b_in, lru_conv_w, lru_conv_b, lru_w_a, lru_b_a, lru_w_i, lru_b_i, lru_lambda, lru_w_out, lru_b_out, kv_w_ada, kv_b_ada, mla_w_dkv, mla_g_kv, mla_w_uk, mla_w_uv, mla_w_dq, mla_g_q, mla_w_uq, mla_w_o, moe_w_r, moe_b_r, moe_w_gu, moe_b_gu, moe_w_dn, moe_b_dn):
    bp, sp, d = x_prompt.shape
    bs, ss, _ = x_sample.shape
    depth = w_ada.shape[0]
    alpha = float((2.0 * depth) ** 0.25)
    nh, nope, kvl = mla_w_uk.shape
    rope = cache_kpe.shape[-1]
    past = cache_ckv.shape[1]
    hist = state_conv.shape[2]
    dims = dict(kvl=kvl, rope=rope, nh=nh, nope=nope, q_scale=float((nope + rope) ** -0.5 * LOG2E))
    row = lambda v: v.reshape(1, -1)

    nrow = bp + bs
    rpad = -nrow % (2 * SUBLANES)
    c_rows = jnp.concatenate([c_prompt, c_sample, jnp.zeros((rpad, d), F32)], axis=0)
    mods = _ada(c_rows, w_ada.reshape(depth * 2, d, 3 * d), b_ada.reshape(depth * 2, 1, 3 * d))
    modkv = _ada(c_rows, kv_w_ada.reshape(1, d, 2 * d), kv_b_ada.reshape(1, 1, 2 * d))[0]
    mod_p = lambda i: mods[i, :bp].reshape(bp, 1, 3 * d)
    mod_s = lambda i: mods[i, bp:nrow].reshape(bs, 1, 3 * d)

    l0w = dict(w_in=lru_w_in[0].astype(BF16), b_in=row(lru_b_in[0]), conv_w=lru_conv_w[0], conv_b=row(lru_conv_b[0]),
               w_a=lru_w_a[0].astype(BF16), b_a=row(lru_b_a[0]), w_i=lru_w_i[0].astype(BF16), b_i=row(lru_b_i[0]),
               lam=row(lru_lambda[0]), w_out=lru_w_out[0].astype(BF16), b_out=row(lru_b_out[0]),
               ln_g=row(ln_g[0, 0]), ln_b=row(ln_b[0, 0]))
    x1p, hmp, conv_p, h_p = _l0_mixer(x_prompt, mod_p(0), mod_p(1), jnp.zeros((bp, hist, d), F32),
                                      jnp.zeros((bp, 1, d), F32), l0w, L0_TB, alpha)
    x1s, hms, conv_s, h_s = _l0_mixer(x_sample, mod_s(0), mod_s(1), state_conv[0], state_rglru[0].reshape(bs, 1, d),
                                      l0w, ss, alpha)

    (y4p, twp), (y4s, tws) = _moe(hmp, hms, moe_w_r[0], moe_b_r[0], moe_w_gu[0], moe_b_gu[0], moe_w_dn[0], moe_b_dn[0])
    x2p = _combine(x1p, y4p, twp, mod_p(1), row(ln_g[0, 1]), row(ln_b[0, 1]), PROMPT_TM, alpha)
    x2s = _combine(x1s, y4s, tws, mod_s(1), row(ln_g[0, 1]), row(ln_b[0, 1]), ss, alpha)

    pe_cols = jnp.arange(rope)
    sw_cols = jnp.concatenate([pe_cols[rope // 2:], pe_cols[:rope // 2]])
    zpad = jnp.zeros((d, LANES - rope), F32)
    w_dkv_ext = jnp.concatenate([mla_w_dkv[:, :kvl], mla_w_dkv[:, kvl:], zpad,
                                 mla_w_dkv[:, kvl:][:, sw_cols], zpad], axis=1).astype(BF16)
    wq = mla_w_uq[0].reshape(-1, nh, nope + rope)
    w_uq_ext = jnp.concatenate([wq[:, :, :nope].reshape(-1, nh * nope), wq[:, :, nope:].reshape(-1, nh * rope),
                                wq[:, :, nope:][:, :, sw_cols].reshape(-1, nh * rope)], axis=1).astype(BF16)
    qw = dict(w_dkv=w_dkv_ext, g_kv=row(mla_g_kv), w_dq=mla_w_dq[0].astype(BF16), g_q=row(mla_g_q[0]),
              w_uq=w_uq_ext, w_uk=mla_w_uk.astype(BF16))
    cos_p, sin_p = _rope_tables(jnp.arange(sp), rope)
    cos_s, sin_s = _rope_tables(past + jnp.arange(ss), rope)
    modkv_p = modkv[:bp].reshape(bp, 1, 2 * d)
    modkv_s = modkv[bp:nrow].reshape(bs, 1, 2 * d)
    ckv_p, kpe_p, kc_p, kp_p, ql_p, qp_p = _qkv(x2p, modkv_p, mod_p(2), cos_p, sin_p, qw, PROMPT_TM, dims)
    ckv_s, kpe_s, kc_s, kp_s, ql_s, qp_s = _qkv(x2s, modkv_s, mod_s(2), cos_s, sin_s, qw, ss, dims)

    o_p = _attention(ql_p, qp_p, kc_p, kp_p, ATT_TQ, ATT_TK, True, sp)
    skv = past + ss
    kpad = -skv % LANES
    kc_all = jnp.concatenate([cache_ckv.astype(BF16), kc_s, jnp.zeros((bs, kpad, kvl), BF16)], axis=1)
    kp_all = jnp.concatenate([cache_kpe.astype(BF16), kp_s, jnp.zeros((bs, kpad, rope), BF16)], axis=1)
    o_s = _attention(ql_s, qp_s, kc_all, kp_all, ss, skv + kpad, False, skv)
    w_uv = mla_w_uv.astype(BF16)
    w_o = mla_w_o[0].astype(BF16)
    x3p, hm3p = _attn_out(o_p, x2p, mod_p(2), mod_p(3), w_uv, w_o, row(ln_g[1, 0]), row(ln_b[1, 0]), PROMPT_TM, alpha)
    x3s, hm3s = _attn_out(o_s, x2s, mod_s(2), mod_s(3), w_uv, w_o, row(ln_g[1, 0]), row(ln_b[1, 0]), ss, alpha)

    (y4p, twp), (y4s, tws) = _moe(hm3p, hm3s, moe_w_r[1], moe_b_r[1], moe_w_gu[1], moe_b_gu[1], moe_w_dn[1],
                                  moe_b_dn[1])
    y_p = _combine(x3p, y4p, twp, mod_p(3), row(ln_g[1, 1]), row(ln_b[1, 1]), PROMPT_TM, alpha)
    y_s = _combine(x3s, y4s, tws, mod_s(3), row(ln_g[1, 1]), row(ln_b[1, 1]), ss, alpha)

    return (y_p, y_s, conv_p[None], h_p.reshape(1, bp, d), ckv_p, kpe_p,
            conv_s[None], h_s.reshape(1, bs, d), ckv_s, kpe_s)
```

```python
import functools

import numpy as np
import jax
import jax.numpy as jnp
from jax import lax
from jax.experimental import pallas as pl
from jax.experimental.pallas import tpu as pltpu

F32 = jnp.float32
BF16 = jnp.bfloat16
U32 = jnp.uint32

CHUNK = 64
N_LRU_BLOCKS = 8
LRU_C = 8.0
ROPE_THETA = 10000.0
TOP_K = 4
SWIGLU_ALPHA = 1.702
SWIGLU_LIMIT = 7.0
LN_EPS = 1e-5
RMS_EPS = 1e-6
LOG2E = 1.4426950408889634

LANES = 128
SUBLANES = 8
VMEM_LIMIT = 56 * 1024 * 1024

PROMPT_TM = 512
L0_TB = 256
EXPERT_TM = 512
PERM_TM = 256
ATT_TQ = 512
ATT_TK = 1024


def _sigmoid(x):
    return 0.5 * (jnp.tanh(0.5 * x) + 1.0)


def _layer_norm(v, g, b):
    mu = jnp.mean(v, axis=-1, keepdims=True)
    d = v - mu
    var = jnp.mean(d * d, axis=-1, keepdims=True)
    return d * lax.rsqrt(var + LN_EPS) * g + b


def _pack_pairs(x):
    w = x.shape[1] // 2
    hi = lax.bitcast_convert_type(x[:, :w].astype(BF16).astype(F32), U32)
    lo = lax.bitcast_convert_type(x[:, w:].astype(BF16).astype(F32), U32)
    return hi | (lo >> 16)


def _unpack_pairs(words):
    hi = lax.bitcast_convert_type(words & jnp.uint32(0xFFFF0000), F32)
    lo = lax.bitcast_convert_type(words << 16, F32)
    return hi, lo


def _unpack_bf16(words):
    hi, lo = _unpack_pairs(words)
    return jnp.concatenate([hi.astype(BF16), lo.astype(BF16)], axis=1)


def _full(shape):
    nd = len(shape)
    return pl.BlockSpec(shape, lambda *_: (0,) * nd)


def _params(sem, vmem=VMEM_LIMIT):
    return pltpu.CompilerParams(dimension_semantics=sem, vmem_limit_bytes=vmem)


def _ada_kernel(c_ref, w_ref, b_ref, o_ref):
    c = c_ref[...]
    s = (c * _sigmoid(c)).astype(BF16)
    o_ref[...] = jnp.dot(s, w_ref[...].astype(BF16), preferred_element_type=F32) + b_ref[...]


def _ada(c_rows, w, b, tn=1024):
    g, d, n = w.shape
    r = c_rows.shape[0]
    return pl.pallas_call(
        _ada_kernel,
        grid=(g, n // tn),
        in_specs=[_full((r, d)),
                  pl.BlockSpec((None, d, tn), lambda i, j: (i, 0, j)),
                  pl.BlockSpec((None, 1, tn), lambda i, j: (i, 0, j))],
        out_specs=pl.BlockSpec((None, r, tn), lambda i, j: (i, 0, j)),
        out_shape=jax.ShapeDtypeStruct((g, r, n), F32),
        compiler_params=_params(("arbitrary", "arbitrary")),
        name="ada_mod",
    )(c_rows, w, b)


def _l0_kernel(x_ref, mod_ref, mod2_ref, conv0_ref, h0_ref, win_ref, bin_ref, cw_ref, cb_ref,
               wa_ref, ba_ref, wi_ref, bi_ref, lam_ref, wout_ref, bout_ref, lng_ref, lnb_ref,
               x1_ref, hm_ref, convn_ref, hlast_ref,
               ubuf, a_s, b_s, h_s, *, tb, d, alpha):
    t = pl.program_id(1)
    pad = SUBLANES
    hist = convn_ref.shape[0]

    @pl.when(t == 0)
    def _():
        ubuf[pad - hist:pad, :] = conv0_ref[...]
        h_s[...] = h0_ref[...]

    x = x_ref[...]
    mod = mod_ref[...]
    shift, scale, gate = mod[:, :d], mod[:, d:2 * d], mod[:, 2 * d:]
    h = (x * (1.0 + scale) + shift).astype(BF16)
    proj = jnp.dot(h, win_ref[...], preferred_element_type=F32) + bin_ref[...]
    gate_b = proj[:, :d]
    ubuf[pad:pad + tb, :] = proj[:, d:]

    cw = cw_ref[...]
    u = cb_ref[...]
    for k in range(hist + 1):
        u = u + ubuf[pad - hist + k:pad - hist + k + tb, :] * cw[k:k + 1, :]
    ubuf[pad - hist:pad, :] = ubuf[pad + tb - hist:pad + tb, :]

    ub = u.astype(BF16)
    blk = d // N_LRU_BLOCKS
    lam = lam_ref[...]
    neg = -lam
    softplus = jnp.maximum(neg, 0.0) + jnp.log1p(jnp.exp(-jnp.abs(neg)))
    for n in range(N_LRU_BLOCKS):
        sl = slice(n * blk, (n + 1) * blk)
        un = ub[:, sl]
        r = _sigmoid(jnp.dot(un, wa_ref[n], preferred_element_type=F32) + ba_ref[:, sl])
        ig = _sigmoid(jnp.dot(un, wi_ref[n], preferred_element_type=F32) + bi_ref[:, sl])
        log_a = -LRU_C * r * softplus[:, sl]
        one_m_a2 = -jnp.tanh(log_a) * (jnp.exp(2.0 * log_a) + 1.0)
        a_s[:, sl] = jnp.exp(log_a)
        b_s[:, sl] = jnp.sqrt(one_m_a2) * (ig * u[:, sl])

    row = lax.broadcasted_iota(jnp.int32, (SUBLANES, d), 0)

    def group(g, hprev):
        r0 = pl.multiple_of(g * SUBLANES, SUBLANES)
        a = a_s[pl.ds(r0, SUBLANES), :]
        b = b_s[pl.ds(r0, SUBLANES), :]
        for sh in (1, 2, 4):
            keep = row >= sh
            a_sh = pltpu.roll(a, sh, axis=0)
            b_sh = pltpu.roll(b, sh, axis=0)
            b = jnp.where(keep, a * b_sh + b, b)
            a = jnp.where(keep, a * a_sh, a)
        hh = a * hprev + b
        b_s[pl.ds(r0, SUBLANES), :] = hh
        return hh[SUBLANES - 1:SUBLANES, :]

    h_s[...] = lax.fori_loop(0, tb // SUBLANES, group, h_s[...])

    y = b_s[...]
    gl = 0.5 * gate_b * (1.0 + jnp.tanh(0.7978845608028654 * (gate_b + 0.044715 * gate_b * gate_b * gate_b)))
    out = jnp.dot((gl * y).astype(BF16), wout_ref[...], preferred_element_type=F32) + bout_ref[...]
    x1 = _layer_norm(alpha * x + (1.0 + gate) * out, lng_ref[...], lnb_ref[...])
    x1_ref[...] = x1
    mod2 = mod2_ref[...]
    hm_ref[...] = _pack_pairs(x1 * (1.0 + mod2[:, d:2 * d]) + mod2[:, :d])

    @pl.when(t == pl.num_programs(1) - 1)
    def _():
        convn_ref[...] = ubuf[pad - hist:pad, :]
        hlast_ref[...] = h_s[...]


def _l0_mixer(x, mod, mod2, conv0, h0, w, tb, alpha):
    bsz, s, d = x.shape
    hist = conv0.shape[1]
    tok = lambda width: pl.BlockSpec((None, tb, width), lambda b, t: (b, t, 0))
    per_b = lambda rows, width: pl.BlockSpec((None, rows, width), lambda b, t: (b, 0, 0))
    kern = functools.partial(_l0_kernel, tb=tb, d=d, alpha=alpha)
    weights = [w["w_in"], w["b_in"], w["conv_w"], w["conv_b"], w["w_a"], w["b_a"], w["w_i"], w["b_i"],
               w["lam"], w["w_out"], w["b_out"], w["ln_g"], w["ln_b"]]
    return pl.pallas_call(
        kern,
        grid=(bsz, s // tb),
        in_specs=[tok(d), per_b(1, 3 * d), per_b(1, 3 * d), per_b(hist, d), per_b(1, d)]
                 + [_full(a.shape) for a in weights],
        out_specs=[tok(d), tok(d // 2), per_b(hist, d), per_b(1, d)],
        out_shape=[jax.ShapeDtypeStruct((bsz, s, d), F32), jax.ShapeDtypeStruct((bsz, s, d // 2), U32),
                   jax.ShapeDtypeStruct((bsz, hist, d), F32), jax.ShapeDtypeStruct((bsz, 1, d), F32)],
        scratch_shapes=[pltpu.VMEM((tb + SUBLANES, d), F32), pltpu.VMEM((tb, d), F32),
                        pltpu.VMEM((tb, d), F32), pltpu.VMEM((1, d), F32)],
        compiler_params=_params(("arbitrary", "arbitrary")),
        name="l0_mixer",
    )(x, mod, mod2, conv0, h0, *weights)


def _router_kernel(hm_ref, wr_ref, br_ref, base0_ref, eid_ref, tw_ref, rank_ref, cnt_ref, base, *, tm, ne):
    i = pl.program_id(0)

    @pl.when(i == 0)
    def _():
        base[...] = base0_ref[...]

    logits = jnp.dot(_unpack_bf16(hm_ref[...]), wr_ref[...], preferred_element_type=F32) + br_ref[...]
    col = lax.broadcasted_iota(jnp.int32, (tm, ne), 1).astype(F32)
    l = logits
    vals, idxs = [], []
    for _ in range(TOP_K):
        m = jnp.max(l, axis=-1, keepdims=True)
        idx = jnp.min(jnp.where(l == m, col, float(ne)), axis=-1, keepdims=True)
        vals.append(m)
        idxs.append(idx)
        l = jnp.where(col == idx, -jnp.inf, l)
    es = [jnp.exp(v - vals[0]) for v in vals]
    tot = es[0] + es[1] + es[2] + es[3]

    ri = lax.broadcasted_iota(jnp.int32, (tm, tm), 0)
    ci = lax.broadcasted_iota(jnp.int32, (tm, tm), 1)
    tri = (ci < ri).astype(BF16)
    run = base[...]
    lane = lax.broadcasted_iota(jnp.int32, (tm, LANES), 1)
    eid_o = jnp.zeros((tm, LANES), F32)
    tw_o = jnp.zeros((tm, LANES), F32)
    rk_o = jnp.zeros((tm, LANES), F32)
    for k in range(TOP_K):
        oh = col == idxs[k]
        ohf = oh.astype(F32)
        before = jnp.dot(tri, ohf.astype(BF16), preferred_element_type=F32)
        rank = jnp.sum(jnp.where(oh, before + run, 0.0), axis=-1, keepdims=True)
        run = run + jnp.sum(ohf, axis=0, keepdims=True)
        sel = lane == k
        eid_o = jnp.where(sel, idxs[k], eid_o)
        tw_o = jnp.where(sel, es[k] / tot, tw_o)
        rk_o = jnp.where(sel, rank, rk_o)
    base[...] = run
    eid_ref[...] = eid_o.astype(jnp.int32)
    tw_ref[...] = tw_o
    rank_ref[...] = rk_o.astype(jnp.int32)
    cnt_ref[...] = run


def _router(hm, w_r, b_r, base0, tm):
    n, half = hm.shape
    ne = w_r.shape[1]
    kern = functools.partial(_router_kernel, tm=tm, ne=ne)
    tokrow = pl.BlockSpec((tm, LANES), lambda i: (i, 0))
    return pl.pallas_call(
        kern,
        grid=(n // tm,),
        in_specs=[pl.BlockSpec((tm, half), lambda i: (i, 0)), _full((2 * half, ne)), _full((1, ne)), _full((1, ne))],
        out_specs=[tokrow, tokrow, tokrow, _full((1, ne))],
        out_shape=[jax.ShapeDtypeStruct((n, LANES), jnp.int32), jax.ShapeDtypeStruct((n, LANES), F32),
                   jax.ShapeDtypeStruct((n, LANES), jnp.int32), jax.ShapeDtypeStruct((1, ne), F32)],
        scratch_shapes=[pltpu.VMEM((1, ne), F32)],
        compiler_params=_params(("arbitrary",)),
        name="moe_router",
    )(hm, w_r, b_r, base0)


def _permute_kernel(pos_ref, hm_ref, init_ref, out_ref, sem, *, tm):
    del init_ref

    def copy(t, k):
        return pltpu.make_async_copy(hm_ref.at[pl.ds(t, 1)], out_ref.at[pl.ds(pos_ref[t * TOP_K + k], 1)], sem)

    def issue(t, c):
        for k in range(TOP_K):
            copy(t, k).start()
        return c

    def drain(t, c):
        for k in range(TOP_K):
            copy(t, k).wait()
        return c

    lax.fori_loop(0, tm, issue, 0)
    lax.fori_loop(0, tm, drain, 0)


def _permute(hm, pos_flat, dest, tm):
    n, w = hm.shape
    return pl.pallas_call(
        functools.partial(_permute_kernel, tm=tm),
        grid=(n // tm,),
        in_specs=[pl.BlockSpec((tm * TOP_K,), lambda i: (i,), memory_space=pltpu.SMEM),
                  pl.BlockSpec((tm, w), lambda i: (i, 0)),
                  pl.BlockSpec(memory_space=pl.ANY)],
        out_specs=pl.BlockSpec(memory_space=pl.ANY),
        out_shape=jax.ShapeDtypeStruct(dest.shape, dest.dtype),
        scratch_shapes=[pltpu.SemaphoreType.DMA],
        input_output_aliases={2: 0},
        compiler_params=_params(("arbitrary",)),
        name="moe_permute",
    )(pos_flat, hm, dest)


def _unpermute_kernel(pos_ref, ys_ref, out_ref, sem, *, tm):
    def copy(t, k):
        return pltpu.make_async_copy(ys_ref.at[pl.ds(pos_ref[t * TOP_K + k], 1)], out_ref.at[k, pl.ds(t, 1)], sem)

    def issue(t, c):
        for k in range(TOP_K):
            copy(t, k).start()
        return c

    def drain(t, c):
        for k in range(TOP_K):
            copy(t, k).wait()
        return c

    lax.fori_loop(0, tm, issue, 0)
    lax.fori_loop(0, tm, drain, 0)


def _unpermute(ys, pos_flat, n, tm):
    _, w = ys.shape
    return pl.pallas_call(
        functools.partial(_unpermute_kernel, tm=tm),
        grid=(n // tm,),
        in_specs=[pl.BlockSpec((tm * TOP_K,), lambda i: (i,), memory_space=pltpu.SMEM),
                  pl.BlockSpec(memory_space=pl.ANY)],
        out_specs=pl.BlockSpec((TOP_K, tm, w), lambda i: (0, i, 0)),
        out_shape=jax.ShapeDtypeStruct((TOP_K, n, w), ys.dtype),
        scratch_shapes=[pltpu.SemaphoreType.DMA],
        compiler_params=_params(("arbitrary",)),
        name="moe_unpermute",
    )(pos_flat, ys)


def _expert_kernel(te_ref, nu_ref, xs_ref, wgu_ref, bgu_ref, wdn_ref, bdn_ref, ys_ref, wgu_bf, wdn_bf, *, d):
    j = pl.program_id(0)
    e = te_ref[j]
    prev = te_ref[jnp.maximum(j - 1, 0)]

    @pl.when(jnp.logical_or(j == 0, e != prev))
    def _():
        rows = 128

        def cast(c, carry):
            r0 = pl.multiple_of(c * rows, rows)
            wgu_bf[pl.ds(r0, rows), :] = wgu_ref[pl.ds(r0, rows), :].astype(BF16)
            wdn_bf[pl.ds(r0, rows), :] = wdn_ref[pl.ds(r0, rows), :].astype(BF16)
            return carry

        lax.fori_loop(0, d // rows, cast, 0)

    @pl.when(j < nu_ref[0])
    def _():
        gu = jnp.dot(_unpack_bf16(xs_ref[...]), wgu_bf[...], preferred_element_type=F32) + bgu_ref[...]
        g = jnp.minimum(gu[:, :d], SWIGLU_LIMIT)
        u = jnp.clip(gu[:, d:], -SWIGLU_LIMIT, SWIGLU_LIMIT)
        act = (u + 1.0) * (g * _sigmoid(SWIGLU_ALPHA * g))
        y = jnp.dot(act.astype(BF16), wdn_bf[...], preferred_element_type=F32) + bdn_ref[...]
        ys_ref[...] = _pack_pairs(y)

    @pl.when(j >= nu_ref[0])
    def _():
        ys_ref[...] = jnp.zeros_like(ys_ref)


def _experts(xs, tile_expert, n_used, w_gu, b_gu, w_dn, b_dn, layer, tm=EXPERT_TM):
    m_pad, half = xs.shape
    d = 2 * half
    nl, ne = w_gu.shape[:2]
    rows = lambda j, te, nu: (jnp.minimum(j, nu[0] - 1), 0)
    wsel = lambda j, te, nu: (layer, te[j], 0, 0)
    grid_spec = pltpu.PrefetchScalarGridSpec(
        num_scalar_prefetch=2,
        grid=(m_pad // tm,),
        in_specs=[pl.BlockSpec((tm, half), rows),
                  pl.BlockSpec((None, None, d, 2 * d), wsel),
                  pl.BlockSpec((None, None, 1, 2 * d), wsel),
                  pl.BlockSpec((None, None, d, d), wsel),
                  pl.BlockSpec((None, None, 1, d), wsel)],
        out_specs=pl.BlockSpec((tm, half), lambda j, te, nu: (j, 0)),
        scratch_shapes=[pltpu.VMEM((d, 2 * d), BF16), pltpu.VMEM((d, d), BF16)],
    )
    return pl.pallas_call(
        functools.partial(_expert_kernel, d=d),
        grid_spec=grid_spec,
        out_shape=jax.ShapeDtypeStruct((m_pad, half), U32),
        compiler_params=_params(("arbitrary",)),
        name="moe_experts",
    )(tile_expert, n_used, xs, w_gu, b_gu.reshape(nl, ne, 1, 2 * d), w_dn, b_dn.reshape(nl, ne, 1, d))


def _combine_kernel(x_ref, y_ref, tw_ref, mod_ref, lng_ref, lnb_ref, o_ref, *, d, alpha):
    tw = tw_ref[...]
    m_hi = None
    m_lo = None
    for k in range(TOP_K):
        hi, lo = _unpack_pairs(y_ref[k])
        wk = tw[:, k:k + 1]
        m_hi = wk * hi if m_hi is None else m_hi + wk * hi
        m_lo = wk * lo if m_lo is None else m_lo + wk * lo
    m = jnp.concatenate([m_hi, m_lo], axis=1)
    gate = mod_ref[...][:, 2 * d:]
    o_ref[...] = _layer_norm(alpha * x_ref[...] + (1.0 + gate) * m, lng_ref[...], lnb_ref[...])


def _combine(x, y4, tw, mod, ln_g, ln_b, tm, alpha):
    bsz, s, d = x.shape
    nt = s // tm
    tok = pl.BlockSpec((None, tm, d), lambda b, t: (b, t, 0))
    return pl.pallas_call(
        functools.partial(_combine_kernel, d=d, alpha=alpha),
        grid=(bsz, nt),
        in_specs=[tok,
                  pl.BlockSpec((TOP_K, tm, d // 2), lambda b, t: (0, b * nt + t, 0)),
                  pl.BlockSpec((tm, LANES), lambda b, t: (b * nt + t, 0)),
                  pl.BlockSpec((None, 1, 3 * d), lambda b, t: (b, 0, 0)),
                  _full((1, d)), _full((1, d))],
        out_specs=tok,
        out_shape=jax.ShapeDtypeStruct((bsz, s, d), F32),
        compiler_params=_params(("arbitrary", "arbitrary")),
        name="moe_combine",
    )(x, y4, tw, mod, ln_g, ln_b)


def _moe(hm_p, hm_s, w_r, b_r, w_gu, b_gu, w_dn, b_dn, layer):
    half = hm_p.shape[-1]
    hm_p = hm_p.reshape(-1, half)
    hm_s = hm_s.reshape(-1, half)
    n_p, n_s = hm_p.shape[0], hm_s.shape[0]
    ne = w_r.shape[1]
    w_r = w_r.astype(BF16)
    b_r = b_r.reshape(1, ne)
    eid_p, tw_p, rank_p, cnt_p = _router(hm_p, w_r, b_r, jnp.zeros((1, ne), F32), PROMPT_TM)
    eid_s, tw_s, rank_s, cnt = _router(hm_s, w_r, b_r, cnt_p, n_s)
    tm = EXPERT_TM
    cnt = cnt.reshape(ne).astype(jnp.int32)
    gsz = ((cnt + tm - 1) // tm) * tm
    ends = jnp.cumsum(gsz)
    offs = ends - gsz
    pos_p = (offs[eid_p[:, :TOP_K]] + rank_p[:, :TOP_K]).reshape(-1)
    pos_s = (offs[eid_s[:, :TOP_K]] + rank_s[:, :TOP_K]).reshape(-1)
    n_tiles = ((n_p + n_s) * TOP_K + ne * (tm - 1)) // tm + 1
    m_pad = n_tiles * tm
    tile_start = jnp.arange(n_tiles, dtype=jnp.int32) * tm
    tile_expert = jnp.minimum(jnp.sum((tile_start[:, None] >= ends[None, :]).astype(jnp.int32), axis=1), ne - 1)
    n_used = (ends[-1] // tm).astype(jnp.int32).reshape(1)
    xs = _permute(hm_p, pos_p, jnp.zeros((m_pad, half), U32), PERM_TM)
    xs = _permute(hm_s, pos_s, xs, n_s)
    ys = _experts(xs, tile_expert, n_used, w_gu, b_gu, w_dn, b_dn, layer)
    return (_unpermute(ys, pos_p, n_p, PERM_TM), tw_p), (_unpermute(ys, pos_s, n_s, n_s), tw_s)


def _qkv_kernel(x_ref, modk_ref, modq_ref, cos_ref, sin_ref, wdkv_ref, gkv_ref, wdq_ref, gq_ref, wuq_ref,
                ckv_ref, kpe_ref, kc_ref, kp_ref, qh_ref, *, d, kvl, rope, nh, nope, q_scale):
    x = x_ref[...]
    cos = cos_ref[...]
    sin = sin_ref[...]
    modk = modk_ref[...]
    hk = (x * (1.0 + modk[:, d:2 * d]) + modk[:, :d]).astype(BF16)
    kv = jnp.dot(hk, wdkv_ref[...], preferred_element_type=F32)
    c = kv[:, :kvl]
    ckv = c * lax.rsqrt(jnp.mean(c * c, axis=-1, keepdims=True) + RMS_EPS) * gkv_ref[...]
    kp = kv[:, kvl:kvl + LANES] * cos + kv[:, kvl + LANES:kvl + 2 * LANES] * sin
    ckv_ref[...] = ckv
    kpe_ref[...] = kp[:, :rope]
    kc_ref[...] = ckv.astype(BF16)
    kp_ref[...] = kp[:, :rope].astype(BF16)

    modq = modq_ref[...]
    hq = (x * (1.0 + modq[:, d:2 * d]) + modq[:, :d]).astype(BF16)
    qd = jnp.dot(hq, wdq_ref[...], preferred_element_type=F32)
    qn = (qd * lax.rsqrt(jnp.mean(qd * qd, axis=-1, keepdims=True) + RMS_EPS) * gq_ref[...]).astype(BF16)
    q = jnp.dot(qn, wuq_ref[...], preferred_element_type=F32)
    for h in range(nh):
        qh_ref[h, :, :nope] = (q[:, h * nope:(h + 1) * nope] * q_scale).astype(BF16)
    pe0 = nh * nope
    sw0 = pe0 + nh * rope
    per = LANES // rope
    for j in range(nh // per):
        r2 = (q[:, pe0 + j * LANES:pe0 + (j + 1) * LANES] * cos
              + q[:, sw0 + j * LANES:sw0 + (j + 1) * LANES] * sin) * q_scale
        for i in range(per):
            qh_ref[j * per + i, :, nope:] = r2[:, i * rope:(i + 1) * rope].astype(BF16)


def _qkv(x, modk, modq, cos_t, sin_t, w, tm, dims):
    bsz, s, d = x.shape
    kvl, rope, nh, nope = dims["kvl"], dims["rope"], dims["nh"], dims["nope"]
    tok = lambda width: pl.BlockSpec((None, tm, width), lambda b, t: (b, t, 0))
    head = lambda width: pl.BlockSpec((None, nh, tm, width), lambda b, t: (b, 0, t, 0))
    tab = pl.BlockSpec((tm, LANES), lambda b, t: (t, 0))
    weights = [w["w_dkv"], w["g_kv"], w["w_dq"], w["g_q"], w["w_uq"]]
    kern = functools.partial(_qkv_kernel, d=d, kvl=kvl, rope=rope, nh=nh, nope=nope, q_scale=dims["q_scale"])
    return pl.pallas_call(
        kern,
        grid=(bsz, s // tm),
        in_specs=[tok(d),
                  pl.BlockSpec((None, 1, 2 * d), lambda b, t: (b, 0, 0)),
                  pl.BlockSpec((None, 1, 3 * d), lambda b, t: (b, 0, 0)),
                  tab, tab] + [_full(a.shape) for a in weights],
        out_specs=[tok(kvl), tok(rope), tok(kvl), tok(rope), head(nope + rope)],
        out_shape=[jax.ShapeDtypeStruct((bsz, s, kvl), F32), jax.ShapeDtypeStruct((bsz, s, rope), F32),
                   jax.ShapeDtypeStruct((bsz, s, kvl), BF16), jax.ShapeDtypeStruct((bsz, s, rope), BF16),
                   jax.ShapeDtypeStruct((bsz, nh, s, nope + rope), BF16)],
        compiler_params=_params(("arbitrary", "arbitrary")),
        name="mla_qkv",
    )(x, modk, modq, cos_t, sin_t, *weights)


def _kproj_kernel(kc_ref, kp_ref, wuk_ref, kh_ref, *, nh, nope):
    kc = kc_ref[...]
    kp = kp_ref[...]
    nt = (((1,), (1,)), ((), ()))
    for h in range(nh):
        kh_ref[h, :, :nope] = lax.dot_general(kc, wuk_ref[h], nt, preferred_element_type=F32).astype(BF16)
        kh_ref[h, :, nope:] = kp


def _kproj(kc, kp, w_uk, tm):
    bsz, s, kvl = kc.shape
    rope = kp.shape[-1]
    nh, nope, _ = w_uk.shape
    tok = lambda width: pl.BlockSpec((None, tm, width), lambda b, t: (b, t, 0))
    return pl.pallas_call(
        functools.partial(_kproj_kernel, nh=nh, nope=nope),
        grid=(bsz, s // tm),
        in_specs=[tok(kvl), tok(rope), _full(w_uk.shape)],
        out_specs=pl.BlockSpec((None, nh, tm, nope + rope), lambda b, t: (b, 0, t, 0)),
        out_shape=jax.ShapeDtypeStruct((bsz, nh, s, nope + rope), BF16),
        compiler_params=_params(("arbitrary", "arbitrary")),
        name="mla_kproj",
    )(kc, kp, w_uk)


def _attn_kernel(qi_ref, kj_ref, last_ref, edge_ref, qh_ref, kh_ref, kc_ref, o_ref, m_s, l_s, acc_s,
                 *, nh, tq, tk, kvl, causal, valid):
    s_id = pl.program_id(1)
    qi = qi_ref[s_id]
    kj = kj_ref[s_id]
    nt = (((1,), (1,)), ((), ()))

    @pl.when(kj == 0)
    def _():
        m_s[...] = jnp.full_like(m_s, -jnp.inf)
        l_s[...] = jnp.zeros_like(l_s)
        acc_s[...] = jnp.zeros_like(acc_s)

    def sweep(masked):
        kc = kc_ref[...]
        if masked:
            kidx = kj * tk + lax.broadcasted_iota(jnp.int32, (tq, tk), 1)
            if causal:
                qidx = qi * tq + lax.broadcasted_iota(jnp.int32, (tq, tk), 0)
                visible = (kidx // CHUNK) <= (qidx // CHUNK)
            else:
                visible = kidx < valid
        for h in range(nh):
            s = lax.dot_general(qh_ref[h], kh_ref[h], nt, preferred_element_type=F32)
            if masked:
                s = jnp.where(visible, s, -jnp.inf)
            slabs = [s[:, j * LANES:(j + 1) * LANES] for j in range(tk // LANES)]
            mx = slabs[0]
            for sj in slabs[1:]:
                mx = jnp.maximum(mx, sj)
            m_prev = m_s[h]
            m_new = jnp.maximum(m_prev, jnp.max(mx, axis=1, keepdims=True))
            scale = jnp.exp2(m_prev - m_new)
            ps = []
            rsum = None
            for sj in slabs:
                pj = jnp.exp2(sj - m_new)
                ps.append(pj.astype(BF16))
                rsum = pj if rsum is None else rsum + pj
            p = jnp.concatenate(ps, axis=1)
            l_s[h] = scale * l_s[h] + jnp.sum(rsum, axis=1, keepdims=True)
            pv = jnp.dot(p, kc, preferred_element_type=F32)
            acc_s[h] = acc_s[h] * jnp.concatenate([scale] * (kvl // LANES), axis=1) + pv
            m_s[h] = m_new

    edge = edge_ref[s_id]

    @pl.when(edge == 1)
    def _():
        sweep(True)

    @pl.when(edge == 0)
    def _():
        sweep(False)

    @pl.when(last_ref[s_id] == 1)
    def _():
        for h in range(nh):
            inv = jnp.concatenate([l_s[h]] * (kvl // LANES), axis=1)
            o_ref[h] = (acc_s[h] / inv).astype(BF16)


def _attention(qh, kh, kc, tq, tk, causal, valid):
    bsz, nh, s, qk = qh.shape
    skv, kvl = kc.shape[1], kc.shape[2]
    qi, kj, last, edge = [], [], [], []
    for i in range(s // tq):
        hi = ((i * tq + tq - 1) // tk) if causal else (skv // tk - 1)
        for j in range(hi + 1):
            qi.append(i)
            kj.append(j)
            last.append(1 if j == hi else 0)
            if causal:
                edge.append(1 if ((j + 1) * tk - 1) // CHUNK > (i * tq) // CHUNK else 0)
            else:
                edge.append(1 if (j + 1) * tk > valid else 0)
    tabs = [jnp.asarray(np.array(a, np.int32)) for a in (qi, kj, last, edge)]
    qspec = lambda width: pl.BlockSpec((None, nh, tq, width), lambda b, t, qi, kj, last, edge: (b, 0, qi[t], 0))
    khspec = pl.BlockSpec((None, nh, tk, qk), lambda b, t, qi, kj, last, edge: (b, 0, kj[t], 0))
    kcspec = pl.BlockSpec((None, tk, kvl), lambda b, t, qi, kj, last, edge: (b, kj[t], 0))
    grid_spec = pltpu.PrefetchScalarGridSpec(
        num_scalar_prefetch=4,
        grid=(bsz, len(qi)),
        in_specs=[qspec(qk), khspec, kcspec],
        out_specs=qspec(kvl),
        scratch_shapes=[pltpu.VMEM((nh, tq, LANES), F32), pltpu.VMEM((nh, tq, LANES), F32),
                        pltpu.VMEM((nh, tq, kvl), F32)],
    )
    kern = functools.partial(_attn_kernel, nh=nh, tq=tq, tk=tk, kvl=kvl, causal=causal, valid=valid)
    return pl.pallas_call(
        kern,
        grid_spec=grid_spec,
        out_shape=jax.ShapeDtypeStruct((bsz, nh, s, kvl), BF16),
        compiler_params=_params(("arbitrary", "arbitrary")),
        name="mla_attention",
    )(*tabs, qh, kh, kc)


def _attn_out_kernel(o_ref, x_ref, mod_ref, mod2_ref, wuv_ref, wo_ref, lng_ref, lnb_ref, x3_ref, hm_ref,
                     *, d, nh, alpha):
    parts = [jnp.dot(o_ref[h], wuv_ref[h], preferred_element_type=F32) for h in range(nh)]
    o = jnp.concatenate(parts, axis=1).astype(BF16)
    m = jnp.dot(o, wo_ref[...], preferred_element_type=F32)
    gate = mod_ref[...][:, 2 * d:]
    x3 = _layer_norm(alpha * x_ref[...] + (1.0 + gate) * m, lng_ref[...], lnb_ref[...])
    x3_ref[...] = x3
    mod2 = mod2_ref[...]
    hm_ref[...] = _pack_pairs(x3 * (1.0 + mod2[:, d:2 * d]) + mod2[:, :d])


def _attn_out(o_lat, x, mod, mod2, w_uv, w_o, ln_g, ln_b, tm, alpha):
    bsz, s, d = x.shape
    nh, kvl = o_lat.shape[1], o_lat.shape[3]
    tok = lambda width: pl.BlockSpec((None, tm, width), lambda b, t: (b, t, 0))
    modspec = pl.BlockSpec((None, 1, 3 * d), lambda b, t: (b, 0, 0))
    return pl.pallas_call(
        functools.partial(_attn_out_kernel, d=d, nh=nh, alpha=alpha),
        grid=(bsz, s // tm),
        in_specs=[pl.BlockSpec((None, nh, tm, kvl), lambda b, t: (b, 0, t, 0)), tok(d), modspec, modspec,
                  _full(w_uv.shape), _full(w_o.shape), _full((1, d)), _full((1, d))],
        out_specs=[tok(d), tok(d // 2)],
        out_shape=[jax.ShapeDtypeStruct((bsz, s, d), F32), jax.ShapeDtypeStruct((bsz, s, d // 2), U32)],
        compiler_params=_params(("arbitrary", "arbitrary")),
        name="mla_out",
    )(o_lat, x, mod, mod2, w_uv, w_o, ln_g, ln_b)


def _rope_tables(pos, rope):
    half = rope // 2
    inv = ROPE_THETA ** (-2.0 * jnp.arange(half, dtype=F32) / rope)
    ang = pos.astype(F32)[:, None] * inv[None, :]
    cos, sin = jnp.cos(ang), jnp.sin(ang)
    rep = LANES // rope
    return (jnp.concatenate([cos, cos] * rep, axis=1), jnp.concatenate([-sin, sin] * rep, axis=1))


def kernel(x_prompt, x_sample, state_conv, state_rglru, cache_ckv, cache_kpe, c_prompt, c_sample, w_ada, b_ada, ln_g, ln_b, lru_w_in, lru_b_in, lru_conv_w, lru_conv_b, lru_w_a, lru_b_a, lru_w_i, lru_b_i, lru_lambda, lru_w_out, lru_b_out, kv_w_ada, kv_b_ada, mla_w_dkv, mla_g_kv, mla_w_uk, mla_w_uv, mla_w_dq, mla_g_q, mla_w_uq, mla_w_o, moe_w_r, moe_b_r, moe_w_gu, moe_b_gu, moe_w_dn, moe_b_dn):
    bp, sp, d = x_prompt.shape
    bs, ss, _ = x_sample.shape
    depth = w_ada.shape[0]
    alpha = float((2.0 * depth) ** 0.25)
    nh, nope, kvl = mla_w_uk.shape
    rope = cache_kpe.shape[-1]
    past = cache_ckv.shape[1]
    hist = state_conv.shape[2]
    dims = dict(kvl=kvl, rope=rope, nh=nh, nope=nope, q_scale=float((nope + rope) ** -0.5 * LOG2E))
    row = lambda v: v.reshape(1, -1)

    nrow = bp + bs
    rpad = -nrow % (2 * SUBLANES)
    c_rows = jnp.concatenate([c_prompt, c_sample, jnp.zeros((rpad, d), F32)], axis=0)
    mods = _ada(c_rows, w_ada.reshape(depth * 2, d, 3 * d), b_ada.reshape(depth * 2, 1, 3 * d))
    modkv = _ada(c_rows, kv_w_ada.reshape(1, d, 2 * d), kv_b_ada.reshape(1, 1, 2 * d))[0]
    mod_p = lambda i: mods[i, :bp].reshape(bp, 1, 3 * d)
    mod_s = lambda i: mods[i, bp:nrow].reshape(bs, 1, 3 * d)

    l0w = dict(w_in=lru_w_in[0].astype(BF16), b_in=row(lru_b_in[0]), conv_w=lru_conv_w[0], conv_b=row(lru_conv_b[0]),
               w_a=lru_w_a[0].astype(BF16), b_a=row(lru_b_a[0]), w_i=lru_w_i[0].astype(BF16), b_i=row(lru_b_i[0]),
               lam=row(lru_lambda[0]), w_out=lru_w_out[0].astype(BF16), b_out=row(lru_b_out[0]),
               ln_g=row(ln_g[0, 0]), ln_b=row(ln_b[0, 0]))
    x1p, hmp, conv_p, h_p = _l0_mixer(x_prompt, mod_p(0), mod_p(1), jnp.zeros((bp, hist, d), F32),
                                      jnp.zeros((bp, 1, d), F32), l0w, L0_TB, alpha)
    x1s, hms, conv_s, h_s = _l0_mixer(x_sample, mod_s(0), mod_s(1), state_conv[0], state_rglru[0].reshape(bs, 1, d),
                                      l0w, ss, alpha)

    (y4p, twp), (y4s, tws) = _moe(hmp, hms, moe_w_r[0], moe_b_r[0], moe_w_gu, moe_b_gu, moe_w_dn, moe_b_dn, 0)
    x2p = _combine(x1p, y4p, twp, mod_p(1), row(ln_g[0, 1]), row(ln_b[0, 1]), PROMPT_TM, alpha)
    x2s = _combine(x1s, y4s, tws, mod_s(1), row(ln_g[0, 1]), row(ln_b[0, 1]), ss, alpha)

    pe_cols = jnp.arange(rope)
    sw_cols = jnp.concatenate([pe_cols[rope // 2:], pe_cols[:rope // 2]])
    zpad = jnp.zeros((d, LANES - rope), F32)
    w_dkv_ext = jnp.concatenate([mla_w_dkv[:, :kvl], mla_w_dkv[:, kvl:], zpad,
                                 mla_w_dkv[:, kvl:][:, sw_cols], zpad], axis=1).astype(BF16)
    wq = mla_w_uq[0].reshape(-1, nh, nope + rope)
    w_uq_ext = jnp.concatenate([wq[:, :, :nope].reshape(-1, nh * nope), wq[:, :, nope:].reshape(-1, nh * rope),
                                wq[:, :, nope:][:, :, sw_cols].reshape(-1, nh * rope)], axis=1).astype(BF16)
    qw = dict(w_dkv=w_dkv_ext, g_kv=row(mla_g_kv), w_dq=mla_w_dq[0].astype(BF16), g_q=row(mla_g_q[0]),
              w_uq=w_uq_ext)
    w_uk = mla_w_uk.astype(BF16)
    cos_p, sin_p = _rope_tables(jnp.arange(sp), rope)
    cos_s, sin_s = _rope_tables(past + jnp.arange(ss), rope)
    modkv_p = modkv[:bp].reshape(bp, 1, 2 * d)
    modkv_s = modkv[bp:nrow].reshape(bs, 1, 2 * d)
    ckv_p, kpe_p, kc_p, kp_p, qh_p = _qkv(x2p, modkv_p, mod_p(2), cos_p, sin_p, qw, PROMPT_TM, dims)
    ckv_s, kpe_s, kc_s, kp_s, qh_s = _qkv(x2s, modkv_s, mod_s(2), cos_s, sin_s, qw, ss, dims)

    o_p = _attention(qh_p, _kproj(kc_p, kp_p, w_uk, PROMPT_TM), kc_p, ATT_TQ, ATT_TK, True, sp)
    skv = past + ss
    kpad = -skv % LANES
    kc_all = jnp.concatenate([cache_ckv.astype(BF16), kc_s, jnp.zeros((bs, kpad, kvl), BF16)], axis=1)
    kp_all = jnp.concatenate([cache_kpe.astype(BF16), kp_s, jnp.zeros((bs, kpad, rope), BF16)], axis=1)
    o_s = _attention(qh_s, _kproj(kc_all, kp_all, w_uk, skv + kpad), kc_all, ss, skv + kpad, False, skv)
    w_uv = mla_w_uv.astype(BF16)
    w_o = mla_w_o[0].astype(BF16)
    x3p, hm3p = _attn_out(o_p, x2p, mod_p(2), mod_p(3), w_uv, w_o, row(ln_g[1, 0]), row(ln_b[1, 0]), PROMPT_TM, alpha)
    x3s, hm3s = _attn_out(o_s, x2s, mod_s(2), mod_s(3), w_uv, w_o, row(ln_g[1, 0]), row(ln_b[1, 0]), ss, alpha)

    (y4p, twp), (y4s, tws) = _moe(hm3p, hm3s, moe_w_r[1], moe_b_r[1], moe_w_gu, moe_b_gu, moe_w_dn, moe_b_dn, 1)
    y_p = _combine(x3p, y4p, twp, mod_p(3), row(ln_g[1, 1]), row(ln_b[1, 1]), PROMPT_TM, alpha)
    y_s = _combine(x3s, y4s, tws, mod_s(3), row(ln_g[1, 1]), row(ln_b[1, 1]), ss, alpha)

    return (y_p, y_s, conv_p[None], h_p.reshape(1, bp, d), ckv_p, kpe_p,
            conv_s[None], h_s.reshape(1, bs, d), ckv_s, kpe_s)
```

```python
import functools

import numpy as np
import jax
import jax.numpy as jnp
from jax import lax
from jax.experimental import pallas as pl
from jax.experimental.pallas import tpu as pltpu
from jax.experimental.pallas import tpu_sc as plsc

F32 = jnp.float32
BF16 = jnp.bfloat16
U32 = jnp.uint32

CHUNK = 64
N_LRU_BLOCKS = 8
LRU_C = 8.0
ROPE_THETA = 10000.0
TOP_K = 4
SWIGLU_ALPHA = 1.702
SWIGLU_LIMIT = 7.0
LN_EPS = 1e-5
RMS_EPS = 1e-6
LOG2E = 1.4426950408889634

LANES = 128
SUBLANES = 8
VMEM_LIMIT = 56 * 1024 * 1024

PROMPT_TM = 512
L0_TB = 256
EXPERT_TM = 512
SC_CORES = 2
SC_WORKERS = SC_CORES * 16
SC_CHUNK = 64
ATT_TQ = 512
ATT_TK = 1024


def _sigmoid(x):
    return 0.5 * (jnp.tanh(0.5 * x) + 1.0)


def _layer_norm(v, g, b):
    mu = jnp.mean(v, axis=-1, keepdims=True)
    d = v - mu
    var = jnp.mean(d * d, axis=-1, keepdims=True)
    return d * lax.rsqrt(var + LN_EPS) * g + b


def _pack_pairs(x):
    w = x.shape[1] // 2
    hi = lax.bitcast_convert_type(x[:, :w].astype(BF16).astype(F32), U32)
    lo = lax.bitcast_convert_type(x[:, w:].astype(BF16).astype(F32), U32)
    return hi | (lo >> 16)


def _unpack_pairs(words):
    hi = lax.bitcast_convert_type(words & jnp.uint32(0xFFFF0000), F32)
    lo = lax.bitcast_convert_type(words << 16, F32)
    return hi, lo


def _unpack_bf16(words):
    hi, lo = _unpack_pairs(words)
    return jnp.concatenate([hi.astype(BF16), lo.astype(BF16)], axis=1)


def _full(shape):
    nd = len(shape)
    return pl.BlockSpec(shape, lambda *_: (0,) * nd)


def _params(sem, vmem=VMEM_LIMIT):
    return pltpu.CompilerParams(dimension_semantics=sem, vmem_limit_bytes=vmem)


def _ada_kernel(c_ref, w_ref, b_ref, o_ref):
    c = c_ref[...]
    s = (c * _sigmoid(c)).astype(BF16)
    o_ref[...] = jnp.dot(s, w_ref[...].astype(BF16), preferred_element_type=F32) + b_ref[...]


def _ada(c_rows, w, b, tn=1024):
    g, d, n = w.shape
    r = c_rows.shape[0]
    return pl.pallas_call(
        _ada_kernel,
        grid=(g, n // tn),
        in_specs=[_full((r, d)),
                  pl.BlockSpec((None, d, tn), lambda i, j: (i, 0, j)),
                  pl.BlockSpec((None, 1, tn), lambda i, j: (i, 0, j))],
        out_specs=pl.BlockSpec((None, r, tn), lambda i, j: (i, 0, j)),
        out_shape=jax.ShapeDtypeStruct((g, r, n), F32),
        compiler_params=_params(("arbitrary", "arbitrary")),
        name="ada_mod",
    )(c_rows, w, b)


def _l0_kernel(x_ref, mod_ref, mod2_ref, conv0_ref, h0_ref, win_ref, bin_ref, cw_ref, cb_ref,
               wa_ref, ba_ref, wi_ref, bi_ref, lam_ref, wout_ref, bout_ref, lng_ref, lnb_ref,
               x1_ref, hm_ref, convn_ref, hlast_ref,
               ubuf, a_s, b_s, h_s, *, tb, d, alpha):
    t = pl.program_id(1)
    pad = SUBLANES
    hist = convn_ref.shape[0]

    @pl.when(t == 0)
    def _():
        ubuf[pad - hist:pad, :] = conv0_ref[...]
        h_s[...] = h0_ref[...]

    x = x_ref[...]
    mod = mod_ref[...]
    shift, scale, gate = mod[:, :d], mod[:, d:2 * d], mod[:, 2 * d:]
    h = (x * (1.0 + scale) + shift).astype(BF16)
    proj = jnp.dot(h, win_ref[...], preferred_element_type=F32) + bin_ref[...]
    gate_b = proj[:, :d]
    ubuf[pad:pad + tb, :] = proj[:, d:]

    cw = cw_ref[...]
    u = cb_ref[...]
    for k in range(hist + 1):
        u = u + ubuf[pad - hist + k:pad - hist + k + tb, :] * cw[k:k + 1, :]
    ubuf[pad - hist:pad, :] = ubuf[pad + tb - hist:pad + tb, :]

    ub = u.astype(BF16)
    blk = d // N_LRU_BLOCKS
    lam = lam_ref[...]
    neg = -lam
    softplus = jnp.maximum(neg, 0.0) + jnp.log1p(jnp.exp(-jnp.abs(neg)))
    for n in range(N_LRU_BLOCKS):
        sl = slice(n * blk, (n + 1) * blk)
        un = ub[:, sl]
        r = _sigmoid(jnp.dot(un, wa_ref[n], preferred_element_type=F32) + ba_ref[:, sl])
        ig = _sigmoid(jnp.dot(un, wi_ref[n], preferred_element_type=F32) + bi_ref[:, sl])
        log_a = -LRU_C * r * softplus[:, sl]
        one_m_a2 = -jnp.tanh(log_a) * (jnp.exp(2.0 * log_a) + 1.0)
        a_s[:, sl] = jnp.exp(log_a)
        b_s[:, sl] = jnp.sqrt(one_m_a2) * (ig * u[:, sl])

    row = lax.broadcasted_iota(jnp.int32, (SUBLANES, d), 0)

    def group(g, hprev):
        r0 = pl.multiple_of(g * SUBLANES, SUBLANES)
        a = a_s[pl.ds(r0, SUBLANES), :]
        b = b_s[pl.ds(r0, SUBLANES), :]
        for sh in (1, 2, 4):
            keep = row >= sh
            a_sh = pltpu.roll(a, sh, axis=0)
            b_sh = pltpu.roll(b, sh, axis=0)
            b = jnp.where(keep, a * b_sh + b, b)
            a = jnp.where(keep, a * a_sh, a)
        hh = a * hprev + b
        b_s[pl.ds(r0, SUBLANES), :] = hh
        return hh[SUBLANES - 1:SUBLANES, :]

    h_s[...] = lax.fori_loop(0, tb // SUBLANES, group, h_s[...])

    y = b_s[...]
    gl = 0.5 * gate_b * (1.0 + jnp.tanh(0.7978845608028654 * (gate_b + 0.044715 * gate_b * gate_b * gate_b)))
    out = jnp.dot((gl * y).astype(BF16), wout_ref[...], preferred_element_type=F32) + bout_ref[...]
    x1 = _layer_norm(alpha * x + (1.0 + gate) * out, lng_ref[...], lnb_ref[...])
    x1_ref[...] = x1
    mod2 = mod2_ref[...]
    hm_ref[...] = _pack_pairs(x1 * (1.0 + mod2[:, d:2 * d]) + mod2[:, :d])

    @pl.when(t == pl.num_programs(1) - 1)
    def _():
        convn_ref[...] = ubuf[pad - hist:pad, :]
        hlast_ref[...] = h_s[...]


def _l0_mixer(x, mod, mod2, conv0, h0, w, tb, alpha):
    bsz, s, d = x.shape
    hist = conv0.shape[1]
    tok = lambda width: pl.BlockSpec((None, tb, width), lambda b, t: (b, t, 0))
    per_b = lambda rows, width: pl.BlockSpec((None, rows, width), lambda b, t: (b, 0, 0))
    kern = functools.partial(_l0_kernel, tb=tb, d=d, alpha=alpha)
    weights = [w["w_in"], w["b_in"], w["conv_w"], w["conv_b"], w["w_a"], w["b_a"], w["w_i"], w["b_i"],
               w["lam"], w["w_out"], w["b_out"], w["ln_g"], w["ln_b"]]
    return pl.pallas_call(
        kern,
        grid=(bsz, s // tb),
        in_specs=[tok(d), per_b(1, 3 * d), per_b(1, 3 * d), per_b(hist, d), per_b(1, d)]
                 + [_full(a.shape) for a in weights],
        out_specs=[tok(d), tok(d // 2), per_b(hist, d), per_b(1, d)],
        out_shape=[jax.ShapeDtypeStruct((bsz, s, d), F32), jax.ShapeDtypeStruct((bsz, s, d // 2), U32),
                   jax.ShapeDtypeStruct((bsz, hist, d), F32), jax.ShapeDtypeStruct((bsz, 1, d), F32)],
        scratch_shapes=[pltpu.VMEM((tb + SUBLANES, d), F32), pltpu.VMEM((tb, d), F32),
                        pltpu.VMEM((tb, d), F32), pltpu.VMEM((1, d), F32)],
        compiler_params=_params(("arbitrary", "arbitrary")),
        name="l0_mixer",
    )(x, mod, mod2, conv0, h0, *weights)


def _router_kernel(hm_ref, wr_ref, br_ref, base0_ref, eid_ref, tw_ref, rank_ref, cnt_ref, base, *, tm, ne):
    i = pl.program_id(0)

    @pl.when(i == 0)
    def _():
        base[...] = base0_ref[...]

    logits = jnp.dot(_unpack_bf16(hm_ref[...]), wr_ref[...], preferred_element_type=F32) + br_ref[...]
    col = lax.broadcasted_iota(jnp.int32, (tm, ne), 1).astype(F32)
    l = logits
    vals, idxs = [], []
    for _ in range(TOP_K):
        m = jnp.max(l, axis=-1, keepdims=True)
        idx = jnp.min(jnp.where(l == m, col, float(ne)), axis=-1, keepdims=True)
        vals.append(m)
        idxs.append(idx)
        l = jnp.where(col == idx, -jnp.inf, l)
    es = [jnp.exp(v - vals[0]) for v in vals]
    tot = es[0] + es[1] + es[2] + es[3]

    ri = lax.broadcasted_iota(jnp.int32, (tm, tm), 0)
    ci = lax.broadcasted_iota(jnp.int32, (tm, tm), 1)
    tri = (ci < ri).astype(BF16)
    run = base[...]
    lane = lax.broadcasted_iota(jnp.int32, (tm, LANES), 1)
    eid_o = jnp.zeros((tm, LANES), F32)
    tw_o = jnp.zeros((tm, LANES), F32)
    rk_o = jnp.zeros((tm, LANES), F32)
    for k in range(TOP_K):
        oh = col == idxs[k]
        ohf = oh.astype(F32)
        before = jnp.dot(tri, ohf.astype(BF16), preferred_element_type=F32)
        rank = jnp.sum(jnp.where(oh, before + run, 0.0), axis=-1, keepdims=True)
        run = run + jnp.sum(ohf, axis=0, keepdims=True)
        sel = lane == k
        eid_o = jnp.where(sel, idxs[k], eid_o)
        tw_o = jnp.where(sel, es[k] / tot, tw_o)
        rk_o = jnp.where(sel, rank, rk_o)
    base[...] = run
    eid_ref[...] = eid_o.astype(jnp.int32)
    tw_ref[...] = tw_o
    rank_ref[...] = rk_o.astype(jnp.int32)
    cnt_ref[...] = run


def _router(hm, w_r, b_r, base0, tm):
    n, half = hm.shape
    ne = w_r.shape[1]
    kern = functools.partial(_router_kernel, tm=tm, ne=ne)
    tokrow = pl.BlockSpec((tm, LANES), lambda i: (i, 0))
    return pl.pallas_call(
        kern,
        grid=(n // tm,),
        in_specs=[pl.BlockSpec((tm, half), lambda i: (i, 0)), _full((2 * half, ne)), _full((1, ne)), _full((1, ne))],
        out_specs=[tokrow, tokrow, tokrow, _full((1, ne))],
        out_shape=[jax.ShapeDtypeStruct((n, LANES), jnp.int32), jax.ShapeDtypeStruct((n, LANES), F32),
                   jax.ShapeDtypeStruct((n, LANES), jnp.int32), jax.ShapeDtypeStruct((1, ne), F32)],
        scratch_shapes=[pltpu.VMEM((1, ne), F32)],
        compiler_params=_params(("arbitrary",)),
        name="moe_router",
    )(hm, w_r, b_r, base0)


def _permute_kernel(pos_ref, hm_ref, init_ref, out_ref, sem, *, tm):
    del init_ref

    def copy(t, k):
        return pltpu.make_async_copy(hm_ref.at[pl.ds(t, 1)], out_ref.at[pl.ds(pos_ref[t * TOP_K + k], 1)], sem)

    def issue(t, c):
        for k in range(TOP_K):
            copy(t, k).start()
        return c

    def drain(t, c):
        for k in range(TOP_K):
            copy(t, k).wait()
        return c

    lax.fori_loop(0, tm, issue, 0)
    lax.fori_loop(0, tm, drain, 0)


def _permute(hm, pos_flat, dest, tm):
    n, w = hm.shape
    return pl.pallas_call(
        functools.partial(_permute_kernel, tm=tm),
        grid=(n // tm,),
        in_specs=[pl.BlockSpec((tm * TOP_K,), lambda i: (i,), memory_space=pltpu.SMEM),
                  pl.BlockSpec((tm, w), lambda i: (i, 0)),
                  pl.BlockSpec(memory_space=pl.ANY)],
        out_specs=pl.BlockSpec(memory_space=pl.ANY),
        out_shape=jax.ShapeDtypeStruct(dest.shape, dest.dtype),
        scratch_shapes=[pltpu.SemaphoreType.DMA],
        input_output_aliases={2: 0},
        compiler_params=_params(("arbitrary",)),
        name="moe_permute",
    )(pos_flat, hm, dest)


def _unpermute_kernel(pos_ref, ys_ref, out_ref, sem, *, tm):
    def copy(t, k):
        return pltpu.make_async_copy(ys_ref.at[pl.ds(pos_ref[t * TOP_K + k], 1)], out_ref.at[k, pl.ds(t, 1)], sem)

    def issue(t, c):
        for k in range(TOP_K):
            copy(t, k).start()
        return c

    def drain(t, c):
        for k in range(TOP_K):
            copy(t, k).wait()
        return c

    lax.fori_loop(0, tm, issue, 0)
    lax.fori_loop(0, tm, drain, 0)


def _unpermute(ys, pos_flat, n, tm):
    _, w = ys.shape
    return pl.pallas_call(
        functools.partial(_unpermute_kernel, tm=tm),
        grid=(n // tm,),
        in_specs=[pl.BlockSpec((tm * TOP_K,), lambda i: (i,), memory_space=pltpu.SMEM),
                  pl.BlockSpec(memory_space=pl.ANY)],
        out_specs=pl.BlockSpec((TOP_K, tm, w), lambda i: (0, i, 0)),
        out_shape=jax.ShapeDtypeStruct((TOP_K, n, w), ys.dtype),
        scratch_shapes=[pltpu.SemaphoreType.DMA],
        compiler_params=_params(("arbitrary",)),
        name="moe_unpermute",
    )(pos_flat, ys)


def _sc_mesh():
    return plsc.VectorSubcoreMesh(core_axis_name="c", subcore_axis_name="s")


def _sc_worker():
    return lax.axis_index("s") * SC_CORES + lax.axis_index("c")


def _sc_scatter_rows(rows, idx, m_pad):
    n, w = rows.shape
    per_w = n // SC_WORKERS
    n_chunks = per_w // SC_CHUNK

    def body(rows_hbm, idx_hbm, out_hbm, idx_v, buf, g0, g1, o0, o1):
        gsem, osem = (g0, g1), (o0, o1)
        wid = _sc_worker()
        base = wid * per_w
        pltpu.sync_copy(idx_hbm.at[wid], idx_v)

        def get(c, slot):
            return pltpu.make_async_copy(rows_hbm.at[pl.ds(base + c * SC_CHUNK, SC_CHUNK)], buf.at[slot], gsem[slot])

        def put(c, k, slot):
            return pltpu.make_async_copy(buf.at[slot], out_hbm.at[idx_v.at[c * TOP_K + k]], osem[slot])

        get(0, 0).start()
        for c in range(n_chunks):
            slot = c % 2
            get(c, slot).wait()
            if c + 1 < n_chunks:
                if c >= 1:
                    for k in range(TOP_K):
                        put(c - 1, k, 1 - slot).wait()
                get(c + 1, 1 - slot).start()
            for k in range(TOP_K):
                put(c, k, slot).start()
        for c in range(max(n_chunks - 2, 0), n_chunks):
            for k in range(TOP_K):
                put(c, k, c % 2).wait()

    return pl.kernel(
        body, mesh=_sc_mesh(),
        out_type=jax.ShapeDtypeStruct((m_pad, w), rows.dtype),
        scratch_types=[pltpu.VMEM((n_chunks * TOP_K, SC_CHUNK), jnp.int32), pltpu.VMEM((2, SC_CHUNK, w), rows.dtype)]
                      + [pltpu.SemaphoreType.DMA] * 4,
        name="moe_sc_scatter",
    )(rows, idx)


def _sc_gather_rows(table, idx):
    b = idx.shape[0]
    w = table.shape[1]
    per_w = b // SC_WORKERS
    n_chunks = per_w // SC_CHUNK

    def body(table_hbm, idx_hbm, out_hbm, idx_v, buf, g0, g1, o0, o1):
        gsem, osem = (g0, g1), (o0, o1)
        base = _sc_worker() * per_w
        pltpu.sync_copy(idx_hbm.at[pl.ds(base, per_w)], idx_v)

        def get(c, slot):
            return pltpu.make_async_copy(table_hbm.at[idx_v.at[pl.ds(c * SC_CHUNK, SC_CHUNK)]], buf.at[slot],
                                         gsem[slot])

        def put(c, slot):
            return pltpu.make_async_copy(buf.at[slot], out_hbm.at[pl.ds(base + c * SC_CHUNK, SC_CHUNK)], osem[slot])

        get(0, 0).start()
        for c in range(n_chunks):
            slot = c % 2
            get(c, slot).wait()
            if c + 1 < n_chunks:
                if c >= 1:
                    put(c - 1, 1 - slot).wait()
                get(c + 1, 1 - slot).start()
            put(c, slot).start()
        for c in range(max(n_chunks - 2, 0), n_chunks):
            put(c, c % 2).wait()

    return pl.kernel(
        body, mesh=_sc_mesh(),
        out_type=jax.ShapeDtypeStruct((b, w), table.dtype),
        scratch_types=[pltpu.VMEM((per_w,), jnp.int32), pltpu.VMEM((2, SC_CHUNK, w), table.dtype)]
                      + [pltpu.SemaphoreType.DMA] * 4,
        name="moe_sc_gather",
    )(table, idx)


def _expert_kernel(te_ref, nu_ref, nv_ref, xs_ref, wgu_ref, bgu_ref, wdn_ref, bdn_ref, ys_ref, wgu_bf, wdn_bf, *, d):
    j = pl.program_id(0)
    e = te_ref[j]
    prev = te_ref[jnp.maximum(j - 1, 0)]

    @pl.when(jnp.logical_or(j == 0, e != prev))
    def _():
        rows = 128

        def cast(c, carry):
            r0 = pl.multiple_of(c * rows, rows)
            wgu_bf[pl.ds(r0, rows), :] = wgu_ref[pl.ds(r0, rows), :].astype(BF16)
            wdn_bf[pl.ds(r0, rows), :] = wdn_ref[pl.ds(r0, rows), :].astype(BF16)
            return carry

        lax.fori_loop(0, d // rows, cast, 0)

    @pl.when(j < nu_ref[0])
    def _():
        rowi = lax.broadcasted_iota(jnp.int32, xs_ref.shape, 0)
        xw = jnp.where(rowi < nv_ref[j], xs_ref[...], jnp.uint32(0))
        gu = jnp.dot(_unpack_bf16(xw), wgu_bf[...], preferred_element_type=F32) + bgu_ref[...]
        g = jnp.minimum(gu[:, :d], SWIGLU_LIMIT)
        u = jnp.clip(gu[:, d:], -SWIGLU_LIMIT, SWIGLU_LIMIT)
        act = (u + 1.0) * (g * _sigmoid(SWIGLU_ALPHA * g))
        y = jnp.dot(act.astype(BF16), wdn_bf[...], preferred_element_type=F32) + bdn_ref[...]
        ys_ref[...] = _pack_pairs(y)

    @pl.when(j >= nu_ref[0])
    def _():
        ys_ref[...] = jnp.zeros_like(ys_ref)


def _experts(xs, tile_expert, n_used, n_valid, w_gu, b_gu, w_dn, b_dn, layer, tm=EXPERT_TM):
    m_pad, half = xs.shape
    d = 2 * half
    nl, ne = w_gu.shape[:2]
    rows = lambda j, te, nu, nv: (jnp.minimum(j, nu[0] - 1), 0)
    wsel = lambda j, te, nu, nv: (layer, te[j], 0, 0)
    grid_spec = pltpu.PrefetchScalarGridSpec(
        num_scalar_prefetch=3,
        grid=(m_pad // tm,),
        in_specs=[pl.BlockSpec((tm, half), rows),
                  pl.BlockSpec((None, None, d, 2 * d), wsel),
                  pl.BlockSpec((None, None, 1, 2 * d), wsel),
                  pl.BlockSpec((None, None, d, d), wsel),
                  pl.BlockSpec((None, None, 1, d), wsel)],
        out_specs=pl.BlockSpec((tm, half), lambda j, te, nu, nv: (j, 0)),
        scratch_shapes=[pltpu.VMEM((d, 2 * d), BF16), pltpu.VMEM((d, d), BF16)],
    )
    return pl.pallas_call(
        functools.partial(_expert_kernel, d=d),
        grid_spec=grid_spec,
        out_shape=jax.ShapeDtypeStruct((m_pad, half), U32),
        compiler_params=_params(("arbitrary",)),
        name="moe_experts",
    )(tile_expert, n_used, n_valid, xs, w_gu, b_gu.reshape(nl, ne, 1, 2 * d), w_dn, b_dn.reshape(nl, ne, 1, d))


def _combine_kernel(x_ref, y_ref, tw_ref, mod_ref, lng_ref, lnb_ref, o_ref, *, d, alpha):
    tw = tw_ref[...]
    m_hi = None
    m_lo = None
    for k in range(TOP_K):
        hi, lo = _unpack_pairs(y_ref[k])
        wk = tw[:, k:k + 1]
        m_hi = wk * hi if m_hi is None else m_hi + wk * hi
        m_lo = wk * lo if m_lo is None else m_lo + wk * lo
    m = jnp.concatenate([m_hi, m_lo], axis=1)
    gate = mod_ref[...][:, 2 * d:]
    o_ref[...] = _layer_norm(alpha * x_ref[...] + (1.0 + gate) * m, lng_ref[...], lnb_ref[...])


def _combine(x, y4, tw, mod, ln_g, ln_b, tm, alpha):
    bsz, s, d = x.shape
    nt = s // tm
    tok = pl.BlockSpec((None, tm, d), lambda b, t: (b, t, 0))
    return pl.pallas_call(
        functools.partial(_combine_kernel, d=d, alpha=alpha),
        grid=(bsz, nt),
        in_specs=[tok,
                  pl.BlockSpec((TOP_K, tm, d // 2), lambda b, t: (0, b * nt + t, 0)),
                  pl.BlockSpec((tm, LANES), lambda b, t: (b * nt + t, 0)),
                  pl.BlockSpec((None, 1, 3 * d), lambda b, t: (b, 0, 0)),
                  _full((1, d)), _full((1, d))],
        out_specs=tok,
        out_shape=jax.ShapeDtypeStruct((bsz, s, d), F32),
        compiler_params=_params(("arbitrary", "arbitrary")),
        name="moe_combine",
    )(x, y4, tw, mod, ln_g, ln_b)


def _moe(hm_p, hm_s, w_r, b_r, w_gu, b_gu, w_dn, b_dn, layer):
    half = hm_p.shape[-1]
    hm_p = hm_p.reshape(-1, half)
    hm_s = hm_s.reshape(-1, half)
    n_p, n_s = hm_p.shape[0], hm_s.shape[0]
    ne = w_r.shape[1]
    w_r = w_r.astype(BF16)
    b_r = b_r.reshape(1, ne)
    eid_p, tw_p, rank_p, cnt_p = _router(hm_p, w_r, b_r, jnp.zeros((1, ne), F32), PROMPT_TM)
    eid_s, tw_s, rank_s, cnt = _router(hm_s, w_r, b_r, cnt_p, n_s)
    tm = EXPERT_TM
    cnt = cnt.reshape(ne).astype(jnp.int32)
    gsz = ((cnt + tm - 1) // tm) * tm
    ends = jnp.cumsum(gsz)
    offs = ends - gsz
    pos_p = (offs[eid_p[:, :TOP_K]] + rank_p[:, :TOP_K]).reshape(-1)
    pos_s = (offs[eid_s[:, :TOP_K]] + rank_s[:, :TOP_K]).reshape(-1)
    n_tiles = ((n_p + n_s) * TOP_K + ne * (tm - 1)) // tm + 1
    m_pad = n_tiles * tm
    tile_start = jnp.arange(n_tiles, dtype=jnp.int32) * tm
    tile_expert = jnp.minimum(jnp.sum((tile_start[:, None] >= ends[None, :]).astype(jnp.int32), axis=1), ne - 1)
    n_used = (ends[-1] // tm).astype(jnp.int32).reshape(1)
    n_valid = jnp.clip((offs + cnt)[tile_expert] - tile_start, 0, tm).astype(jnp.int32)
    as_i32 = lambda a: lax.bitcast_convert_type(a, jnp.int32)
    as_u32 = lambda a: lax.bitcast_convert_type(a, U32)
    pos2 = pos_p.reshape(n_p, TOP_K)
    idx_scatter = pos2.reshape(SC_WORKERS, -1, SC_CHUNK, TOP_K).transpose(0, 1, 3, 2).reshape(SC_WORKERS, -1, SC_CHUNK)
    xs = as_u32(_sc_scatter_rows(as_i32(hm_p), idx_scatter, m_pad))
    xs = _permute(hm_s, pos_s, xs, n_s)
    ys = _experts(xs, tile_expert, n_used, n_valid, w_gu, b_gu, w_dn, b_dn, layer)
    y4_p = as_u32(_sc_gather_rows(as_i32(ys), pos2.T.reshape(-1))).reshape(TOP_K, n_p, half)
    return (y4_p, tw_p), (_unpermute(ys, pos_s, n_s, n_s), tw_s)


def _qkv_kernel(x_ref, modk_ref, modq_ref, cos_ref, sin_ref, wdkv_ref, gkv_ref, wdq_ref, gq_ref, wuq_ref,
                ckv_ref, kpe_ref, kc_ref, kp_ref, qh_ref, *, d, kvl, rope, nh, nope, q_scale):
    x = x_ref[...]
    cos = cos_ref[...]
    sin = sin_ref[...]
    modk = modk_ref[...]
    hk = (x * (1.0 + modk[:, d:2 * d]) + modk[:, :d]).astype(BF16)
    kv = jnp.dot(hk, wdkv_ref[...], preferred_element_type=F32)
    c = kv[:, :kvl]
    ckv = c * lax.rsqrt(jnp.mean(c * c, axis=-1, keepdims=True) + RMS_EPS) * gkv_ref[...]
    kp = kv[:, kvl:kvl + LANES] * cos + kv[:, kvl + LANES:kvl + 2 * LANES] * sin
    ckv_ref[...] = ckv
    kpe_ref[...] = kp[:, :rope]
    kc_ref[...] = ckv.astype(BF16)
    kp_ref[...] = kp[:, :rope].astype(BF16)

    modq = modq_ref[...]
    hq = (x * (1.0 + modq[:, d:2 * d]) + modq[:, :d]).astype(BF16)
    qd = jnp.dot(hq, wdq_ref[...], preferred_element_type=F32)
    qn = (qd * lax.rsqrt(jnp.mean(qd * qd, axis=-1, keepdims=True) + RMS_EPS) * gq_ref[...]).astype(BF16)
    q = jnp.dot(qn, wuq_ref[...], preferred_element_type=F32)
    for h in range(nh):
        qh_ref[h, :, :nope] = (q[:, h * nope:(h + 1) * nope] * q_scale).astype(BF16)
    pe0 = nh * nope
    sw0 = pe0 + nh * rope
    per = LANES // rope
    for j in range(nh // per):
        r2 = (q[:, pe0 + j * LANES:pe0 + (j + 1) * LANES] * cos
              + q[:, sw0 + j * LANES:sw0 + (j + 1) * LANES] * sin) * q_scale
        for i in range(per):
            qh_ref[j * per + i, :, nope:] = r2[:, i * rope:(i + 1) * rope].astype(BF16)


def _qkv(x, modk, modq, cos_t, sin_t, w, tm, dims):
    bsz, s, d = x.shape
    kvl, rope, nh, nope = dims["kvl"], dims["rope"], dims["nh"], dims["nope"]
    tok = lambda width: pl.BlockSpec((None, tm, width), lambda b, t: (b, t, 0))
    head = lambda width: pl.BlockSpec((None, nh, tm, width), lambda b, t: (b, 0, t, 0))
    tab = pl.BlockSpec((tm, LANES), lambda b, t: (t, 0))
    weights = [w["w_dkv"], w["g_kv"], w["w_dq"], w["g_q"], w["w_uq"]]
    kern = functools.partial(_qkv_kernel, d=d, kvl=kvl, rope=rope, nh=nh, nope=nope, q_scale=dims["q_scale"])
    return pl.pallas_call(
        kern,
        grid=(bsz, s // tm),
        in_specs=[tok(d),
                  pl.BlockSpec((None, 1, 2 * d), lambda b, t: (b, 0, 0)),
                  pl.BlockSpec((None, 1, 3 * d), lambda b, t: (b, 0, 0)),
                  tab, tab] + [_full(a.shape) for a in weights],
        out_specs=[tok(kvl), tok(rope), tok(kvl), tok(rope), head(nope + rope)],
        out_shape=[jax.ShapeDtypeStruct((bsz, s, kvl), F32), jax.ShapeDtypeStruct((bsz, s, rope), F32),
                   jax.ShapeDtypeStruct((bsz, s, kvl), BF16), jax.ShapeDtypeStruct((bsz, s, rope), BF16),
                   jax.ShapeDtypeStruct((bsz, nh, s, nope + rope), BF16)],
        compiler_params=_params(("arbitrary", "arbitrary")),
        name="mla_qkv",
    )(x, modk, modq, cos_t, sin_t, *weights)


def _kproj_kernel(kc_ref, kp_ref, wuk_ref, kh_ref, *, nh, nope):
    kc = kc_ref[...]
    kp = kp_ref[...]
    nt = (((1,), (1,)), ((), ()))
    for h in range(nh):
        kh_ref[h, :, :nope] = lax.dot_general(kc, wuk_ref[h], nt, preferred_element_type=F32).astype(BF16)
        kh_ref[h, :, nope:] = kp


def _kproj(kc, kp, w_uk, tm):
    bsz, s, kvl = kc.shape
    rope = kp.shape[-1]
    nh, nope, _ = w_uk.shape
    tok = lambda width: pl.BlockSpec((None, tm, width), lambda b, t: (b, t, 0))
    return pl.pallas_call(
        functools.partial(_kproj_kernel, nh=nh, nope=nope),
        grid=(bsz, s // tm),
        in_specs=[tok(kvl), tok(rope), _full(w_uk.shape)],
        out_specs=pl.BlockSpec((None, nh, tm, nope + rope), lambda b, t: (b, 0, t, 0)),
        out_shape=jax.ShapeDtypeStruct((bsz, nh, s, nope + rope), BF16),
        compiler_params=_params(("arbitrary", "arbitrary")),
        name="mla_kproj",
    )(kc, kp, w_uk)


def _attn_kernel(qi_ref, kj_ref, last_ref, edge_ref, qh_ref, kh_ref, kc_ref, o_ref, m_s, l_s, acc_s,
                 *, nh, tq, tk, kvl, causal, valid):
    s_id = pl.program_id(1)
    qi = qi_ref[s_id]
    kj = kj_ref[s_id]
    nt = (((1,), (1,)), ((), ()))

    @pl.when(kj == 0)
    def _():
        m_s[...] = jnp.full_like(m_s, -jnp.inf)
        l_s[...] = jnp.zeros_like(l_s)
        acc_s[...] = jnp.zeros_like(acc_s)

    def sweep(masked):
        kc = kc_ref[...]
        if masked:
            kidx = kj * tk + lax.broadcasted_iota(jnp.int32, (tq, tk), 1)
            if causal:
                qidx = qi * tq + lax.broadcasted_iota(jnp.int32, (tq, tk), 0)
                visible = (kidx // CHUNK) <= (qidx // CHUNK)
            else:
                visible = kidx < valid
        for h in range(nh):
            s = lax.dot_general(qh_ref[h], kh_ref[h], nt, preferred_element_type=F32)
            if masked:
                s = jnp.where(visible, s, -jnp.inf)
            slabs = [s[:, j * LANES:(j + 1) * LANES] for j in range(tk // LANES)]
            mx = slabs[0]
            for sj in slabs[1:]:
                mx = jnp.maximum(mx, sj)
            m_prev = m_s[h]
            m_new = jnp.maximum(m_prev, jnp.max(mx, axis=1, keepdims=True))
            scale = jnp.exp2(m_prev - m_new)
            ps = []
            rsum = None
            for sj in slabs:
                pj = jnp.exp2(sj - m_new)
                ps.append(pj.astype(BF16))
                rsum = pj if rsum is None else rsum + pj
            p = jnp.concatenate(ps, axis=1)
            l_s[h] = scale * l_s[h] + jnp.sum(rsum, axis=1, keepdims=True)
            pv = jnp.dot(p, kc, preferred_element_type=F32)
            acc_s[h] = acc_s[h] * jnp.concatenate([scale] * (kvl // LANES), axis=1) + pv
            m_s[h] = m_new

    edge = edge_ref[s_id]

    @pl.when(edge == 1)
    def _():
        sweep(True)

    @pl.when(edge == 0)
    def _():
        sweep(False)

    @pl.when(last_ref[s_id] == 1)
    def _():
        for h in range(nh):
            inv = jnp.concatenate([l_s[h]] * (kvl // LANES), axis=1)
            o_ref[h] = (acc_s[h] / inv).astype(BF16)


def _attention(qh, kh, kc, tq, tk, causal, valid):
    bsz, nh, s, qk = qh.shape
    skv, kvl = kc.shape[1], kc.shape[2]
    qi, kj, last, edge = [], [], [], []
    for i in range(s // tq):
        hi = ((i * tq + tq - 1) // tk) if causal else (skv // tk - 1)
        for j in range(hi + 1):
            qi.append(i)
            kj.append(j)
            last.append(1 if j == hi else 0)
            if causal:
                edge.append(1 if ((j + 1) * tk - 1) // CHUNK > (i * tq) // CHUNK else 0)
            else:
                edge.append(1 if (j + 1) * tk > valid else 0)
    tabs = [jnp.asarray(np.array(a, np.int32)) for a in (qi, kj, last, edge)]
    qspec = lambda width: pl.BlockSpec((None, nh, tq, width), lambda b, t, qi, kj, last, edge: (b, 0, qi[t], 0))
    khspec = pl.BlockSpec((None, nh, tk, qk), lambda b, t, qi, kj, last, edge: (b, 0, kj[t], 0))
    kcspec = pl.BlockSpec((None, tk, kvl), lambda b, t, qi, kj, last, edge: (b, kj[t], 0))
    grid_spec = pltpu.PrefetchScalarGridSpec(
        num_scalar_prefetch=4,
        grid=(bsz, len(qi)),
        in_specs=[qspec(qk), khspec, kcspec],
        out_specs=qspec(kvl),
        scratch_shapes=[pltpu.VMEM((nh, tq, LANES), F32), pltpu.VMEM((nh, tq, LANES), F32),
                        pltpu.VMEM((nh, tq, kvl), F32)],
    )
    kern = functools.partial(_attn_kernel, nh=nh, tq=tq, tk=tk, kvl=kvl, causal=causal, valid=valid)
    return pl.pallas_call(
        kern,
        grid_spec=grid_spec,
        out_shape=jax.ShapeDtypeStruct((bsz, nh, s, kvl), BF16),
        compiler_params=_params(("arbitrary", "arbitrary")),
        name="mla_attention",
    )(*tabs, qh, kh, kc)


def _attn_out_kernel(o_ref, x_ref, mod_ref, mod2_ref, wuv_ref, wo_ref, lng_ref, lnb_ref, x3_ref, hm_ref,
                     *, d, nh, alpha):
    parts = [jnp.dot(o_ref[h], wuv_ref[h], preferred_element_type=F32) for h in range(nh)]
    o = jnp.concatenate(parts, axis=1).astype(BF16)
    m = jnp.dot(o, wo_ref[...], preferred_element_type=F32)
    gate = mod_ref[...][:, 2 * d:]
    x3 = _layer_norm(alpha * x_ref[...] + (1.0 + gate) * m, lng_ref[...], lnb_ref[...])
    x3_ref[...] = x3
    mod2 = mod2_ref[...]
    hm_ref[...] = _pack_pairs(x3 * (1.0 + mod2[:, d:2 * d]) + mod2[:, :d])


def _attn_out(o_lat, x, mod, mod2, w_uv, w_o, ln_g, ln_b, tm, alpha):
    bsz, s, d = x.shape
    nh, kvl = o_lat.shape[1], o_lat.shape[3]
    tok = lambda width: pl.BlockSpec((None, tm, width), lambda b, t: (b, t, 0))
    modspec = pl.BlockSpec((None, 1, 3 * d), lambda b, t: (b, 0, 0))
    return pl.pallas_call(
        functools.partial(_attn_out_kernel, d=d, nh=nh, alpha=alpha),
        grid=(bsz, s // tm),
        in_specs=[pl.BlockSpec((None, nh, tm, kvl), lambda b, t: (b, 0, t, 0)), tok(d), modspec, modspec,
                  _full(w_uv.shape), _full(w_o.shape), _full((1, d)), _full((1, d))],
        out_specs=[tok(d), tok(d // 2)],
        out_shape=[jax.ShapeDtypeStruct((bsz, s, d), F32), jax.ShapeDtypeStruct((bsz, s, d // 2), U32)],
        compiler_params=_params(("arbitrary", "arbitrary")),
        name="mla_out",
    )(o_lat, x, mod, mod2, w_uv, w_o, ln_g, ln_b)


def _rope_tables(pos, rope):
    half = rope // 2
    inv = ROPE_THETA ** (-2.0 * jnp.arange(half, dtype=F32) / rope)
    ang = pos.astype(F32)[:, None] * inv[None, :]
    cos, sin = jnp.cos(ang), jnp.sin(ang)
    rep = LANES // rope
    return (jnp.concatenate([cos, cos] * rep, axis=1), jnp.concatenate([-sin, sin] * rep, axis=1))


def kernel(x_prompt, x_sample, state_conv, state_rglru, cache_ckv, cache_kpe, c_prompt, c_sample, w_ada, b_ada, ln_g, ln_b, lru_w_in, lru_b_in, lru_conv_w, lru_conv_b, lru_w_a, lru_b_a, lru_w_i, lru_b_i, lru_lambda, lru_w_out, lru_b_out, kv_w_ada, kv_b_ada, mla_w_dkv, mla_g_kv, mla_w_uk, mla_w_uv, mla_w_dq, mla_g_q, mla_w_uq, mla_w_o, moe_w_r, moe_b_r, moe_w_gu, moe_b_gu, moe_w_dn, moe_b_dn):
    bp, sp, d = x_prompt.shape
    bs, ss, _ = x_sample.shape
    depth = w_ada.shape[0]
    alpha = float((2.0 * depth) ** 0.25)
    nh, nope, kvl = mla_w_uk.shape
    rope = cache_kpe.shape[-1]
    past = cache_ckv.shape[1]
    hist = state_conv.shape[2]
    dims = dict(kvl=kvl, rope=rope, nh=nh, nope=nope, q_scale=float((nope + rope) ** -0.5 * LOG2E))
    row = lambda v: v.reshape(1, -1)

    nrow = bp + bs
    rpad = -nrow % (2 * SUBLANES)
    c_rows = jnp.concatenate([c_prompt, c_sample, jnp.zeros((rpad, d), F32)], axis=0)
    mods = _ada(c_rows, w_ada.reshape(depth * 2, d, 3 * d), b_ada.reshape(depth * 2, 1, 3 * d))
    modkv = _ada(c_rows, kv_w_ada.reshape(1, d, 2 * d), kv_b_ada.reshape(1, 1, 2 * d))[0]
    mod_p = lambda i: mods[i, :bp].reshape(bp, 1, 3 * d)
    mod_s = lambda i: mods[i, bp:nrow].reshape(bs, 1, 3 * d)

    l0w = dict(w_in=lru_w_in[0].astype(BF16), b_in=row(lru_b_in[0]), conv_w=lru_conv_w[0], conv_b=row(lru_conv_b[0]),
               w_a=lru_w_a[0].astype(BF16), b_a=row(lru_b_a[0]), w_i=lru_w_i[0].astype(BF16), b_i=row(lru_b_i[0]),
               lam=row(lru_lambda[0]), w_out=lru_w_out[0].astype(BF16), b_out=row(lru_b_out[0]),
               ln_g=row(ln_g[0, 0]), ln_b=row(ln_b[0, 0]))
    x1p, hmp, conv_p, h_p = _l0_mixer(x_prompt, mod_p(0), mod_p(1), jnp.zeros((bp, hist, d), F32),
                                      jnp.zeros((bp, 1, d), F32), l0w, L0_TB, alpha)
    x1s, hms, conv_s, h_s = _l0_mixer(x_sample, mod_s(0), mod_s(1), state_conv[0], state_rglru[0].reshape(bs, 1, d),
                                      l0w, ss, alpha)

    (y4p, twp), (y4s, tws) = _moe(hmp, hms, moe_w_r[0], moe_b_r[0], moe_w_gu, moe_b_gu, moe_w_dn, moe_b_dn, 0)
    x2p = _combine(x1p, y4p, twp, mod_p(1), row(ln_g[0, 1]), row(ln_b[0, 1]), PROMPT_TM, alpha)
    x2s = _combine(x1s, y4s, tws, mod_s(1), row(ln_g[0, 1]), row(ln_b[0, 1]), ss, alpha)

    pe_cols = jnp.arange(rope)
    sw_cols = jnp.concatenate([pe_cols[rope // 2:], pe_cols[:rope // 2]])
    zpad = jnp.zeros((d, LANES - rope), F32)
    w_dkv_ext = jnp.concatenate([mla_w_dkv[:, :kvl], mla_w_dkv[:, kvl:], zpad,
                                 mla_w_dkv[:, kvl:][:, sw_cols], zpad], axis=1).astype(BF16)
    wq = mla_w_uq[0].reshape(-1, nh, nope + rope)
    w_uq_ext = jnp.concatenate([wq[:, :, :nope].reshape(-1, nh * nope), wq[:, :, nope:].reshape(-1, nh * rope),
                                wq[:, :, nope:][:, :, sw_cols].reshape(-1, nh * rope)], axis=1).astype(BF16)
    qw = dict(w_dkv=w_dkv_ext, g_kv=row(mla_g_kv), w_dq=mla_w_dq[0].astype(BF16), g_q=row(mla_g_q[0]),
              w_uq=w_uq_ext)
    w_uk = mla_w_uk.astype(BF16)
    cos_p, sin_p = _rope_tables(jnp.arange(sp), rope)
    cos_s, sin_s = _rope_tables(past + jnp.arange(ss), rope)
    modkv_p = modkv[:bp].reshape(bp, 1, 2 * d)
    modkv_s = modkv[bp:nrow].reshape(bs, 1, 2 * d)
    ckv_p, kpe_p, kc_p, kp_p, qh_p = _qkv(x2p, modkv_p, mod_p(2), cos_p, sin_p, qw, PROMPT_TM, dims)
    ckv_s, kpe_s, kc_s, kp_s, qh_s = _qkv(x2s, modkv_s, mod_s(2), cos_s, sin_s, qw, ss, dims)

    o_p = _attention(qh_p, _kproj(kc_p, kp_p, w_uk, PROMPT_TM), kc_p, ATT_TQ, ATT_TK, True, sp)
    skv = past + ss
    kpad = -skv % LANES
    kc_all = jnp.concatenate([cache_ckv.astype(BF16), kc_s, jnp.zeros((bs, kpad, kvl), BF16)], axis=1)
    kp_all = jnp.concatenate([cache_kpe.astype(BF16), kp_s, jnp.zeros((bs, kpad, rope), BF16)], axis=1)
    o_s = _attention(qh_s, _kproj(kc_all, kp_all, w_uk, skv + kpad), kc_all, ss, skv + kpad, False, skv)
    w_uv = mla_w_uv.astype(BF16)
    w_o = mla_w_o[0].astype(BF16)
    x3p, hm3p = _attn_out(o_p, x2p, mod_p(2), mod_p(3), w_uv, w_o, row(ln_g[1, 0]), row(ln_b[1, 0]), PROMPT_TM, alpha)
    x3s, hm3s = _attn_out(o_s, x2s, mod_s(2), mod_s(3), w_uv, w_o, row(ln_g[1, 0]), row(ln_b[1, 0]), ss, alpha)

    (y4p, twp), (y4s, tws) = _moe(hm3p, hm3s, moe_w_r[1], moe_b_r[1], moe_w_gu, moe_b_gu, moe_w_dn, moe_b_dn, 1)
    y_p = _combine(x3p, y4p, twp, mod_p(3), row(ln_g[1, 1]), row(ln_b[1, 1]), PROMPT_TM, alpha)
    y_s = _combine(x3s, y4s, tws, mod_s(3), row(ln_g[1, 1]), row(ln_b[1, 1]), ss, alpha)

    return (y_p, y_s, conv_p[None], h_p.reshape(1, bp, d), ckv_p, kpe_p,
            conv_s[None], h_s.reshape(1, bs, d), ckv_s, kpe_s)
```

```python
import functools

import numpy as np
import jax
import jax.numpy as jnp
from jax import lax
from jax.experimental import pallas as pl
from jax.experimental.pallas import tpu as pltpu
from jax.experimental.pallas import tpu_sc as plsc

F32 = jnp.float32
BF16 = jnp.bfloat16
I32 = jnp.int32

CHUNK = 64
N_LRU_BLOCKS = 8
LRU_C = 8.0
ROPE_THETA = 10000.0
TOP_K = 4
SWIGLU_ALPHA = 1.702
SWIGLU_LIMIT = 7.0
LN_EPS = 1e-5
RMS_EPS = 1e-6
LOG2E = 1.4426950408889634

LANES = 128
SUBLANES = 8
VMEM_LIMIT = 56 * 1024 * 1024

PROMPT_TM = 512
L0_TB = 256
EXPERT_TM = 512
SC_CORES = 2
SC_WORKERS = SC_CORES * 16
SC_CHUNK = 64
ATT_TQ = 512
ATT_TK = 1024


def _sigmoid(x):
    return 0.5 * (jnp.tanh(0.5 * x) + 1.0)


def _layer_norm(v, g, b):
    mu = jnp.mean(v, axis=-1, keepdims=True)
    d = v - mu
    var = jnp.mean(d * d, axis=-1, keepdims=True)
    return d * lax.rsqrt(var + LN_EPS) * g + b


def _pack_pairs(x):
    w = x.shape[1] // 2
    hi = lax.bitcast_convert_type(x[:, :w].astype(BF16).astype(F32), I32)
    lo = lax.bitcast_convert_type(x[:, w:].astype(BF16).astype(F32), I32)
    return hi | lax.shift_right_logical(lo, jnp.int32(16))


def _unpack_pairs(words):
    hi = lax.bitcast_convert_type(words & jnp.int32(-65536), F32)
    lo = lax.bitcast_convert_type(words << 16, F32)
    return hi, lo


def _unpack_bf16(words):
    hi, lo = _unpack_pairs(words)
    return jnp.concatenate([hi.astype(BF16), lo.astype(BF16)], axis=1)


def _full(shape):
    nd = len(shape)
    return pl.BlockSpec(shape, lambda *_: (0,) * nd)


def _params(sem, vmem=VMEM_LIMIT):
    return pltpu.CompilerParams(dimension_semantics=sem, vmem_limit_bytes=vmem)


def _ada_kernel(c_ref, w_ref, b_ref, o_ref):
    c = c_ref[...]
    s = (c * _sigmoid(c)).astype(BF16)
    o_ref[...] = jnp.dot(s, w_ref[...].astype(BF16), preferred_element_type=F32) + b_ref[...]


def _ada(c_rows, w, b, tn=1024):
    g, d, n = w.shape
    r = c_rows.shape[0]
    return pl.pallas_call(
        _ada_kernel,
        grid=(g, n // tn),
        in_specs=[_full((r, d)),
                  pl.BlockSpec((None, d, tn), lambda i, j: (i, 0, j)),
                  pl.BlockSpec((None, 1, tn), lambda i, j: (i, 0, j))],
        out_specs=pl.BlockSpec((None, r, tn), lambda i, j: (i, 0, j)),
        out_shape=jax.ShapeDtypeStruct((g, r, n), F32),
        compiler_params=_params(("arbitrary", "arbitrary")),
        name="ada_mod",
    )(c_rows, w, b)


def _l0_kernel(x_ref, mod_ref, mod2_ref, conv0_ref, h0_ref, win_ref, bin_ref, cw_ref, cb_ref,
               wa_ref, ba_ref, wi_ref, bi_ref, lam_ref, wout_ref, bout_ref, lng_ref, lnb_ref,
               x1_ref, hm_ref, convn_ref, hlast_ref,
               ubuf, a_s, b_s, h_s, *, tb, d, alpha):
    t = pl.program_id(1)
    pad = SUBLANES
    hist = convn_ref.shape[0]

    @pl.when(t == 0)
    def _():
        ubuf[pad - hist:pad, :] = conv0_ref[...]
        h_s[...] = h0_ref[...]

    x = x_ref[...]
    mod = mod_ref[...]
    shift, scale, gate = mod[:, :d], mod[:, d:2 * d], mod[:, 2 * d:]
    h = (x * (1.0 + scale) + shift).astype(BF16)
    proj = jnp.dot(h, win_ref[...], preferred_element_type=F32) + bin_ref[...]
    gate_b = proj[:, :d]
    ubuf[pad:pad + tb, :] = proj[:, d:]

    cw = cw_ref[...]
    u = cb_ref[...]
    for k in range(hist + 1):
        u = u + ubuf[pad - hist + k:pad - hist + k + tb, :] * cw[k:k + 1, :]
    ubuf[pad - hist:pad, :] = ubuf[pad + tb - hist:pad + tb, :]

    ub = u.astype(BF16)
    blk = d // N_LRU_BLOCKS
    lam = lam_ref[...]
    neg = -lam
    softplus = jnp.maximum(neg, 0.0) + jnp.log1p(jnp.exp(-jnp.abs(neg)))
    for n in range(N_LRU_BLOCKS):
        sl = slice(n * blk, (n + 1) * blk)
        un = ub[:, sl]
        r = _sigmoid(jnp.dot(un, wa_ref[n], preferred_element_type=F32) + ba_ref[:, sl])
        ig = _sigmoid(jnp.dot(un, wi_ref[n], preferred_element_type=F32) + bi_ref[:, sl])
        log_a = -LRU_C * r * softplus[:, sl]
        a = jnp.exp(log_a)
        one_m_a2 = -jnp.tanh(log_a) * (a * a + 1.0)
        a_s[:, sl] = a
        b_s[:, sl] = jnp.sqrt(one_m_a2) * (ig * u[:, sl])

    row = lax.broadcasted_iota(jnp.int32, (SUBLANES, d), 0)

    def group(g, hprev):
        r0 = pl.multiple_of(g * SUBLANES, SUBLANES)
        a = a_s[pl.ds(r0, SUBLANES), :]
        b = b_s[pl.ds(r0, SUBLANES), :]
        for sh in (1, 2, 4):
            keep = row >= sh
            a_sh = pltpu.roll(a, sh, axis=0)
            b_sh = pltpu.roll(b, sh, axis=0)
            b = jnp.where(keep, a * b_sh + b, b)
            a = jnp.where(keep, a * a_sh, a)
        hh = a * hprev + b
        b_s[pl.ds(r0, SUBLANES), :] = hh
        return hh[SUBLANES - 1:SUBLANES, :]

    h_s[...] = lax.fori_loop(0, tb // SUBLANES, group, h_s[...])

    y = b_s[...]
    gl = 0.5 * gate_b * (1.0 + jnp.tanh(0.7978845608028654 * (gate_b + 0.044715 * gate_b * gate_b * gate_b)))
    out = jnp.dot((gl * y).astype(BF16), wout_ref[...], preferred_element_type=F32) + bout_ref[...]
    x1 = _layer_norm(alpha * x + (1.0 + gate) * out, lng_ref[...], lnb_ref[...])
    x1_ref[...] = x1
    mod2 = mod2_ref[...]
    hm_ref[...] = _pack_pairs(x1 * (1.0 + mod2[:, d:2 * d]) + mod2[:, :d])

    @pl.when(t == pl.num_programs(1) - 1)
    def _():
        convn_ref[...] = ubuf[pad - hist:pad, :]
        hlast_ref[...] = h_s[...]


def _l0_mixer(x, mod, mod2, conv0, h0, w, tb, alpha):
    bsz, s, d = x.shape
    hist = conv0.shape[1]
    tok = lambda width: pl.BlockSpec((None, tb, width), lambda b, t: (b, t, 0))
    per_b = lambda rows, width: pl.BlockSpec((None, rows, width), lambda b, t: (b, 0, 0))
    kern = functools.partial(_l0_kernel, tb=tb, d=d, alpha=alpha)
    weights = [w["w_in"], w["b_in"], w["conv_w"], w["conv_b"], w["w_a"], w["b_a"], w["w_i"], w["b_i"],
               w["lam"], w["w_out"], w["b_out"], w["ln_g"], w["ln_b"]]
    return pl.pallas_call(
        kern,
        grid=(bsz, s // tb),
        in_specs=[tok(d), per_b(1, 3 * d), per_b(1, 3 * d), per_b(hist, d), per_b(1, d)]
                 + [_full(a.shape) for a in weights],
        out_specs=[tok(d), tok(d // 2), per_b(hist, d), per_b(1, d)],
        out_shape=[jax.ShapeDtypeStruct((bsz, s, d), F32), jax.ShapeDtypeStruct((bsz, s, d // 2), I32),
                   jax.ShapeDtypeStruct((bsz, hist, d), F32), jax.ShapeDtypeStruct((bsz, 1, d), F32)],
        scratch_shapes=[pltpu.VMEM((tb + SUBLANES, d), F32), pltpu.VMEM((tb, d), F32),
                        pltpu.VMEM((tb, d), F32), pltpu.VMEM((1, d), F32)],
        compiler_params=_params(("arbitrary", "arbitrary")),
        name="l0_mixer",
    )(x, mod, mod2, conv0, h0, *weights)


def _router_kernel(hm_ref, wr_ref, br_ref, base0_ref, eid_ref, tw_ref, rank_ref, cnt_ref, base, *, tm, ne):
    i = pl.program_id(0)

    @pl.when(i == 0)
    def _():
        base[...] = base0_ref[...]

    logits = jnp.dot(_unpack_bf16(hm_ref[...]), wr_ref[...], preferred_element_type=F32) + br_ref[...]
    col = lax.broadcasted_iota(jnp.int32, (tm, ne), 1).astype(F32)
    l = logits
    vals, idxs = [], []
    for _ in range(TOP_K):
        m = jnp.max(l, axis=-1, keepdims=True)
        idx = jnp.min(jnp.where(l == m, col, float(ne)), axis=-1, keepdims=True)
        vals.append(m)
        idxs.append(idx)
        l = jnp.where(col == idx, -jnp.inf, l)
    es = [jnp.exp(v - vals[0]) for v in vals]
    tot = es[0] + es[1] + es[2] + es[3]

    ri = lax.broadcasted_iota(jnp.int32, (tm, tm), 0)
    ci = lax.broadcasted_iota(jnp.int32, (tm, tm), 1)
    tri = (ci < ri).astype(BF16)
    run = base[...]
    lane = lax.broadcasted_iota(jnp.int32, (tm, LANES), 1)
    eid_o = jnp.zeros((tm, LANES), F32)
    tw_o = jnp.zeros((tm, LANES), F32)
    rk_o = jnp.zeros((tm, LANES), F32)
    for k in range(TOP_K):
        oh = col == idxs[k]
        ohf = oh.astype(F32)
        before = jnp.dot(tri, ohf.astype(BF16), preferred_element_type=F32)
        rank = jnp.sum(jnp.where(oh, before + run, 0.0), axis=-1, keepdims=True)
        run = run + jnp.sum(ohf, axis=0, keepdims=True)
        sel = lane == k
        eid_o = jnp.where(sel, idxs[k], eid_o)
        tw_o = jnp.where(sel, es[k] / tot, tw_o)
        rk_o = jnp.where(sel, rank, rk_o)
    base[...] = run
    eid_ref[...] = eid_o.astype(jnp.int32)
    tw_ref[...] = tw_o
    rank_ref[...] = rk_o.astype(jnp.int32)
    cnt_ref[...] = run


def _router(hm, w_r, b_r, base0, tm):
    n, half = hm.shape
    ne = w_r.shape[1]
    kern = functools.partial(_router_kernel, tm=tm, ne=ne)
    tokrow = pl.BlockSpec((tm, LANES), lambda i: (i, 0))
    return pl.pallas_call(
        kern,
        grid=(n // tm,),
        in_specs=[pl.BlockSpec((tm, half), lambda i: (i, 0)), _full((2 * half, ne)), _full((1, ne)), _full((1, ne))],
        out_specs=[tokrow, tokrow, tokrow, _full((1, ne))],
        out_shape=[jax.ShapeDtypeStruct((n, LANES), jnp.int32), jax.ShapeDtypeStruct((n, LANES), F32),
                   jax.ShapeDtypeStruct((n, LANES), jnp.int32), jax.ShapeDtypeStruct((1, ne), F32)],
        scratch_shapes=[pltpu.VMEM((1, ne), F32)],
        compiler_params=_params(("arbitrary",)),
        name="moe_router",
    )(hm, w_r, b_r, base0)


def _permute_kernel(pos_ref, hm_ref, init_ref, out_ref, sem, *, tm):
    del init_ref

    def copy(t, k):
        return pltpu.make_async_copy(hm_ref.at[pl.ds(t, 1)], out_ref.at[pl.ds(pos_ref[t * TOP_K + k], 1)], sem)

    def issue(t, c):
        for k in range(TOP_K):
            copy(t, k).start()
        return c

    def drain(t, c):
        for k in range(TOP_K):
            copy(t, k).wait()
        return c

    lax.fori_loop(0, tm, issue, 0)
    lax.fori_loop(0, tm, drain, 0)


def _permute(hm, pos_flat, dest, tm):
    n, w = hm.shape
    return pl.pallas_call(
        functools.partial(_permute_kernel, tm=tm),
        grid=(n // tm,),
        in_specs=[pl.BlockSpec((tm * TOP_K,), lambda i: (i,), memory_space=pltpu.SMEM),
                  pl.BlockSpec((tm, w), lambda i: (i, 0)),
                  pl.BlockSpec(memory_space=pl.ANY)],
        out_specs=pl.BlockSpec(memory_space=pl.ANY),
        out_shape=jax.ShapeDtypeStruct(dest.shape, dest.dtype),
        scratch_shapes=[pltpu.SemaphoreType.DMA],
        input_output_aliases={2: 0},
        compiler_params=_params(("arbitrary",)),
        name="moe_permute",
    )(pos_flat, hm, dest)


def _unpermute_kernel(pos_ref, ys_ref, out_ref, sem, *, tm):
    def copy(t, k):
        return pltpu.make_async_copy(ys_ref.at[pl.ds(pos_ref[t * TOP_K + k], 1)], out_ref.at[k, pl.ds(t, 1)], sem)

    def issue(t, c):
        for k in range(TOP_K):
            copy(t, k).start()
        return c

    def drain(t, c):
        for k in range(TOP_K):
            copy(t, k).wait()
        return c

    lax.fori_loop(0, tm, issue, 0)
    lax.fori_loop(0, tm, drain, 0)


def _unpermute(ys, pos_flat, n, tm):
    _, w = ys.shape
    return pl.pallas_call(
        functools.partial(_unpermute_kernel, tm=tm),
        grid=(n // tm,),
        in_specs=[pl.BlockSpec((tm * TOP_K,), lambda i: (i,), memory_space=pltpu.SMEM),
                  pl.BlockSpec(memory_space=pl.ANY)],
        out_specs=pl.BlockSpec((TOP_K, tm, w), lambda i: (0, i, 0)),
        out_shape=jax.ShapeDtypeStruct((TOP_K, n, w), ys.dtype),
        scratch_shapes=[pltpu.SemaphoreType.DMA],
        compiler_params=_params(("arbitrary",)),
        name="moe_unpermute",
    )(pos_flat, ys)


def _sc_mesh():
    return plsc.VectorSubcoreMesh(core_axis_name="c", subcore_axis_name="s")


def _sc_worker():
    return lax.axis_index("s") * SC_CORES + lax.axis_index("c")


def _sc_scatter_rows(rows, idx, m_pad):
    n, w = rows.shape
    per_w = n // SC_WORKERS
    n_chunks = per_w // SC_CHUNK

    def body(rows_hbm, idx_hbm, out_hbm, idx_v, buf, g0, g1, o0, o1):
        gsem, osem = (g0, g1), (o0, o1)
        wid = _sc_worker()
        base = wid * per_w
        pltpu.sync_copy(idx_hbm.at[wid], idx_v)

        def get(c, slot):
            return pltpu.make_async_copy(rows_hbm.at[pl.ds(base + c * SC_CHUNK, SC_CHUNK)], buf.at[slot], gsem[slot])

        def put(c, k, slot):
            return pltpu.make_async_copy(buf.at[slot], out_hbm.at[idx_v.at[c * TOP_K + k]], osem[slot])

        get(0, 0).start()
        for c in range(n_chunks):
            slot = c % 2
            get(c, slot).wait()
            if c + 1 < n_chunks:
                if c >= 1:
                    for k in range(TOP_K):
                        put(c - 1, k, 1 - slot).wait()
                get(c + 1, 1 - slot).start()
            for k in range(TOP_K):
                put(c, k, slot).start()
        for c in range(max(n_chunks - 2, 0), n_chunks):
            for k in range(TOP_K):
                put(c, k, c % 2).wait()

    return pl.kernel(
        body, mesh=_sc_mesh(),
        out_type=jax.ShapeDtypeStruct((m_pad, w), rows.dtype),
        scratch_types=[pltpu.VMEM((n_chunks * TOP_K, SC_CHUNK), jnp.int32), pltpu.VMEM((2, SC_CHUNK, w), rows.dtype)]
                      + [pltpu.SemaphoreType.DMA] * 4,
        name="moe_sc_scatter",
    )(rows, idx)


def _sc_gather_rows(table, idx):
    b = idx.shape[0]
    w = table.shape[1]
    per_w = b // SC_WORKERS
    n_chunks = per_w // SC_CHUNK

    def body(table_hbm, idx_hbm, out_hbm, idx_v, buf, g0, g1, o0, o1):
        gsem, osem = (g0, g1), (o0, o1)
        base = _sc_worker() * per_w
        pltpu.sync_copy(idx_hbm.at[pl.ds(base, per_w)], idx_v)

        def get(c, slot):
            return pltpu.make_async_copy(table_hbm.at[idx_v.at[pl.ds(c * SC_CHUNK, SC_CHUNK)]], buf.at[slot],
                                         gsem[slot])

        def put(c, slot):
            return pltpu.make_async_copy(buf.at[slot], out_hbm.at[pl.ds(base + c * SC_CHUNK, SC_CHUNK)], osem[slot])

        get(0, 0).start()
        for c in range(n_chunks):
            slot = c % 2
            get(c, slot).wait()
            if c + 1 < n_chunks:
                if c >= 1:
                    put(c - 1, 1 - slot).wait()
                get(c + 1, 1 - slot).start()
            put(c, slot).start()
        for c in range(max(n_chunks - 2, 0), n_chunks):
            put(c, c % 2).wait()

    return pl.kernel(
        body, mesh=_sc_mesh(),
        out_type=jax.ShapeDtypeStruct((b, w), table.dtype),
        scratch_types=[pltpu.VMEM((per_w,), jnp.int32), pltpu.VMEM((2, SC_CHUNK, w), table.dtype)]
                      + [pltpu.SemaphoreType.DMA] * 4,
        name="moe_sc_gather",
    )(table, idx)


def _expert_kernel(te_ref, nu_ref, nv_ref, xs_ref, wgu_ref, bgu_ref, wdn_ref, bdn_ref, ys_ref, wgu_bf, wdn_bf, *, d):
    j = pl.program_id(0)
    e = te_ref[j]
    prev = te_ref[jnp.maximum(j - 1, 0)]

    @pl.when(jnp.logical_or(j == 0, e != prev))
    def _():
        rows = 128

        def cast(c, carry):
            r0 = pl.multiple_of(c * rows, rows)
            wgu_bf[pl.ds(r0, rows), :] = wgu_ref[pl.ds(r0, rows), :].astype(BF16)
            wdn_bf[pl.ds(r0, rows), :] = wdn_ref[pl.ds(r0, rows), :].astype(BF16)
            return carry

        lax.fori_loop(0, d // rows, cast, 0)

    @pl.when(j < nu_ref[0])
    def _():
        rowi = lax.broadcasted_iota(jnp.int32, xs_ref.shape, 0)
        xw = jnp.where(rowi < nv_ref[j], xs_ref[...], jnp.int32(0))
        gu = jnp.dot(_unpack_bf16(xw), wgu_bf[...], preferred_element_type=F32) + bgu_ref[...]
        g = jnp.minimum(gu[:, :d], SWIGLU_LIMIT)
        u = jnp.clip(gu[:, d:], -SWIGLU_LIMIT, SWIGLU_LIMIT)
        act = (u + 1.0) * (g * _sigmoid(SWIGLU_ALPHA * g))
        y = jnp.dot(act.astype(BF16), wdn_bf[...], preferred_element_type=F32) + bdn_ref[...]
        ys_ref[...] = _pack_pairs(y)

    @pl.when(j >= nu_ref[0])
    def _():
        ys_ref[...] = jnp.zeros_like(ys_ref)


def _experts(xs, tile_expert, n_used, n_valid, w_gu, b_gu, w_dn, b_dn, layer, tm=EXPERT_TM):
    m_pad, half = xs.shape
    d = 2 * half
    nl, ne = w_gu.shape[:2]
    rows = lambda j, te, nu, nv: (jnp.minimum(j, nu[0] - 1), 0)
    wsel = lambda j, te, nu, nv: (layer, te[j], 0, 0)
    grid_spec = pltpu.PrefetchScalarGridSpec(
        num_scalar_prefetch=3,
        grid=(m_pad // tm,),
        in_specs=[pl.BlockSpec((tm, half), rows),
                  pl.BlockSpec((None, None, d, 2 * d), wsel),
                  pl.BlockSpec((None, None, 1, 2 * d), wsel),
                  pl.BlockSpec((None, None, d, d), wsel),
                  pl.BlockSpec((None, None, 1, d), wsel)],
        out_specs=pl.BlockSpec((tm, half), lambda j, te, nu, nv: (j, 0)),
        scratch_shapes=[pltpu.VMEM((d, 2 * d), BF16), pltpu.VMEM((d, d), BF16)],
    )
    return pl.pallas_call(
        functools.partial(_expert_kernel, d=d),
        grid_spec=grid_spec,
        out_shape=jax.ShapeDtypeStruct((m_pad, half), I32),
        compiler_params=_params(("arbitrary",)),
        name="moe_experts",
    )(tile_expert, n_used, n_valid, xs, w_gu, b_gu.reshape(nl, ne, 1, 2 * d), w_dn, b_dn.reshape(nl, ne, 1, d))


def _combine_kernel(x_ref, y_ref, tw_ref, mod_ref, lng_ref, lnb_ref, o_ref, *, d, alpha):
    tw = tw_ref[...]
    m_hi = None
    m_lo = None
    for k in range(TOP_K):
        hi, lo = _unpack_pairs(y_ref[k])
        wk = tw[:, k:k + 1]
        m_hi = wk * hi if m_hi is None else m_hi + wk * hi
        m_lo = wk * lo if m_lo is None else m_lo + wk * lo
    m = jnp.concatenate([m_hi, m_lo], axis=1)
    gate = mod_ref[...][:, 2 * d:]
    o_ref[...] = _layer_norm(alpha * x_ref[...] + (1.0 + gate) * m, lng_ref[...], lnb_ref[...])


def _combine(x, y4, tw, mod, ln_g, ln_b, tm, alpha):
    bsz, s, d = x.shape
    nt = s // tm
    tok = pl.BlockSpec((None, tm, d), lambda b, t: (b, t, 0))
    return pl.pallas_call(
        functools.partial(_combine_kernel, d=d, alpha=alpha),
        grid=(bsz, nt),
        in_specs=[tok,
                  pl.BlockSpec((TOP_K, tm, d // 2), lambda b, t: (0, b * nt + t, 0)),
                  pl.BlockSpec((tm, LANES), lambda b, t: (b * nt + t, 0)),
                  pl.BlockSpec((None, 1, 3 * d), lambda b, t: (b, 0, 0)),
                  _full((1, d)), _full((1, d))],
        out_specs=tok,
        out_shape=jax.ShapeDtypeStruct((bsz, s, d), F32),
        compiler_params=_params(("arbitrary", "arbitrary")),
        name="moe_combine",
    )(x, y4, tw, mod, ln_g, ln_b)


def _moe(hm_p, hm_s, w_r, b_r, w_gu, b_gu, w_dn, b_dn, layer):
    half = hm_p.shape[-1]
    hm_p = hm_p.reshape(-1, half)
    hm_s = hm_s.reshape(-1, half)
    n_p, n_s = hm_p.shape[0], hm_s.shape[0]
    ne = w_r.shape[1]
    w_r = w_r.astype(BF16)
    b_r = b_r.reshape(1, ne)
    eid_p, tw_p, rank_p, cnt_p = _router(hm_p, w_r, b_r, jnp.zeros((1, ne), F32), PROMPT_TM)
    eid_s, tw_s, rank_s, cnt = _router(hm_s, w_r, b_r, cnt_p, n_s)
    tm = EXPERT_TM
    cnt = cnt.reshape(ne).astype(jnp.int32)
    gsz = ((cnt + tm - 1) // tm) * tm
    ends = jnp.cumsum(gsz)
    offs = ends - gsz
    pos_p = (offs[eid_p[:, :TOP_K]] + rank_p[:, :TOP_K]).reshape(-1)
    pos_s = (offs[eid_s[:, :TOP_K]] + rank_s[:, :TOP_K]).reshape(-1)
    n_tiles = ((n_p + n_s) * TOP_K + ne * (tm - 1)) // tm + 1
    m_pad = n_tiles * tm
    tile_start = jnp.arange(n_tiles, dtype=jnp.int32) * tm
    tile_expert = jnp.minimum(jnp.sum((tile_start[:, None] >= ends[None, :]).astype(jnp.int32), axis=1), ne - 1)
    n_used = (ends[-1] // tm).astype(jnp.int32).reshape(1)
    n_valid = jnp.clip((offs + cnt)[tile_expert] - tile_start, 0, tm).astype(jnp.int32)
    pos2 = pos_p.reshape(n_p, TOP_K)
    idx_scatter = pos2.reshape(SC_WORKERS, -1, SC_CHUNK, TOP_K).transpose(0, 1, 3, 2).reshape(SC_WORKERS, -1, SC_CHUNK)
    xs = _sc_scatter_rows(hm_p, idx_scatter, m_pad)
    xs = _permute(hm_s, pos_s, xs, n_s)
    ys = _experts(xs, tile_expert, n_used, n_valid, w_gu, b_gu, w_dn, b_dn, layer)
    y4_p = _sc_gather_rows(ys, pos2.T.reshape(-1)).reshape(TOP_K, n_p, half)
    return (y4_p, tw_p), (_unpermute(ys, pos_s, n_s, n_s), tw_s)


def _qkv_kernel(x_ref, modk_ref, modq_ref, cos_ref, sin_ref, wdkv_ref, gkv_ref, wdq_ref, gq_ref, wuq_ref,
                ckv_ref, kpe_ref, kc_ref, kp_ref, qh_ref, *, d, kvl, rope, nh, nope, q_scale):
    x = x_ref[...]
    cos = cos_ref[...]
    sin = sin_ref[...]
    modk = modk_ref[...]
    hk = (x * (1.0 + modk[:, d:2 * d]) + modk[:, :d]).astype(BF16)
    kv = jnp.dot(hk, wdkv_ref[...], preferred_element_type=F32)
    c = kv[:, :kvl]
    ckv = c * lax.rsqrt(jnp.mean(c * c, axis=-1, keepdims=True) + RMS_EPS) * gkv_ref[...]
    kp = kv[:, kvl:kvl + LANES] * cos + kv[:, kvl + LANES:kvl + 2 * LANES] * sin
    ckv_ref[...] = ckv
    kpe_ref[...] = kp[:, :rope]
    kc_ref[...] = ckv.astype(BF16)
    kp_ref[...] = kp[:, :rope].astype(BF16)

    modq = modq_ref[...]
    hq = (x * (1.0 + modq[:, d:2 * d]) + modq[:, :d]).astype(BF16)
    qd = jnp.dot(hq, wdq_ref[...], preferred_element_type=F32)
    qn = (qd * lax.rsqrt(jnp.mean(qd * qd, axis=-1, keepdims=True) + RMS_EPS) * gq_ref[...]).astype(BF16)
    q = jnp.dot(qn, wuq_ref[...], preferred_element_type=F32)
    for h in range(nh):
        qh_ref[h, :, :nope] = (q[:, h * nope:(h + 1) * nope] * q_scale).astype(BF16)
    pe0 = nh * nope
    sw0 = pe0 + nh * rope
    per = LANES // rope
    for j in range(nh // per):
        r2 = (q[:, pe0 + j * LANES:pe0 + (j + 1) * LANES] * cos
              + q[:, sw0 + j * LANES:sw0 + (j + 1) * LANES] * sin) * q_scale
        for i in range(per):
            qh_ref[j * per + i, :, nope:] = r2[:, i * rope:(i + 1) * rope].astype(BF16)


def _qkv(x, modk, modq, cos_t, sin_t, w, tm, dims):
    bsz, s, d = x.shape
    kvl, rope, nh, nope = dims["kvl"], dims["rope"], dims["nh"], dims["nope"]
    tok = lambda width: pl.BlockSpec((None, tm, width), lambda b, t: (b, t, 0))
    head = lambda width: pl.BlockSpec((None, nh, tm, width), lambda b, t: (b, 0, t, 0))
    tab = pl.BlockSpec((tm, LANES), lambda b, t: (t, 0))
    weights = [w["w_dkv"], w["g_kv"], w["w_dq"], w["g_q"], w["w_uq"]]
    kern = functools.partial(_qkv_kernel, d=d, kvl=kvl, rope=rope, nh=nh, nope=nope, q_scale=dims["q_scale"])
    return pl.pallas_call(
        kern,
        grid=(bsz, s // tm),
        in_specs=[tok(d),
                  pl.BlockSpec((None, 1, 2 * d), lambda b, t: (b, 0, 0)),
                  pl.BlockSpec((None, 1, 3 * d), lambda b, t: (b, 0, 0)),
                  tab, tab] + [_full(a.shape) for a in weights],
        out_specs=[tok(kvl), tok(rope), tok(kvl), tok(rope), head(nope + rope)],
        out_shape=[jax.ShapeDtypeStruct((bsz, s, kvl), F32), jax.ShapeDtypeStruct((bsz, s, rope), F32),
                   jax.ShapeDtypeStruct((bsz, s, kvl), BF16), jax.ShapeDtypeStruct((bsz, s, rope), BF16),
                   jax.ShapeDtypeStruct((bsz, nh, s, nope + rope), BF16)],
        compiler_params=_params(("arbitrary", "arbitrary")),
        name="mla_qkv",
    )(x, modk, modq, cos_t, sin_t, *weights)


def _kproj_kernel(kc_ref, kp_ref, wuk_ref, kh_ref, *, nh, nope):
    kc = kc_ref[...]
    kp = kp_ref[...]
    nt = (((1,), (1,)), ((), ()))
    for h in range(nh):
        kh_ref[h, :, :nope] = lax.dot_general(kc, wuk_ref[h], nt, preferred_element_type=F32).astype(BF16)
        kh_ref[h, :, nope:] = kp


def _kproj(kc, kp, w_uk, tm):
    bsz, s, kvl = kc.shape
    rope = kp.shape[-1]
    nh, nope, _ = w_uk.shape
    tok = lambda width: pl.BlockSpec((None, tm, width), lambda b, t: (b, t, 0))
    return pl.pallas_call(
        functools.partial(_kproj_kernel, nh=nh, nope=nope),
        grid=(bsz, s // tm),
        in_specs=[tok(kvl), tok(rope), _full(w_uk.shape)],
        out_specs=pl.BlockSpec((None, nh, tm, nope + rope), lambda b, t: (b, 0, t, 0)),
        out_shape=jax.ShapeDtypeStruct((bsz, nh, s, nope + rope), BF16),
        compiler_params=_params(("arbitrary", "arbitrary")),
        name="mla_kproj",
    )(kc, kp, w_uk)


def _attn_kernel(qi_ref, kj_ref, last_ref, edge_ref, qh_ref, kh_ref, kc_ref, o_ref, m_s, l_s, acc_s,
                 *, nh, tq, tk, kvl, causal, valid):
    s_id = pl.program_id(1)
    qi = qi_ref[s_id]
    kj = kj_ref[s_id]
    nt = (((1,), (1,)), ((), ()))

    @pl.when(kj == 0)
    def _():
        m_s[...] = jnp.full_like(m_s, -jnp.inf)
        l_s[...] = jnp.zeros_like(l_s)
        acc_s[...] = jnp.zeros_like(acc_s)

    def sweep(masked):
        kc = kc_ref[...]
        if masked:
            kidx = kj * tk + lax.broadcasted_iota(jnp.int32, (tq, tk), 1)
            if causal:
                qidx = qi * tq + lax.broadcasted_iota(jnp.int32, (tq, tk), 0)
                visible = (kidx // CHUNK) <= (qidx // CHUNK)
            else:
                visible = kidx < valid
        for h in range(nh):
            s = lax.dot_general(qh_ref[h], kh_ref[h], nt, preferred_element_type=F32)
            if masked:
                s = jnp.where(visible, s, -jnp.inf)
            slabs = [s[:, j * LANES:(j + 1) * LANES] for j in range(tk // LANES)]
            mx = slabs[0]
            for sj in slabs[1:]:
                mx = jnp.maximum(mx, sj)
            m_prev = m_s[h]
            m_new = jnp.maximum(m_prev, jnp.max(mx, axis=1, keepdims=True))
            scale = jnp.exp2(m_prev - m_new)
            ps = []
            rsum = None
            for sj in slabs:
                pj = jnp.exp2(sj - m_new)
                ps.append(pj.astype(BF16))
                rsum = pj if rsum is None else rsum + pj
            p = jnp.concatenate(ps, axis=1)
            l_s[h] = scale * l_s[h] + jnp.sum(rsum, axis=1, keepdims=True)
            pv = jnp.dot(p, kc, preferred_element_type=F32)
            acc_s[h] = acc_s[h] * jnp.concatenate([scale] * (kvl // LANES), axis=1) + pv
            m_s[h] = m_new

    edge = edge_ref[s_id]

    @pl.when(edge == 1)
    def _():
        sweep(True)

    @pl.when(edge == 0)
    def _():
        sweep(False)

    @pl.when(last_ref[s_id] == 1)
    def _():
        for h in range(nh):
            inv = jnp.concatenate([l_s[h]] * (kvl // LANES), axis=1)
            o_ref[h] = (acc_s[h] / inv).astype(BF16)


def _attention(qh, kh, kc, tq, tk, causal, valid):
    bsz, nh, s, qk = qh.shape
    skv, kvl = kc.shape[1], kc.shape[2]
    qi, kj, last, edge = [], [], [], []
    for i in range(s // tq):
        hi = ((i * tq + tq - 1) // tk) if causal else (skv // tk - 1)
        for j in range(hi + 1):
            qi.append(i)
            kj.append(j)
            last.append(1 if j == hi else 0)
            if causal:
                edge.append(1 if ((j + 1) * tk - 1) // CHUNK > (i * tq) // CHUNK else 0)
            else:
                edge.append(1 if (j + 1) * tk > valid else 0)
    tabs = [jnp.asarray(np.array(a, np.int32)) for a in (qi, kj, last, edge)]
    qspec = lambda width: pl.BlockSpec((None, nh, tq, width), lambda b, t, qi, kj, last, edge: (b, 0, qi[t], 0))
    khspec = pl.BlockSpec((None, nh, tk, qk), lambda b, t, qi, kj, last, edge: (b, 0, kj[t], 0))
    kcspec = pl.BlockSpec((None, tk, kvl), lambda b, t, qi, kj, last, edge: (b, kj[t], 0))
    grid_spec = pltpu.PrefetchScalarGridSpec(
        num_scalar_prefetch=4,
        grid=(bsz, len(qi)),
        in_specs=[qspec(qk), khspec, kcspec],
        out_specs=qspec(kvl),
        scratch_shapes=[pltpu.VMEM((nh, tq, LANES), F32), pltpu.VMEM((nh, tq, LANES), F32),
                        pltpu.VMEM((nh, tq, kvl), F32)],
    )
    kern = functools.partial(_attn_kernel, nh=nh, tq=tq, tk=tk, kvl=kvl, causal=causal, valid=valid)
    return pl.pallas_call(
        kern,
        grid_spec=grid_spec,
        out_shape=jax.ShapeDtypeStruct((bsz, nh, s, kvl), BF16),
        compiler_params=_params(("arbitrary", "arbitrary")),
        name="mla_attention",
    )(*tabs, qh, kh, kc)


def _attn_out_kernel(o_ref, x_ref, mod_ref, mod2_ref, wuv_ref, wo_ref, lng_ref, lnb_ref, x3_ref, hm_ref,
                     *, d, nh, alpha):
    parts = [jnp.dot(o_ref[h], wuv_ref[h], preferred_element_type=F32) for h in range(nh)]
    o = jnp.concatenate(parts, axis=1).astype(BF16)
    m = jnp.dot(o, wo_ref[...], preferred_element_type=F32)
    gate = mod_ref[...][:, 2 * d:]
    x3 = _layer_norm(alpha * x_ref[...] + (1.0 + gate) * m, lng_ref[...], lnb_ref[...])
    x3_ref[...] = x3
    mod2 = mod2_ref[...]
    hm_ref[...] = _pack_pairs(x3 * (1.0 + mod2[:, d:2 * d]) + mod2[:, :d])


def _attn_out(o_lat, x, mod, mod2, w_uv, w_o, ln_g, ln_b, tm, alpha):
    bsz, s, d = x.shape
    nh, kvl = o_lat.shape[1], o_lat.shape[3]
    tok = lambda width: pl.BlockSpec((None, tm, width), lambda b, t: (b, t, 0))
    modspec = pl.BlockSpec((None, 1, 3 * d), lambda b, t: (b, 0, 0))
    return pl.pallas_call(
        functools.partial(_attn_out_kernel, d=d, nh=nh, alpha=alpha),
        grid=(bsz, s // tm),
        in_specs=[pl.BlockSpec((None, nh, tm, kvl), lambda b, t: (b, 0, t, 0)), tok(d), modspec, modspec,
                  _full(w_uv.shape), _full(w_o.shape), _full((1, d)), _full((1, d))],
        out_specs=[tok(d), tok(d // 2)],
        out_shape=[jax.ShapeDtypeStruct((bsz, s, d), F32), jax.ShapeDtypeStruct((bsz, s, d // 2), I32)],
        compiler_params=_params(("arbitrary", "arbitrary")),
        name="mla_out",
    )(o_lat, x, mod, mod2, w_uv, w_o, ln_g, ln_b)


def _rope_tables(pos, rope):
    half = rope // 2
    inv = ROPE_THETA ** (-2.0 * jnp.arange(half, dtype=F32) / rope)
    ang = pos.astype(F32)[:, None] * inv[None, :]
    cos, sin = jnp.cos(ang), jnp.sin(ang)
    rep = LANES // rope
    return (jnp.concatenate([cos, cos] * rep, axis=1), jnp.concatenate([-sin, sin] * rep, axis=1))


def kernel(x_prompt, x_sample, state_conv, state_rglru, cache_ckv, cache_kpe, c_prompt, c_sample, w_ada, b_ada, ln_g, ln_b, lru_w_in, lru_b_in, lru_conv_w, lru_conv_b, lru_w_a, lru_b_a, lru_w_i, lru_b_i, lru_lambda, lru_w_out, lru_b_out, kv_w_ada, kv_b_ada, mla_w_dkv, mla_g_kv, mla_w_uk, mla_w_uv, mla_w_dq, mla_g_q, mla_w_uq, mla_w_o, moe_w_r, moe_b_r, moe_w_gu, moe_b_gu, moe_w_dn, moe_b_dn):
    bp, sp, d = x_prompt.shape
    bs, ss, _ = x_sample.shape
    depth = w_ada.shape[0]
    alpha = float((2.0 * depth) ** 0.25)
    nh, nope, kvl = mla_w_uk.shape
    rope = cache_kpe.shape[-1]
    past = cache_ckv.shape[1]
    hist = state_conv.shape[2]
    dims = dict(kvl=kvl, rope=rope, nh=nh, nope=nope, q_scale=float((nope + rope) ** -0.5 * LOG2E))
    row = lambda v: v.reshape(1, -1)

    nrow = bp + bs
    rpad = -nrow % (2 * SUBLANES)
    c_rows = jnp.concatenate([c_prompt, c_sample, jnp.zeros((rpad, d), F32)], axis=0)
    mods = _ada(c_rows, w_ada.reshape(depth * 2, d, 3 * d), b_ada.reshape(depth * 2, 1, 3 * d))
    modkv = _ada(c_rows, kv_w_ada.reshape(1, d, 2 * d), kv_b_ada.reshape(1, 1, 2 * d))[0]
    mod_p = lambda i: mods[i, :bp].reshape(bp, 1, 3 * d)
    mod_s = lambda i: mods[i, bp:nrow].reshape(bs, 1, 3 * d)

    l0w = dict(w_in=lru_w_in[0].astype(BF16), b_in=row(lru_b_in[0]), conv_w=lru_conv_w[0], conv_b=row(lru_conv_b[0]),
               w_a=lru_w_a[0].astype(BF16), b_a=row(lru_b_a[0]), w_i=lru_w_i[0].astype(BF16), b_i=row(lru_b_i[0]),
               lam=row(lru_lambda[0]), w_out=lru_w_out[0].astype(BF16), b_out=row(lru_b_out[0]),
               ln_g=row(ln_g[0, 0]), ln_b=row(ln_b[0, 0]))
    x1p, hmp, conv_p, h_p = _l0_mixer(x_prompt, mod_p(0), mod_p(1), jnp.zeros((bp, hist, d), F32),
                                      jnp.zeros((bp, 1, d), F32), l0w, L0_TB, alpha)
    x1s, hms, conv_s, h_s = _l0_mixer(x_sample, mod_s(0), mod_s(1), state_conv[0], state_rglru[0].reshape(bs, 1, d),
                                      l0w, ss, alpha)

    (y4p, twp), (y4s, tws) = _moe(hmp, hms, moe_w_r[0], moe_b_r[0], moe_w_gu, moe_b_gu, moe_w_dn, moe_b_dn, 0)
    x2p = _combine(x1p, y4p, twp, mod_p(1), row(ln_g[0, 1]), row(ln_b[0, 1]), PROMPT_TM, alpha)
    x2s = _combine(x1s, y4s, tws, mod_s(1), row(ln_g[0, 1]), row(ln_b[0, 1]), ss, alpha)

    pe_cols = jnp.arange(rope)
    sw_cols = jnp.concatenate([pe_cols[rope // 2:], pe_cols[:rope // 2]])
    zpad = jnp.zeros((d, LANES - rope), F32)
    w_dkv_ext = jnp.concatenate([mla_w_dkv[:, :kvl], mla_w_dkv[:, kvl:], zpad,
                                 mla_w_dkv[:, kvl:][:, sw_cols], zpad], axis=1).astype(BF16)
    wq = mla_w_uq[0].reshape(-1, nh, nope + rope)
    w_uq_ext = jnp.concatenate([wq[:, :, :nope].reshape(-1, nh * nope), wq[:, :, nope:].reshape(-1, nh * rope),
                                wq[:, :, nope:][:, :, sw_cols].reshape(-1, nh * rope)], axis=1).astype(BF16)
    qw = dict(w_dkv=w_dkv_ext, g_kv=row(mla_g_kv), w_dq=mla_w_dq[0].astype(BF16), g_q=row(mla_g_q[0]),
              w_uq=w_uq_ext)
    w_uk = mla_w_uk.astype(BF16)
    cos_p, sin_p = _rope_tables(jnp.arange(sp), rope)
    cos_s, sin_s = _rope_tables(past + jnp.arange(ss), rope)
    modkv_p = modkv[:bp].reshape(bp, 1, 2 * d)
    modkv_s = modkv[bp:nrow].reshape(bs, 1, 2 * d)
    ckv_p, kpe_p, kc_p, kp_p, qh_p = _qkv(x2p, modkv_p, mod_p(2), cos_p, sin_p, qw, PROMPT_TM, dims)
    ckv_s, kpe_s, kc_s, kp_s, qh_s = _qkv(x2s, modkv_s, mod_s(2), cos_s, sin_s, qw, ss, dims)

    o_p = _attention(qh_p, _kproj(kc_p, kp_p, w_uk, PROMPT_TM), kc_p, ATT_TQ, ATT_TK, True, sp)
    skv = past + ss
    kpad = -skv % LANES
    kc_all = jnp.concatenate([cache_ckv.astype(BF16), kc_s, jnp.zeros((bs, kpad, kvl), BF16)], axis=1)
    kp_all = jnp.concatenate([cache_kpe.astype(BF16), kp_s, jnp.zeros((bs, kpad, rope), BF16)], axis=1)
    o_s = _attention(qh_s, _kproj(kc_all, kp_all, w_uk, skv + kpad), kc_all, ss, skv + kpad, False, skv)
    w_uv = mla_w_uv.astype(BF16)
    w_o = mla_w_o[0].astype(BF16)
    x3p, hm3p = _attn_out(o_p, x2p, mod_p(2), mod_p(3), w_uv, w_o, row(ln_g[1, 0]), row(ln_b[1, 0]), PROMPT_TM, alpha)
    x3s, hm3s = _attn_out(o_s, x2s, mod_s(2), mod_s(3), w_uv, w_o, row(ln_g[1, 0]), row(ln_b[1, 0]), ss, alpha)

    (y4p, twp), (y4s, tws) = _moe(hm3p, hm3s, moe_w_r[1], moe_b_r[1], moe_w_gu, moe_b_gu, moe_w_dn, moe_b_dn, 1)
    y_p = _combine(x3p, y4p, twp, mod_p(3), row(ln_g[1, 1]), row(ln_b[1, 1]), PROMPT_TM, alpha)
    y_s = _combine(x3s, y4s, tws, mod_s(3), row(ln_g[1, 1]), row(ln_b[1, 1]), ss, alpha)

    return (y_p, y_s, conv_p[None], h_p.reshape(1, bp, d), ckv_p, kpe_p,
            conv_s[None], h_s.reshape(1, bs, d), ckv_s, kpe_s)
```

```python
import functools

import numpy as np
import jax
import jax.numpy as jnp
from jax import lax
from jax.experimental import pallas as pl
from jax.experimental.pallas import tpu as pltpu
from jax.experimental.pallas import tpu_sc as plsc

F32 = jnp.float32
BF16 = jnp.bfloat16
I32 = jnp.int32

CHUNK = 64
N_LRU_BLOCKS = 8
LRU_C = 8.0
ROPE_THETA = 10000.0
TOP_K = 4
SWIGLU_ALPHA = 1.702
SWIGLU_LIMIT = 7.0
LN_EPS = 1e-5
RMS_EPS = 1e-6
LOG2E = 1.4426950408889634

LANES = 128
SUBLANES = 8
VMEM_LIMIT = 56 * 1024 * 1024

PROMPT_TM = 512
L0_TB = 256
EXPERT_TM = 512
SC_CORES = 2
SC_WORKERS = SC_CORES * 16
SC_CHUNK = 64
ATT_TQ = 512
ATT_TK = 1024


def _sigmoid(x):
    return 0.5 * (jnp.tanh(0.5 * x) + 1.0)


def _layer_norm(v, g, b):
    mu = jnp.mean(v, axis=-1, keepdims=True)
    d = v - mu
    var = jnp.mean(d * d, axis=-1, keepdims=True)
    return d * lax.rsqrt(var + LN_EPS) * g + b


def _pack_pairs(x):
    w = x.shape[1] // 2
    hi = lax.bitcast_convert_type(x[:, :w].astype(BF16).astype(F32), I32)
    lo = lax.bitcast_convert_type(x[:, w:].astype(BF16).astype(F32), I32)
    return hi | lax.shift_right_logical(lo, jnp.int32(16))


def _unpack_pairs(words):
    hi = lax.bitcast_convert_type(words & jnp.int32(-65536), F32)
    lo = lax.bitcast_convert_type(words << 16, F32)
    return hi, lo


def _unpack_bf16(words):
    hi, lo = _unpack_pairs(words)
    return jnp.concatenate([hi.astype(BF16), lo.astype(BF16)], axis=1)


def _full(shape):
    nd = len(shape)
    return pl.BlockSpec(shape, lambda *_: (0,) * nd)


def _params(sem, vmem=VMEM_LIMIT):
    return pltpu.CompilerParams(dimension_semantics=sem, vmem_limit_bytes=vmem)


def _ada_kernel(c_ref, w_ref, b_ref, o_ref):
    c = c_ref[...]
    s = (c * _sigmoid(c)).astype(BF16)
    o_ref[...] = jnp.dot(s, w_ref[...].astype(BF16), preferred_element_type=F32) + b_ref[...]


def _ada(c_rows, w, b, tn=1024):
    g, d, n = w.shape
    r = c_rows.shape[0]
    return pl.pallas_call(
        _ada_kernel,
        grid=(g, n // tn),
        in_specs=[_full((r, d)),
                  pl.BlockSpec((None, d, tn), lambda i, j: (i, 0, j)),
                  pl.BlockSpec((None, 1, tn), lambda i, j: (i, 0, j))],
        out_specs=pl.BlockSpec((None, r, tn), lambda i, j: (i, 0, j)),
        out_shape=jax.ShapeDtypeStruct((g, r, n), F32),
        compiler_params=_params(("arbitrary", "arbitrary")),
        name="ada_mod",
    )(c_rows, w, b)


def _l0_kernel(x_ref, mod_ref, mod2_ref, conv0_ref, h0_ref, win_ref, bin_ref, cw_ref, cb_ref,
               wa_ref, ba_ref, wi_ref, bi_ref, lam_ref, wout_ref, bout_ref, lng_ref, lnb_ref,
               x1_ref, hm_ref, convn_ref, hlast_ref,
               ubuf, a_s, b_s, h_s, *, tb, d, alpha):
    t = pl.program_id(1)
    pad = SUBLANES
    hist = convn_ref.shape[0]

    @pl.when(t == 0)
    def _():
        ubuf[pad - hist:pad, :] = conv0_ref[...]
        h_s[...] = h0_ref[...]

    x = x_ref[...]
    mod = mod_ref[...]
    shift, scale, gate = mod[:, :d], mod[:, d:2 * d], mod[:, 2 * d:]
    h = (x * (1.0 + scale) + shift).astype(BF16)
    proj = jnp.dot(h, win_ref[...], preferred_element_type=F32) + bin_ref[...]
    gate_b = proj[:, :d]
    ubuf[pad:pad + tb, :] = proj[:, d:]

    cw = cw_ref[...]
    u = cb_ref[...]
    for k in range(hist + 1):
        u = u + ubuf[pad - hist + k:pad - hist + k + tb, :] * cw[k:k + 1, :]
    ubuf[pad - hist:pad, :] = ubuf[pad + tb - hist:pad + tb, :]

    ub = u.astype(BF16)
    blk = d // N_LRU_BLOCKS
    lam = lam_ref[...]
    neg = -lam
    softplus = jnp.maximum(neg, 0.0) + jnp.log1p(jnp.exp(-jnp.abs(neg)))
    for n in range(N_LRU_BLOCKS):
        sl = slice(n * blk, (n + 1) * blk)
        un = ub[:, sl]
        r = _sigmoid(jnp.dot(un, wa_ref[n], preferred_element_type=F32) + ba_ref[:, sl])
        ig = _sigmoid(jnp.dot(un, wi_ref[n], preferred_element_type=F32) + bi_ref[:, sl])
        log_a = -LRU_C * r * softplus[:, sl]
        a = jnp.exp(log_a)
        one_m_a2 = -jnp.tanh(log_a) * (a * a + 1.0)
        a_s[:, sl] = a
        b_s[:, sl] = jnp.sqrt(one_m_a2) * (ig * u[:, sl])

    row = lax.broadcasted_iota(jnp.int32, (SUBLANES, d), 0)

    def group(g, hprev):
        r0 = pl.multiple_of(g * SUBLANES, SUBLANES)
        a = a_s[pl.ds(r0, SUBLANES), :]
        b = b_s[pl.ds(r0, SUBLANES), :]
        for sh in (1, 2, 4):
            keep = row >= sh
            a_sh = pltpu.roll(a, sh, axis=0)
            b_sh = pltpu.roll(b, sh, axis=0)
            b = jnp.where(keep, a * b_sh + b, b)
            a = jnp.where(keep, a * a_sh, a)
        hh = a * hprev + b
        b_s[pl.ds(r0, SUBLANES), :] = hh
        return hh[SUBLANES - 1:SUBLANES, :]

    h_s[...] = lax.fori_loop(0, tb // SUBLANES, group, h_s[...])

    y = b_s[...]
    gl = 0.5 * gate_b * (1.0 + jnp.tanh(0.7978845608028654 * (gate_b + 0.044715 * gate_b * gate_b * gate_b)))
    out = jnp.dot((gl * y).astype(BF16), wout_ref[...], preferred_element_type=F32) + bout_ref[...]
    x1 = _layer_norm(alpha * x + (1.0 + gate) * out, lng_ref[...], lnb_ref[...])
    x1_ref[...] = x1
    mod2 = mod2_ref[...]
    hm_ref[...] = _pack_pairs(x1 * (1.0 + mod2[:, d:2 * d]) + mod2[:, :d])

    @pl.when(t == pl.num_programs(1) - 1)
    def _():
        convn_ref[...] = ubuf[pad - hist:pad, :]
        hlast_ref[...] = h_s[...]


def _l0_mixer(x, mod, mod2, conv0, h0, w, tb, alpha):
    bsz, s, d = x.shape
    hist = conv0.shape[1]
    tok = lambda width: pl.BlockSpec((None, tb, width), lambda b, t: (b, t, 0))
    per_b = lambda rows, width: pl.BlockSpec((None, rows, width), lambda b, t: (b, 0, 0))
    kern = functools.partial(_l0_kernel, tb=tb, d=d, alpha=alpha)
    weights = [w["w_in"], w["b_in"], w["conv_w"], w["conv_b"], w["w_a"], w["b_a"], w["w_i"], w["b_i"],
               w["lam"], w["w_out"], w["b_out"], w["ln_g"], w["ln_b"]]
    return pl.pallas_call(
        kern,
        grid=(bsz, s // tb),
        in_specs=[tok(d), per_b(1, 3 * d), per_b(1, 3 * d), per_b(hist, d), per_b(1, d)]
                 + [_full(a.shape) for a in weights],
        out_specs=[tok(d), tok(d // 2), per_b(hist, d), per_b(1, d)],
        out_shape=[jax.ShapeDtypeStruct((bsz, s, d), F32), jax.ShapeDtypeStruct((bsz, s, d // 2), I32),
                   jax.ShapeDtypeStruct((bsz, hist, d), F32), jax.ShapeDtypeStruct((bsz, 1, d), F32)],
        scratch_shapes=[pltpu.VMEM((tb + SUBLANES, d), F32), pltpu.VMEM((tb, d), F32),
                        pltpu.VMEM((tb, d), F32), pltpu.VMEM((1, d), F32)],
        compiler_params=_params(("arbitrary", "arbitrary")),
        name="l0_mixer",
    )(x, mod, mod2, conv0, h0, *weights)


def _router_kernel(hm_ref, wr_ref, br_ref, base0_ref, eid_ref, tw_ref, rank_ref, cnt_ref, base, *, tm, ne):
    i = pl.program_id(0)

    @pl.when(i == 0)
    def _():
        base[...] = base0_ref[...]

    logits = jnp.dot(_unpack_bf16(hm_ref[...]), wr_ref[...], preferred_element_type=F32) + br_ref[...]
    col = lax.broadcasted_iota(jnp.int32, (tm, ne), 1).astype(F32)
    l = logits
    vals, idxs = [], []
    for _ in range(TOP_K):
        m = jnp.max(l, axis=-1, keepdims=True)
        idx = jnp.min(jnp.where(l == m, col, float(ne)), axis=-1, keepdims=True)
        vals.append(m)
        idxs.append(idx)
        l = jnp.where(col == idx, -jnp.inf, l)
    es = [jnp.exp(v - vals[0]) for v in vals]
    tot = es[0] + es[1] + es[2] + es[3]

    ri = lax.broadcasted_iota(jnp.int32, (tm, tm), 0)
    ci = lax.broadcasted_iota(jnp.int32, (tm, tm), 1)
    tri = (ci < ri).astype(BF16)
    run = base[...]
    lane = lax.broadcasted_iota(jnp.int32, (tm, LANES), 1)
    eid_o = jnp.zeros((tm, LANES), F32)
    tw_o = jnp.zeros((tm, LANES), F32)
    rk_o = jnp.zeros((tm, LANES), F32)
    for k in range(TOP_K):
        oh = col == idxs[k]
        ohf = oh.astype(F32)
        before = jnp.dot(tri, ohf.astype(BF16), preferred_element_type=F32)
        rank = jnp.sum(jnp.where(oh, before + run, 0.0), axis=-1, keepdims=True)
        run = run + jnp.sum(ohf, axis=0, keepdims=True)
        sel = lane == k
        eid_o = jnp.where(sel, idxs[k], eid_o)
        tw_o = jnp.where(sel, es[k] / tot, tw_o)
        rk_o = jnp.where(sel, rank, rk_o)
    base[...] = run
    eid_ref[...] = eid_o.T[:SUBLANES, :].astype(jnp.int32)
    tw_ref[...] = tw_o
    rank_ref[...] = rk_o.T[:SUBLANES, :].astype(jnp.int32)
    cnt_ref[...] = run


def _router(hm, w_r, b_r, base0, tm):
    n, half = hm.shape
    ne = w_r.shape[1]
    kern = functools.partial(_router_kernel, tm=tm, ne=ne)
    tokrow = pl.BlockSpec((tm, LANES), lambda i: (i, 0))
    tokcol = pl.BlockSpec((SUBLANES, tm), lambda i: (0, i))
    return pl.pallas_call(
        kern,
        grid=(n // tm,),
        in_specs=[pl.BlockSpec((tm, half), lambda i: (i, 0)), _full((2 * half, ne)), _full((1, ne)), _full((1, ne))],
        out_specs=[tokcol, tokrow, tokcol, _full((1, ne))],
        out_shape=[jax.ShapeDtypeStruct((SUBLANES, n), jnp.int32), jax.ShapeDtypeStruct((n, LANES), F32),
                   jax.ShapeDtypeStruct((SUBLANES, n), jnp.int32), jax.ShapeDtypeStruct((1, ne), F32)],
        scratch_shapes=[pltpu.VMEM((1, ne), F32)],
        compiler_params=_params(("arbitrary",)),
        name="moe_router",
    )(hm, w_r, b_r, base0)


def _permute_kernel(pos_ref, hm_ref, init_ref, out_ref, sem, *, tm):
    del init_ref

    def copy(t, k):
        return pltpu.make_async_copy(hm_ref.at[pl.ds(t, 1)], out_ref.at[pl.ds(pos_ref[t * TOP_K + k], 1)], sem)

    def issue(t, c):
        for k in range(TOP_K):
            copy(t, k).start()
        return c

    def drain(t, c):
        for k in range(TOP_K):
            copy(t, k).wait()
        return c

    lax.fori_loop(0, tm, issue, 0)
    lax.fori_loop(0, tm, drain, 0)


def _permute(hm, pos_flat, dest, tm):
    n, w = hm.shape
    return pl.pallas_call(
        functools.partial(_permute_kernel, tm=tm),
        grid=(n // tm,),
        in_specs=[pl.BlockSpec((tm * TOP_K,), lambda i: (i,), memory_space=pltpu.SMEM),
                  pl.BlockSpec((tm, w), lambda i: (i, 0)),
                  pl.BlockSpec(memory_space=pl.ANY)],
        out_specs=pl.BlockSpec(memory_space=pl.ANY),
        out_shape=jax.ShapeDtypeStruct(dest.shape, dest.dtype),
        scratch_shapes=[pltpu.SemaphoreType.DMA],
        input_output_aliases={2: 0},
        compiler_params=_params(("arbitrary",)),
        name="moe_permute",
    )(pos_flat, hm, dest)


def _unpermute_kernel(pos_ref, ys_ref, out_ref, sem, *, tm):
    def copy(t, k):
        return pltpu.make_async_copy(ys_ref.at[pl.ds(pos_ref[t * TOP_K + k], 1)], out_ref.at[k, pl.ds(t, 1)], sem)

    def issue(t, c):
        for k in range(TOP_K):
            copy(t, k).start()
        return c

    def drain(t, c):
        for k in range(TOP_K):
            copy(t, k).wait()
        return c

    lax.fori_loop(0, tm, issue, 0)
    lax.fori_loop(0, tm, drain, 0)


def _unpermute(ys, pos_flat, n, tm):
    _, w = ys.shape
    return pl.pallas_call(
        functools.partial(_unpermute_kernel, tm=tm),
        grid=(n // tm,),
        in_specs=[pl.BlockSpec((tm * TOP_K,), lambda i: (i,), memory_space=pltpu.SMEM),
                  pl.BlockSpec(memory_space=pl.ANY)],
        out_specs=pl.BlockSpec((TOP_K, tm, w), lambda i: (0, i, 0)),
        out_shape=jax.ShapeDtypeStruct((TOP_K, n, w), ys.dtype),
        scratch_shapes=[pltpu.SemaphoreType.DMA],
        compiler_params=_params(("arbitrary",)),
        name="moe_unpermute",
    )(pos_flat, ys)


def _sc_mesh():
    return plsc.VectorSubcoreMesh(core_axis_name="c", subcore_axis_name="s")


def _sc_worker():
    return lax.axis_index("s") * SC_CORES + lax.axis_index("c")


def _sc_scatter_rows(rows, idx, m_pad):
    n, w = rows.shape
    per_w = n // SC_WORKERS
    n_chunks = per_w // SC_CHUNK

    def body(rows_hbm, idx_hbm, out_hbm, idx_v, buf, g0, g1, o0, o1):
        gsem, osem = (g0, g1), (o0, o1)
        wid = _sc_worker()
        base = wid * per_w
        pltpu.sync_copy(idx_hbm.at[wid], idx_v)

        def get(c, slot):
            return pltpu.make_async_copy(rows_hbm.at[pl.ds(base + c * SC_CHUNK, SC_CHUNK)], buf.at[slot], gsem[slot])

        def put(c, k, slot):
            return pltpu.make_async_copy(buf.at[slot], out_hbm.at[idx_v.at[c * TOP_K + k]], osem[slot])

        get(0, 0).start()
        for c in range(n_chunks):
            slot = c % 2
            get(c, slot).wait()
            if c + 1 < n_chunks:
                if c >= 1:
                    for k in range(TOP_K):
                        put(c - 1, k, 1 - slot).wait()
                get(c + 1, 1 - slot).start()
            for k in range(TOP_K):
                put(c, k, slot).start()
        for c in range(max(n_chunks - 2, 0), n_chunks):
            for k in range(TOP_K):
                put(c, k, c % 2).wait()

    return pl.kernel(
        body, mesh=_sc_mesh(),
        out_type=jax.ShapeDtypeStruct((m_pad, w), rows.dtype),
        scratch_types=[pltpu.VMEM((n_chunks * TOP_K, SC_CHUNK), jnp.int32), pltpu.VMEM((2, SC_CHUNK, w), rows.dtype)]
                      + [pltpu.SemaphoreType.DMA] * 4,
        name="moe_sc_scatter",
    )(rows, idx)


def _sc_gather_rows(table, idx):
    b = idx.shape[0]
    w = table.shape[1]
    per_w = b // SC_WORKERS
    n_chunks = per_w // SC_CHUNK

    def body(table_hbm, idx_hbm, out_hbm, idx_v, buf, g0, g1, o0, o1):
        gsem, osem = (g0, g1), (o0, o1)
        base = _sc_worker() * per_w
        pltpu.sync_copy(idx_hbm.at[pl.ds(base, per_w)], idx_v)

        def get(c, slot):
            return pltpu.make_async_copy(table_hbm.at[idx_v.at[pl.ds(c * SC_CHUNK, SC_CHUNK)]], buf.at[slot],
                                         gsem[slot])

        def put(c, slot):
            return pltpu.make_async_copy(buf.at[slot], out_hbm.at[pl.ds(base + c * SC_CHUNK, SC_CHUNK)], osem[slot])

        get(0, 0).start()
        for c in range(n_chunks):
            slot = c % 2
            get(c, slot).wait()
            if c + 1 < n_chunks:
                if c >= 1:
                    put(c - 1, 1 - slot).wait()
                get(c + 1, 1 - slot).start()
            put(c, slot).start()
        for c in range(max(n_chunks - 2, 0), n_chunks):
            put(c, c % 2).wait()

    return pl.kernel(
        body, mesh=_sc_mesh(),
        out_type=jax.ShapeDtypeStruct((b, w), table.dtype),
        scratch_types=[pltpu.VMEM((per_w,), jnp.int32), pltpu.VMEM((2, SC_CHUNK, w), table.dtype)]
                      + [pltpu.SemaphoreType.DMA] * 4,
        name="moe_sc_gather",
    )(table, idx)


def _expert_kernel(te_ref, nu_ref, nv_ref, xs_ref, wgu_ref, bgu_ref, wdn_ref, bdn_ref, ys_ref, wgu_bf, wdn_bf, *, d):
    j = pl.program_id(0)
    e = te_ref[j]
    prev = te_ref[jnp.maximum(j - 1, 0)]

    @pl.when(jnp.logical_or(j == 0, e != prev))
    def _():
        rows = 128

        def cast(c, carry):
            r0 = pl.multiple_of(c * rows, rows)
            wgu_bf[pl.ds(r0, rows), :] = wgu_ref[pl.ds(r0, rows), :].astype(BF16)
            wdn_bf[pl.ds(r0, rows), :] = wdn_ref[pl.ds(r0, rows), :].astype(BF16)
            return carry

        lax.fori_loop(0, d // rows, cast, 0)

    nv = jnp.where(j < nu_ref[0], nv_ref[j], 0)
    hrows = xs_ref.shape[0] // 2
    for half in range(2):
        r0 = half * hrows

        @pl.when(nv > r0)
        def _():
            rowi = r0 + lax.broadcasted_iota(jnp.int32, (hrows, xs_ref.shape[1]), 0)
            xw = jnp.where(rowi < nv, xs_ref[r0:r0 + hrows, :], jnp.int32(0))
            gu = jnp.dot(_unpack_bf16(xw), wgu_bf[...], preferred_element_type=F32) + bgu_ref[...]
            g = jnp.minimum(gu[:, :d], SWIGLU_LIMIT)
            u = jnp.clip(gu[:, d:], -SWIGLU_LIMIT, SWIGLU_LIMIT)
            act = (u + 1.0) * (g * _sigmoid(SWIGLU_ALPHA * g))
            y = jnp.dot(act.astype(BF16), wdn_bf[...], preferred_element_type=F32) + bdn_ref[...]
            ys_ref[r0:r0 + hrows, :] = _pack_pairs(y)

        @pl.when(nv <= r0)
        def _():
            ys_ref[r0:r0 + hrows, :] = jnp.zeros((hrows, ys_ref.shape[1]), ys_ref.dtype)


def _experts(xs, tile_expert, n_used, n_valid, w_gu, b_gu, w_dn, b_dn, layer, tm=EXPERT_TM):
    m_pad, half = xs.shape
    d = 2 * half
    nl, ne = w_gu.shape[:2]
    rows = lambda j, te, nu, nv: (jnp.minimum(j, nu[0] - 1), 0)
    wsel = lambda j, te, nu, nv: (layer, te[j], 0, 0)
    grid_spec = pltpu.PrefetchScalarGridSpec(
        num_scalar_prefetch=3,
        grid=(m_pad // tm,),
        in_specs=[pl.BlockSpec((tm, half), rows),
                  pl.BlockSpec((None, None, d, 2 * d), wsel),
                  pl.BlockSpec((None, None, 1, 2 * d), wsel),
                  pl.BlockSpec((None, None, d, d), wsel),
                  pl.BlockSpec((None, None, 1, d), wsel)],
        out_specs=pl.BlockSpec((tm, half), lambda j, te, nu, nv: (j, 0)),
        scratch_shapes=[pltpu.VMEM((d, 2 * d), BF16), pltpu.VMEM((d, d), BF16)],
    )
    return pl.pallas_call(
        functools.partial(_expert_kernel, d=d),
        grid_spec=grid_spec,
        out_shape=jax.ShapeDtypeStruct((m_pad, half), I32),
        compiler_params=_params(("arbitrary",)),
        name="moe_experts",
    )(tile_expert, n_used, n_valid, xs, w_gu, b_gu.reshape(nl, ne, 1, 2 * d), w_dn, b_dn.reshape(nl, ne, 1, d))


def _combine_kernel(x_ref, y_ref, tw_ref, mod_ref, lng_ref, lnb_ref, o_ref, *, d, alpha):
    tw = tw_ref[...]
    m_hi = None
    m_lo = None
    for k in range(TOP_K):
        hi, lo = _unpack_pairs(y_ref[k])
        wk = tw[:, k:k + 1]
        m_hi = wk * hi if m_hi is None else m_hi + wk * hi
        m_lo = wk * lo if m_lo is None else m_lo + wk * lo
    m = jnp.concatenate([m_hi, m_lo], axis=1)
    gate = mod_ref[...][:, 2 * d:]
    o_ref[...] = _layer_norm(alpha * x_ref[...] + (1.0 + gate) * m, lng_ref[...], lnb_ref[...])


def _combine(x, y4, tw, mod, ln_g, ln_b, tm, alpha):
    bsz, s, d = x.shape
    nt = s // tm
    tok = pl.BlockSpec((None, tm, d), lambda b, t: (b, t, 0))
    return pl.pallas_call(
        functools.partial(_combine_kernel, d=d, alpha=alpha),
        grid=(bsz, nt),
        in_specs=[tok,
                  pl.BlockSpec((TOP_K, tm, d // 2), lambda b, t: (0, b * nt + t, 0)),
                  pl.BlockSpec((tm, LANES), lambda b, t: (b * nt + t, 0)),
                  pl.BlockSpec((None, 1, 3 * d), lambda b, t: (b, 0, 0)),
                  _full((1, d)), _full((1, d))],
        out_specs=tok,
        out_shape=jax.ShapeDtypeStruct((bsz, s, d), F32),
        compiler_params=_params(("arbitrary", "arbitrary")),
        name="moe_combine",
    )(x, y4, tw, mod, ln_g, ln_b)


def _moe(hm_p, hm_s, w_r, b_r, w_gu, b_gu, w_dn, b_dn, layer):
    half = hm_p.shape[-1]
    hm_p = hm_p.reshape(-1, half)
    hm_s = hm_s.reshape(-1, half)
    n_p, n_s = hm_p.shape[0], hm_s.shape[0]
    ne = w_r.shape[1]
    w_r = w_r.astype(BF16)
    b_r = b_r.reshape(1, ne)
    eid_p, tw_p, rank_p, cnt_p = _router(hm_p, w_r, b_r, jnp.zeros((1, ne), F32), PROMPT_TM)
    eid_s, tw_s, rank_s, cnt = _router(hm_s, w_r, b_r, cnt_p, n_s)
    tm = EXPERT_TM
    cnt = cnt.reshape(ne).astype(jnp.int32)
    gsz = ((cnt + tm - 1) // tm) * tm
    ends = jnp.cumsum(gsz)
    offs = ends - gsz
    pos_p = offs[eid_p[:TOP_K]] + rank_p[:TOP_K]
    pos_s = (offs[eid_s[:TOP_K]] + rank_s[:TOP_K]).T.reshape(-1)
    n_tiles = ((n_p + n_s) * TOP_K + ne * (tm - 1)) // tm + 1
    m_pad = n_tiles * tm
    tile_start = jnp.arange(n_tiles, dtype=jnp.int32) * tm
    tile_expert = jnp.minimum(jnp.sum((tile_start[:, None] >= ends[None, :]).astype(jnp.int32), axis=1), ne - 1)
    n_used = (ends[-1] // tm).astype(jnp.int32).reshape(1)
    n_valid = jnp.clip((offs + cnt)[tile_expert] - tile_start, 0, tm).astype(jnp.int32)
    idx_scatter = (pos_p.reshape(TOP_K, SC_WORKERS, -1, SC_CHUNK).transpose(1, 2, 0, 3)
                   .reshape(SC_WORKERS, -1, SC_CHUNK))
    xs = _sc_scatter_rows(hm_p, idx_scatter, m_pad)
    xs = _permute(hm_s, pos_s, xs, n_s)
    ys = _experts(xs, tile_expert, n_used, n_valid, w_gu, b_gu, w_dn, b_dn, layer)
    y4_p = _sc_gather_rows(ys, pos_p.reshape(-1)).reshape(TOP_K, n_p, half)
    return (y4_p, tw_p), (_unpermute(ys, pos_s, n_s, n_s), tw_s)


def _qkv_kernel(x_ref, modk_ref, modq_ref, cos_ref, sin_ref, wdkv_ref, gkv_ref, wdq_ref, gq_ref, wuq_ref,
                ckv_ref, kpe_ref, kc_ref, kp_ref, qh_ref, *, d, kvl, rope, nh, nope, q_scale):
    x = x_ref[...]
    cos = cos_ref[...]
    sin = sin_ref[...]
    modk = modk_ref[...]
    hk = (x * (1.0 + modk[:, d:2 * d]) + modk[:, :d]).astype(BF16)
    kv = jnp.dot(hk, wdkv_ref[...], preferred_element_type=F32)
    c = kv[:, :kvl]
    ckv = c * lax.rsqrt(jnp.mean(c * c, axis=-1, keepdims=True) + RMS_EPS) * gkv_ref[...]
    kp = kv[:, kvl:kvl + LANES] * cos + kv[:, kvl + LANES:kvl + 2 * LANES] * sin
    ckv_ref[...] = ckv
    kpe_ref[...] = kp[:, :rope]
    kc_ref[...] = ckv.astype(BF16)
    kp_ref[...] = kp[:, :rope].astype(BF16)

    modq = modq_ref[...]
    hq = (x * (1.0 + modq[:, d:2 * d]) + modq[:, :d]).astype(BF16)
    qd = jnp.dot(hq, wdq_ref[...], preferred_element_type=F32)
    qn = (qd * lax.rsqrt(jnp.mean(qd * qd, axis=-1, keepdims=True) + RMS_EPS) * gq_ref[...]).astype(BF16)
    q = jnp.dot(qn, wuq_ref[...], preferred_element_type=F32)
    for h in range(nh):
        qh_ref[h, :, :nope] = (q[:, h * nope:(h + 1) * nope] * q_scale).astype(BF16)
    pe0 = nh * nope
    sw0 = pe0 + nh * rope
    per = LANES // rope
    for j in range(nh // per):
        r2 = (q[:, pe0 + j * LANES:pe0 + (j + 1) * LANES] * cos
              + q[:, sw0 + j * LANES:sw0 + (j + 1) * LANES] * sin) * q_scale
        for i in range(per):
            qh_ref[j * per + i, :, nope:] = r2[:, i * rope:(i + 1) * rope].astype(BF16)


def _qkv(x, modk, modq, cos_t, sin_t, w, tm, dims):
    bsz, s, d = x.shape
    kvl, rope, nh, nope = dims["kvl"], dims["rope"], dims["nh"], dims["nope"]
    tok = lambda width: pl.BlockSpec((None, tm, width), lambda b, t: (b, t, 0))
    head = lambda width: pl.BlockSpec((None, nh, tm, width), lambda b, t: (b, 0, t, 0))
    tab = pl.BlockSpec((tm, LANES), lambda b, t: (t, 0))
    weights = [w["w_dkv"], w["g_kv"], w["w_dq"], w["g_q"], w["w_uq"]]
    kern = functools.partial(_qkv_kernel, d=d, kvl=kvl, rope=rope, nh=nh, nope=nope, q_scale=dims["q_scale"])
    return pl.pallas_call(
        kern,
        grid=(bsz, s // tm),
        in_specs=[tok(d),
                  pl.BlockSpec((None, 1, 2 * d), lambda b, t: (b, 0, 0)),
                  pl.BlockSpec((None, 1, 3 * d), lambda b, t: (b, 0, 0)),
                  tab, tab] + [_full(a.shape) for a in weights],
        out_specs=[tok(kvl), tok(rope), tok(kvl), tok(rope), head(nope + rope)],
        out_shape=[jax.ShapeDtypeStruct((bsz, s, kvl), F32), jax.ShapeDtypeStruct((bsz, s, rope), F32),
                   jax.ShapeDtypeStruct((bsz, s, kvl), BF16), jax.ShapeDtypeStruct((bsz, s, rope), BF16),
                   jax.ShapeDtypeStruct((bsz, nh, s, nope + rope), BF16)],
        compiler_params=_params(("arbitrary", "arbitrary")),
        name="mla_qkv",
    )(x, modk, modq, cos_t, sin_t, *weights)


def _kproj_kernel(kc_ref, kp_ref, wuk_ref, kh_ref, *, nh, nope):
    kc = kc_ref[...]
    kp = kp_ref[...]
    nt = (((1,), (1,)), ((), ()))
    for h in range(nh):
        kh_ref[h, :, :nope] = lax.dot_general(kc, wuk_ref[h], nt, preferred_element_type=F32).astype(BF16)
        kh_ref[h, :, nope:] = kp


def _kproj(kc, kp, w_uk, tm):
    bsz, s, kvl = kc.shape
    rope = kp.shape[-1]
    nh, nope, _ = w_uk.shape
    tok = lambda width: pl.BlockSpec((None, tm, width), lambda b, t: (b, t, 0))
    return pl.pallas_call(
        functools.partial(_kproj_kernel, nh=nh, nope=nope),
        grid=(bsz, s // tm),
        in_specs=[tok(kvl), tok(rope), _full(w_uk.shape)],
        out_specs=pl.BlockSpec((None, nh, tm, nope + rope), lambda b, t: (b, 0, t, 0)),
        out_shape=jax.ShapeDtypeStruct((bsz, nh, s, nope + rope), BF16),
        compiler_params=_params(("arbitrary", "arbitrary")),
        name="mla_kproj",
    )(kc, kp, w_uk)


def _attn_kernel(qi_ref, kj_ref, last_ref, edge_ref, qh_ref, kh_ref, kc_ref, o_ref, m_s, l_s, acc_s,
                 *, nh, tq, tk, kvl, causal, valid):
    s_id = pl.program_id(1)
    qi = qi_ref[s_id]
    kj = kj_ref[s_id]
    nt = (((1,), (1,)), ((), ()))

    @pl.when(kj == 0)
    def _():
        m_s[...] = jnp.full_like(m_s, -jnp.inf)
        l_s[...] = jnp.zeros_like(l_s)
        acc_s[...] = jnp.zeros_like(acc_s)

    def sweep(masked):
        kc = kc_ref[...]
        if masked:
            kidx = kj * tk + lax.broadcasted_iota(jnp.int32, (tq, tk), 1)
            if causal:
                qidx = qi * tq + lax.broadcasted_iota(jnp.int32, (tq, tk), 0)
                visible = (kidx // CHUNK) <= (qidx // CHUNK)
            else:
                visible = kidx < valid
        def scores(h):
            s = lax.dot_general(qh_ref[h], kh_ref[h], nt, preferred_element_type=F32)
            return jnp.where(visible, s, -jnp.inf) if masked else s

        s_next = scores(0)
        for h in range(nh):
            s = s_next
            if h + 1 < nh:
                s_next = scores(h + 1)
            slabs = [s[:, j * LANES:(j + 1) * LANES] for j in range(tk // LANES)]
            mx = slabs[0]
            for sj in slabs[1:]:
                mx = jnp.maximum(mx, sj)
            m_prev = m_s[h]
            m_new = jnp.maximum(m_prev, jnp.max(mx, axis=1, keepdims=True))
            scale = jnp.exp2(m_prev - m_new)
            ps = []
            rsum = None
            for sj in slabs:
                pj = jnp.exp2(sj - m_new)
                ps.append(pj.astype(BF16))
                rsum = pj if rsum is None else rsum + pj
            p = jnp.concatenate(ps, axis=1)
            l_s[h] = scale * l_s[h] + jnp.sum(rsum, axis=1, keepdims=True)
            pv = jnp.dot(p, kc, preferred_element_type=F32)
            acc_s[h] = acc_s[h] * jnp.concatenate([scale] * (kvl // LANES), axis=1) + pv
            m_s[h] = m_new

    edge = edge_ref[s_id]

    @pl.when(edge == 1)
    def _():
        sweep(True)

    @pl.when(edge == 0)
    def _():
        sweep(False)

    @pl.when(last_ref[s_id] == 1)
    def _():
        for h in range(nh):
            inv = jnp.concatenate([l_s[h]] * (kvl // LANES), axis=1)
            o_ref[h] = (acc_s[h] / inv).astype(BF16)


def _attention(qh, kh, kc, tq, tk, causal, valid):
    bsz, nh, s, qk = qh.shape
    skv, kvl = kc.shape[1], kc.shape[2]
    qi, kj, last, edge = [], [], [], []
    for i in range(s // tq):
        hi = ((i * tq + tq - 1) // tk) if causal else (skv // tk - 1)
        for j in range(hi + 1):
            qi.append(i)
            kj.append(j)
            last.append(1 if j == hi else 0)
            if causal:
                edge.append(1 if ((j + 1) * tk - 1) // CHUNK > (i * tq) // CHUNK else 0)
            else:
                edge.append(1 if (j + 1) * tk > valid else 0)
    tabs = [jnp.asarray(np.array(a, np.int32)) for a in (qi, kj, last, edge)]
    qspec = lambda width: pl.BlockSpec((None, nh, tq, width), lambda b, t, qi, kj, last, edge: (b, 0, qi[t], 0))
    khspec = pl.BlockSpec((None, nh, tk, qk), lambda b, t, qi, kj, last, edge: (b, 0, kj[t], 0))
    kcspec = pl.BlockSpec((None, tk, kvl), lambda b, t, qi, kj, last, edge: (b, kj[t], 0))
    grid_spec = pltpu.PrefetchScalarGridSpec(
        num_scalar_prefetch=4,
        grid=(bsz, len(qi)),
        in_specs=[qspec(qk), khspec, kcspec],
        out_specs=qspec(kvl),
        scratch_shapes=[pltpu.VMEM((nh, tq, LANES), F32), pltpu.VMEM((nh, tq, LANES), F32),
                        pltpu.VMEM((nh, tq, kvl), F32)],
    )
    kern = functools.partial(_attn_kernel, nh=nh, tq=tq, tk=tk, kvl=kvl, causal=causal, valid=valid)
    return pl.pallas_call(
        kern,
        grid_spec=grid_spec,
        out_shape=jax.ShapeDtypeStruct((bsz, nh, s, kvl), BF16),
        compiler_params=_params(("arbitrary", "arbitrary")),
        name="mla_attention",
    )(*tabs, qh, kh, kc)


def _attn_out_kernel(o_ref, x_ref, mod_ref, mod2_ref, wuv_ref, wo_ref, lng_ref, lnb_ref, x3_ref, hm_ref,
                     *, d, nh, alpha):
    parts = [jnp.dot(o_ref[h], wuv_ref[h], preferred_element_type=F32) for h in range(nh)]
    o = jnp.concatenate(parts, axis=1).astype(BF16)
    m = jnp.dot(o, wo_ref[...], preferred_element_type=F32)
    gate = mod_ref[...][:, 2 * d:]
    x3 = _layer_norm(alpha * x_ref[...] + (1.0 + gate) * m, lng_ref[...], lnb_ref[...])
    x3_ref[...] = x3
    mod2 = mod2_ref[...]
    hm_ref[...] = _pack_pairs(x3 * (1.0 + mod2[:, d:2 * d]) + mod2[:, :d])


def _attn_out(o_lat, x, mod, mod2, w_uv, w_o, ln_g, ln_b, tm, alpha):
    bsz, s, d = x.shape
    nh, kvl = o_lat.shape[1], o_lat.shape[3]
    tok = lambda width: pl.BlockSpec((None, tm, width), lambda b, t: (b, t, 0))
    modspec = pl.BlockSpec((None, 1, 3 * d), lambda b, t: (b, 0, 0))
    return pl.pallas_call(
        functools.partial(_attn_out_kernel, d=d, nh=nh, alpha=alpha),
        grid=(bsz, s // tm),
        in_specs=[pl.BlockSpec((None, nh, tm, kvl), lambda b, t: (b, 0, t, 0)), tok(d), modspec, modspec,
                  _full(w_uv.shape), _full(w_o.shape), _full((1, d)), _full((1, d))],
        out_specs=[tok(d), tok(d // 2)],
        out_shape=[jax.ShapeDtypeStruct((bsz, s, d), F32), jax.ShapeDtypeStruct((bsz, s, d // 2), I32)],
        compiler_params=_params(("arbitrary", "arbitrary")),
        name="mla_out",
    )(o_lat, x, mod, mod2, w_uv, w_o, ln_g, ln_b)


def _rope_tables(pos, rope):
    half = rope // 2
    inv = ROPE_THETA ** (-2.0 * jnp.arange(half, dtype=F32) / rope)
    ang = pos.astype(F32)[:, None] * inv[None, :]
    cos, sin = jnp.cos(ang), jnp.sin(ang)
    rep = LANES // rope
    return (jnp.concatenate([cos, cos] * rep, axis=1), jnp.concatenate([-sin, sin] * rep, axis=1))


def kernel(x_prompt, x_sample, state_conv, state_rglru, cache_ckv, cache_kpe, c_prompt, c_sample, w_ada, b_ada, ln_g, ln_b, lru_w_in, lru_b_in, lru_conv_w, lru_conv_b, lru_w_a, lru_b_a, lru_w_i, lru_b_i, lru_lambda, lru_w_out, lru_b_out, kv_w_ada, kv_b_ada, mla_w_dkv, mla_g_kv, mla_w_uk, mla_w_uv, mla_w_dq, mla_g_q, mla_w_uq, mla_w_o, moe_w_r, moe_b_r, moe_w_gu, moe_b_gu, moe_w_dn, moe_b_dn):
    bp, sp, d = x_prompt.shape
    bs, ss, _ = x_sample.shape
    depth = w_ada.shape[0]
    alpha = float((2.0 * depth) ** 0.25)
    nh, nope, kvl = mla_w_uk.shape
    rope = cache_kpe.shape[-1]
    past = cache_ckv.shape[1]
    hist = state_conv.shape[2]
    dims = dict(kvl=kvl, rope=rope, nh=nh, nope=nope, q_scale=float((nope + rope) ** -0.5 * LOG2E))
    row = lambda v: v.reshape(1, -1)

    nrow = bp + bs
    rpad = -nrow % (2 * SUBLANES)
    c_rows = jnp.concatenate([c_prompt, c_sample, jnp.zeros((rpad, d), F32)], axis=0)
    mods = _ada(c_rows, w_ada.reshape(depth * 2, d, 3 * d), b_ada.reshape(depth * 2, 1, 3 * d))
    modkv = _ada(c_rows, kv_w_ada.reshape(1, d, 2 * d), kv_b_ada.reshape(1, 1, 2 * d))[0]
    mod_p = lambda i: mods[i, :bp].reshape(bp, 1, 3 * d)
    mod_s = lambda i: mods[i, bp:nrow].reshape(bs, 1, 3 * d)

    l0w = dict(w_in=lru_w_in[0].astype(BF16), b_in=row(lru_b_in[0]), conv_w=lru_conv_w[0], conv_b=row(lru_conv_b[0]),
               w_a=lru_w_a[0].astype(BF16), b_a=row(lru_b_a[0]), w_i=lru_w_i[0].astype(BF16), b_i=row(lru_b_i[0]),
               lam=row(lru_lambda[0]), w_out=lru_w_out[0].astype(BF16), b_out=row(lru_b_out[0]),
               ln_g=row(ln_g[0, 0]), ln_b=row(ln_b[0, 0]))
    x1p, hmp, conv_p, h_p = _l0_mixer(x_prompt, mod_p(0), mod_p(1), jnp.zeros((bp, hist, d), F32),
                                      jnp.zeros((bp, 1, d), F32), l0w, L0_TB, alpha)
    x1s, hms, conv_s, h_s = _l0_mixer(x_sample, mod_s(0), mod_s(1), state_conv[0], state_rglru[0].reshape(bs, 1, d),
                                      l0w, ss, alpha)

    (y4p, twp), (y4s, tws) = _moe(hmp, hms, moe_w_r[0], moe_b_r[0], moe_w_gu, moe_b_gu, moe_w_dn, moe_b_dn, 0)
    x2p = _combine(x1p, y4p, twp, mod_p(1), row(ln_g[0, 1]), row(ln_b[0, 1]), PROMPT_TM, alpha)
    x2s = _combine(x1s, y4s, tws, mod_s(1), row(ln_g[0, 1]), row(ln_b[0, 1]), ss, alpha)

    pe_cols = jnp.arange(rope)
    sw_cols = jnp.concatenate([pe_cols[rope // 2:], pe_cols[:rope // 2]])
    zpad = jnp.zeros((d, LANES - rope), F32)
    w_dkv_ext = jnp.concatenate([mla_w_dkv[:, :kvl], mla_w_dkv[:, kvl:], zpad,
                                 mla_w_dkv[:, kvl:][:, sw_cols], zpad], axis=1).astype(BF16)
    wq = mla_w_uq[0].reshape(-1, nh, nope + rope)
    w_uq_ext = jnp.concatenate([wq[:, :, :nope].reshape(-1, nh * nope), wq[:, :, nope:].reshape(-1, nh * rope),
                                wq[:, :, nope:][:, :, sw_cols].reshape(-1, nh * rope)], axis=1).astype(BF16)
    qw = dict(w_dkv=w_dkv_ext, g_kv=row(mla_g_kv), w_dq=mla_w_dq[0].astype(BF16), g_q=row(mla_g_q[0]),
              w_uq=w_uq_ext)
    w_uk = mla_w_uk.astype(BF16)
    cos_p, sin_p = _rope_tables(jnp.arange(sp), rope)
    cos_s, sin_s = _rope_tables(past + jnp.arange(ss), rope)
    modkv_p = modkv[:bp].reshape(bp, 1, 2 * d)
    modkv_s = modkv[bp:nrow].reshape(bs, 1, 2 * d)
    ckv_p, kpe_p, kc_p, kp_p, qh_p = _qkv(x2p, modkv_p, mod_p(2), cos_p, sin_p, qw, PROMPT_TM, dims)
    ckv_s, kpe_s, kc_s, kp_s, qh_s = _qkv(x2s, modkv_s, mod_s(2), cos_s, sin_s, qw, ss, dims)

    o_p = _attention(qh_p, _kproj(kc_p, kp_p, w_uk, PROMPT_TM), kc_p, ATT_TQ, ATT_TK, True, sp)
    skv = past + ss
    kpad = -skv % LANES
    kc_all = jnp.concatenate([cache_ckv.astype(BF16), kc_s, jnp.zeros((bs, kpad, kvl), BF16)], axis=1)
    kp_all = jnp.concatenate([cache_kpe.astype(BF16), kp_s, jnp.zeros((bs, kpad, rope), BF16)], axis=1)
    o_s = _attention(qh_s, _kproj(kc_all, kp_all, w_uk, skv + kpad), kc_all, ss, skv + kpad, False, skv)
    w_uv = mla_w_uv.astype(BF16)
    w_o = mla_w_o[0].astype(BF16)
    x3p, hm3p = _attn_out(o_p, x2p, mod_p(2), mod_p(3), w_uv, w_o, row(ln_g[1, 0]), row(ln_b[1, 0]), PROMPT_TM, alpha)
    x3s, hm3s = _attn_out(o_s, x2s, mod_s(2), mod_s(3), w_uv, w_o, row(ln_g[1, 0]), row(ln_b[1, 0]), ss, alpha)

    (y4p, twp), (y4s, tws) = _moe(hm3p, hm3s, moe_w_r[1], moe_b_r[1], moe_w_gu, moe_b_gu, moe_w_dn, moe_b_dn, 1)
    y_p = _combine(x3p, y4p, twp, mod_p(3), row(ln_g[1, 1]), row(ln_b[1, 1]), PROMPT_TM, alpha)
    y_s = _combine(x3s, y4s, tws, mod_s(3), row(ln_g[1, 1]), row(ln_b[1, 1]), ss, alpha)

    return (y_p, y_s, conv_p[None], h_p.reshape(1, bp, d), ckv_p, kpe_p,
            conv_s[None], h_s.reshape(1, bs, d), ckv_s, kpe_s)
```

```python
import functools

import numpy as np
import jax
import jax.numpy as jnp
from jax import lax
from jax.experimental import pallas as pl
from jax.experimental.pallas import tpu as pltpu
from jax.experimental.pallas import tpu_sc as plsc

F32 = jnp.float32
BF16 = jnp.bfloat16
I32 = jnp.int32

CHUNK = 64
N_LRU_BLOCKS = 8
LRU_C = 8.0
ROPE_THETA = 10000.0
TOP_K = 4
SWIGLU_ALPHA = 1.702
SWIGLU_LIMIT = 7.0
LN_EPS = 1e-5
RMS_EPS = 1e-6
LOG2E = 1.4426950408889634

LANES = 128
SUBLANES = 8
VMEM_LIMIT = 56 * 1024 * 1024

PROMPT_TM = 512
L0_TB = 256
EXPERT_TM = 512
SC_CORES = 2
SC_WORKERS = SC_CORES * 16
SC_CHUNK = 64
ATT_TQ = 512
ATT_TK = 1024


def _sigmoid(x):
    return 0.5 * (jnp.tanh(0.5 * x) + 1.0)


def _layer_norm(v, g, b):
    mu = jnp.mean(v, axis=-1, keepdims=True)
    d = v - mu
    var = jnp.mean(d * d, axis=-1, keepdims=True)
    return d * lax.rsqrt(var + LN_EPS) * g + b


def _pack_pairs(x):
    w = x.shape[1] // 2
    hi = lax.bitcast_convert_type(x[:, :w].astype(BF16).astype(F32), I32)
    lo = lax.bitcast_convert_type(x[:, w:].astype(BF16).astype(F32), I32)
    return hi | lax.shift_right_logical(lo, jnp.int32(16))


def _unpack_pairs(words):
    hi = lax.bitcast_convert_type(words & jnp.int32(-65536), F32)
    lo = lax.bitcast_convert_type(words << 16, F32)
    return hi, lo


def _unpack_bf16(words):
    hi, lo = _unpack_pairs(words)
    return jnp.concatenate([hi.astype(BF16), lo.astype(BF16)], axis=1)


def _full(shape):
    nd = len(shape)
    return pl.BlockSpec(shape, lambda *_: (0,) * nd)


def _params(sem, vmem=VMEM_LIMIT):
    return pltpu.CompilerParams(dimension_semantics=sem, vmem_limit_bytes=vmem)


def _ada_kernel(c_ref, w_ref, b_ref, o_ref):
    c = c_ref[...]
    s = (c * _sigmoid(c)).astype(BF16)
    o_ref[...] = jnp.dot(s, w_ref[...].astype(BF16), preferred_element_type=F32) + b_ref[...]


def _ada(c_rows, w, b, tn=1024):
    g, d, n = w.shape
    r = c_rows.shape[0]
    return pl.pallas_call(
        _ada_kernel,
        grid=(g, n // tn),
        in_specs=[_full((r, d)),
                  pl.BlockSpec((None, d, tn), lambda i, j: (i, 0, j)),
                  pl.BlockSpec((None, 1, tn), lambda i, j: (i, 0, j))],
        out_specs=pl.BlockSpec((None, r, tn), lambda i, j: (i, 0, j)),
        out_shape=jax.ShapeDtypeStruct((g, r, n), F32),
        compiler_params=_params(("arbitrary", "arbitrary")),
        name="ada_mod",
    )(c_rows, w, b)


def _l0_kernel(x_ref, mod_ref, mod2_ref, conv0_ref, h0_ref, win_ref, bin_ref, cw_ref, cb_ref,
               wa_ref, ba_ref, wi_ref, bi_ref, lam_ref, wout_ref, bout_ref, lng_ref, lnb_ref,
               x1_ref, hm_ref, convn_ref, hlast_ref,
               ubuf, a_s, b_s, h_s, *, tb, d, alpha):
    t = pl.program_id(1)
    pad = SUBLANES
    hist = convn_ref.shape[0]

    @pl.when(t == 0)
    def _():
        ubuf[pad - hist:pad, :] = conv0_ref[...]
        h_s[...] = h0_ref[...]

    x = x_ref[...]
    mod = mod_ref[...]
    shift, scale, gate = mod[:, :d], mod[:, d:2 * d], mod[:, 2 * d:]
    h = (x * (1.0 + scale) + shift).astype(BF16)
    proj = jnp.dot(h, win_ref[...], preferred_element_type=F32) + bin_ref[...]
    gate_b = proj[:, :d]
    ubuf[pad:pad + tb, :] = proj[:, d:]

    cw = cw_ref[...]
    u = cb_ref[...]
    for k in range(hist + 1):
        u = u + ubuf[pad - hist + k:pad - hist + k + tb, :] * cw[k:k + 1, :]
    ubuf[pad - hist:pad, :] = ubuf[pad + tb - hist:pad + tb, :]

    ub = u.astype(BF16)
    blk = d // N_LRU_BLOCKS
    lam = lam_ref[...]
    neg = -lam
    softplus = jnp.maximum(neg, 0.0) + jnp.log1p(jnp.exp(-jnp.abs(neg)))
    for n in range(N_LRU_BLOCKS):
        sl = slice(n * blk, (n + 1) * blk)
        un = ub[:, sl]
        r = _sigmoid(jnp.dot(un, wa_ref[n], preferred_element_type=F32) + ba_ref[:, sl])
        ig = _sigmoid(jnp.dot(un, wi_ref[n], preferred_element_type=F32) + bi_ref[:, sl])
        log_a = -LRU_C * r * softplus[:, sl]
        a = jnp.exp(log_a)
        one_m_a2 = -jnp.tanh(log_a) * (a * a + 1.0)
        a_s[:, sl] = a
        b_s[:, sl] = jnp.sqrt(one_m_a2) * (ig * u[:, sl])

    row = lax.broadcasted_iota(jnp.int32, (SUBLANES, d), 0)

    def group(g, hprev):
        r0 = pl.multiple_of(g * SUBLANES, SUBLANES)
        a = a_s[pl.ds(r0, SUBLANES), :]
        b = b_s[pl.ds(r0, SUBLANES), :]
        for sh in (1, 2, 4):
            keep = row >= sh
            a_sh = pltpu.roll(a, sh, axis=0)
            b_sh = pltpu.roll(b, sh, axis=0)
            b = jnp.where(keep, a * b_sh + b, b)
            a = jnp.where(keep, a * a_sh, a)
        hh = a * hprev + b
        b_s[pl.ds(r0, SUBLANES), :] = hh
        return hh[SUBLANES - 1:SUBLANES, :]

    h_s[...] = lax.fori_loop(0, tb // SUBLANES, group, h_s[...])

    y = b_s[...]
    gl = 0.5 * gate_b * (1.0 + jnp.tanh(0.7978845608028654 * (gate_b + 0.044715 * gate_b * gate_b * gate_b)))
    out = jnp.dot((gl * y).astype(BF16), wout_ref[...], preferred_element_type=F32) + bout_ref[...]
    x1 = _layer_norm(alpha * x + (1.0 + gate) * out, lng_ref[...], lnb_ref[...])
    x1_ref[...] = x1
    mod2 = mod2_ref[...]
    hm_ref[...] = _pack_pairs(x1 * (1.0 + mod2[:, d:2 * d]) + mod2[:, :d])

    @pl.when(t == pl.num_programs(1) - 1)
    def _():
        convn_ref[...] = ubuf[pad - hist:pad, :]
        hlast_ref[...] = h_s[...]


def _l0_mixer(x, mod, mod2, conv0, h0, w, tb, alpha):
    bsz, s, d = x.shape
    hist = conv0.shape[1]
    tok = lambda width: pl.BlockSpec((None, tb, width), lambda b, t: (b, t, 0))
    per_b = lambda rows, width: pl.BlockSpec((None, rows, width), lambda b, t: (b, 0, 0))
    kern = functools.partial(_l0_kernel, tb=tb, d=d, alpha=alpha)
    weights = [w["w_in"], w["b_in"], w["conv_w"], w["conv_b"], w["w_a"], w["b_a"], w["w_i"], w["b_i"],
               w["lam"], w["w_out"], w["b_out"], w["ln_g"], w["ln_b"]]
    return pl.pallas_call(
        kern,
        grid=(bsz, s // tb),
        in_specs=[tok(d), per_b(1, 3 * d), per_b(1, 3 * d), per_b(hist, d), per_b(1, d)]
                 + [_full(a.shape) for a in weights],
        out_specs=[tok(d), tok(d // 2), per_b(hist, d), per_b(1, d)],
        out_shape=[jax.ShapeDtypeStruct((bsz, s, d), F32), jax.ShapeDtypeStruct((bsz, s, d // 2), I32),
                   jax.ShapeDtypeStruct((bsz, hist, d), F32), jax.ShapeDtypeStruct((bsz, 1, d), F32)],
        scratch_shapes=[pltpu.VMEM((tb + SUBLANES, d), F32), pltpu.VMEM((tb, d), F32),
                        pltpu.VMEM((tb, d), F32), pltpu.VMEM((1, d), F32)],
        compiler_params=_params(("arbitrary", "arbitrary")),
        name="l0_mixer",
    )(x, mod, mod2, conv0, h0, *weights)


def _router_kernel(hm_ref, wr_ref, br_ref, base0_ref, eid_ref, tw_ref, rank_ref, cnt_ref, base, *, tm, ne):
    i = pl.program_id(0)

    @pl.when(i == 0)
    def _():
        base[...] = base0_ref[...]

    logits = jnp.dot(_unpack_bf16(hm_ref[...]), wr_ref[...], preferred_element_type=F32) + br_ref[...]
    col = lax.broadcasted_iota(jnp.int32, (tm, ne), 1).astype(F32)
    l = logits
    vals, idxs = [], []
    for _ in range(TOP_K):
        m = jnp.max(l, axis=-1, keepdims=True)
        idx = jnp.min(jnp.where(l == m, col, float(ne)), axis=-1, keepdims=True)
        vals.append(m)
        idxs.append(idx)
        l = jnp.where(col == idx, -jnp.inf, l)
    es = [jnp.exp(v - vals[0]) for v in vals]
    tot = es[0] + es[1] + es[2] + es[3]

    ri = lax.broadcasted_iota(jnp.int32, (tm, tm), 0)
    ci = lax.broadcasted_iota(jnp.int32, (tm, tm), 1)
    tri = (ci < ri).astype(BF16)
    run = base[...]
    lane = lax.broadcasted_iota(jnp.int32, (tm, LANES), 1)
    eid_o = jnp.zeros((tm, LANES), F32)
    tw_o = jnp.zeros((tm, LANES), F32)
    rk_o = jnp.zeros((tm, LANES), F32)
    for k in range(TOP_K):
        oh = col == idxs[k]
        ohf = oh.astype(F32)
        before = jnp.dot(tri, ohf.astype(BF16), preferred_element_type=F32)
        rank = jnp.sum(jnp.where(oh, before + run, 0.0), axis=-1, keepdims=True)
        run = run + jnp.sum(ohf, axis=0, keepdims=True)
        sel = lane == k
        eid_o = jnp.where(sel, idxs[k], eid_o)
        tw_o = jnp.where(sel, es[k] / tot, tw_o)
        rk_o = jnp.where(sel, rank, rk_o)
    base[...] = run
    eid_ref[...] = eid_o.T[:SUBLANES, :].astype(jnp.int32)
    tw_ref[...] = tw_o
    rank_ref[...] = rk_o.T[:SUBLANES, :].astype(jnp.int32)
    cnt_ref[...] = run


def _router(hm, w_r, b_r, base0, tm):
    n, half = hm.shape
    ne = w_r.shape[1]
    kern = functools.partial(_router_kernel, tm=tm, ne=ne)
    tokrow = pl.BlockSpec((tm, LANES), lambda i: (i, 0))
    tokcol = pl.BlockSpec((SUBLANES, tm), lambda i: (0, i))
    return pl.pallas_call(
        kern,
        grid=(n // tm,),
        in_specs=[pl.BlockSpec((tm, half), lambda i: (i, 0)), _full((2 * half, ne)), _full((1, ne)), _full((1, ne))],
        out_specs=[tokcol, tokrow, tokcol, _full((1, ne))],
        out_shape=[jax.ShapeDtypeStruct((SUBLANES, n), jnp.int32), jax.ShapeDtypeStruct((n, LANES), F32),
                   jax.ShapeDtypeStruct((SUBLANES, n), jnp.int32), jax.ShapeDtypeStruct((1, ne), F32)],
        scratch_shapes=[pltpu.VMEM((1, ne), F32)],
        compiler_params=_params(("arbitrary",)),
        name="moe_router",
    )(hm, w_r, b_r, base0)


def _permute_kernel(pos_ref, hm_ref, init_ref, out_ref, sem, *, tm):
    del init_ref

    def copy(t, k):
        return pltpu.make_async_copy(hm_ref.at[pl.ds(t, 1)], out_ref.at[pl.ds(pos_ref[t * TOP_K + k], 1)], sem)

    def issue(t, c):
        for k in range(TOP_K):
            copy(t, k).start()
        return c

    def drain(t, c):
        for k in range(TOP_K):
            copy(t, k).wait()
        return c

    lax.fori_loop(0, tm, issue, 0)
    lax.fori_loop(0, tm, drain, 0)


def _permute(hm, pos_flat, dest, tm):
    n, w = hm.shape
    return pl.pallas_call(
        functools.partial(_permute_kernel, tm=tm),
        grid=(n // tm,),
        in_specs=[pl.BlockSpec((tm * TOP_K,), lambda i: (i,), memory_space=pltpu.SMEM),
                  pl.BlockSpec((tm, w), lambda i: (i, 0)),
                  pl.BlockSpec(memory_space=pl.ANY)],
        out_specs=pl.BlockSpec(memory_space=pl.ANY),
        out_shape=jax.ShapeDtypeStruct(dest.shape, dest.dtype),
        scratch_shapes=[pltpu.SemaphoreType.DMA],
        input_output_aliases={2: 0},
        compiler_params=_params(("arbitrary",)),
        name="moe_permute",
    )(pos_flat, hm, dest)


def _unpermute_kernel(pos_ref, ys_ref, out_ref, sem, *, tm):
    def copy(t, k):
        return pltpu.make_async_copy(ys_ref.at[pl.ds(pos_ref[t * TOP_K + k], 1)], out_ref.at[k, pl.ds(t, 1)], sem)

    def issue(t, c):
        for k in range(TOP_K):
            copy(t, k).start()
        return c

    def drain(t, c):
        for k in range(TOP_K):
            copy(t, k).wait()
        return c

    lax.fori_loop(0, tm, issue, 0)
    lax.fori_loop(0, tm, drain, 0)


def _unpermute(ys, pos_flat, n, tm):
    _, w = ys.shape
    return pl.pallas_call(
        functools.partial(_unpermute_kernel, tm=tm),
        grid=(n // tm,),
        in_specs=[pl.BlockSpec((tm * TOP_K,), lambda i: (i,), memory_space=pltpu.SMEM),
                  pl.BlockSpec(memory_space=pl.ANY)],
        out_specs=pl.BlockSpec((TOP_K, tm, w), lambda i: (0, i, 0)),
        out_shape=jax.ShapeDtypeStruct((TOP_K, n, w), ys.dtype),
        scratch_shapes=[pltpu.SemaphoreType.DMA],
        compiler_params=_params(("arbitrary",)),
        name="moe_unpermute",
    )(pos_flat, ys)


def _sc_mesh():
    return plsc.VectorSubcoreMesh(core_axis_name="c", subcore_axis_name="s")


def _sc_worker():
    return lax.axis_index("s") * SC_CORES + lax.axis_index("c")


def _sc_scatter_rows(rows, idx, m_pad):
    n, w = rows.shape
    per_w = n // SC_WORKERS
    n_chunks = per_w // SC_CHUNK

    def body(rows_hbm, idx_hbm, out_hbm, idx_v, buf, g0, g1, o0, o1):
        gsem, osem = (g0, g1), (o0, o1)
        wid = _sc_worker()
        base = wid * per_w
        pltpu.sync_copy(idx_hbm.at[wid], idx_v)

        def get(c, slot):
            return pltpu.make_async_copy(rows_hbm.at[pl.ds(base + c * SC_CHUNK, SC_CHUNK)], buf.at[slot], gsem[slot])

        def put(c, k, slot):
            return pltpu.make_async_copy(buf.at[slot], out_hbm.at[idx_v.at[c * TOP_K + k]], osem[slot])

        get(0, 0).start()
        for c in range(n_chunks):
            slot = c % 2
            get(c, slot).wait()
            if c + 1 < n_chunks:
                if c >= 1:
                    for k in range(TOP_K):
                        put(c - 1, k, 1 - slot).wait()
                get(c + 1, 1 - slot).start()
            for k in range(TOP_K):
                put(c, k, slot).start()
        for c in range(max(n_chunks - 2, 0), n_chunks):
            for k in range(TOP_K):
                put(c, k, c % 2).wait()

    return pl.kernel(
        body, mesh=_sc_mesh(),
        out_type=jax.ShapeDtypeStruct((m_pad, w), rows.dtype),
        scratch_types=[pltpu.VMEM((n_chunks * TOP_K, SC_CHUNK), jnp.int32), pltpu.VMEM((2, SC_CHUNK, w), rows.dtype)]
                      + [pltpu.SemaphoreType.DMA] * 4,
        name="moe_sc_scatter",
    )(rows, idx)


def _sc_gather_rows(table, idx):
    b = idx.shape[0]
    w = table.shape[1]
    per_w = b // SC_WORKERS
    n_chunks = per_w // SC_CHUNK

    def body(table_hbm, idx_hbm, out_hbm, idx_v, buf, g0, g1, o0, o1):
        gsem, osem = (g0, g1), (o0, o1)
        base = _sc_worker() * per_w
        pltpu.sync_copy(idx_hbm.at[pl.ds(base, per_w)], idx_v)

        def get(c, slot):
            return pltpu.make_async_copy(table_hbm.at[idx_v.at[pl.ds(c * SC_CHUNK, SC_CHUNK)]], buf.at[slot],
                                         gsem[slot])

        def put(c, slot):
            return pltpu.make_async_copy(buf.at[slot], out_hbm.at[pl.ds(base + c * SC_CHUNK, SC_CHUNK)], osem[slot])

        get(0, 0).start()
        for c in range(n_chunks):
            slot = c % 2
            get(c, slot).wait()
            if c + 1 < n_chunks:
                if c >= 1:
                    put(c - 1, 1 - slot).wait()
                get(c + 1, 1 - slot).start()
            put(c, slot).start()
        for c in range(max(n_chunks - 2, 0), n_chunks):
            put(c, c % 2).wait()

    return pl.kernel(
        body, mesh=_sc_mesh(),
        out_type=jax.ShapeDtypeStruct((b, w), table.dtype),
        scratch_types=[pltpu.VMEM((per_w,), jnp.int32), pltpu.VMEM((2, SC_CHUNK, w), table.dtype)]
                      + [pltpu.SemaphoreType.DMA] * 4,
        name="moe_sc_gather",
    )(table, idx)


def _expert_kernel(te_ref, nu_ref, nv_ref, xs_ref, wgu_ref, bgu_ref, wdn_ref, bdn_ref, ys_ref, wgu_bf, wdn_bf, *, d):
    j = pl.program_id(0)
    e = te_ref[j]
    prev = te_ref[jnp.maximum(j - 1, 0)]

    @pl.when(jnp.logical_or(j == 0, e != prev))
    def _():
        rows = 128

        def cast(c, carry):
            r0 = pl.multiple_of(c * rows, rows)
            wgu_bf[pl.ds(r0, rows), :] = wgu_ref[pl.ds(r0, rows), :].astype(BF16)
            wdn_bf[pl.ds(r0, rows), :] = wdn_ref[pl.ds(r0, rows), :].astype(BF16)
            return carry

        lax.fori_loop(0, d // rows, cast, 0)

    @pl.when(j < nu_ref[0])
    def _():
        rowi = lax.broadcasted_iota(jnp.int32, xs_ref.shape, 0)
        xw = jnp.where(rowi < nv_ref[j], xs_ref[...], jnp.int32(0))
        gu = jnp.dot(_unpack_bf16(xw), wgu_bf[...], preferred_element_type=F32) + bgu_ref[...]
        g = jnp.minimum(gu[:, :d], SWIGLU_LIMIT)
        u = jnp.clip(gu[:, d:], -SWIGLU_LIMIT, SWIGLU_LIMIT)
        act = (u + 1.0) * (g * _sigmoid(SWIGLU_ALPHA * g))
        y = jnp.dot(act.astype(BF16), wdn_bf[...], preferred_element_type=F32) + bdn_ref[...]
        ys_ref[...] = _pack_pairs(y)

    @pl.when(j >= nu_ref[0])
    def _():
        ys_ref[...] = jnp.zeros_like(ys_ref)


def _experts(xs, tile_expert, n_used, n_valid, w_gu, b_gu, w_dn, b_dn, layer, tm=EXPERT_TM):
    m_pad, half = xs.shape
    d = 2 * half
    nl, ne = w_gu.shape[:2]
    rows = lambda j, te, nu, nv: (jnp.minimum(j, nu[0] - 1), 0)
    wsel = lambda j, te, nu, nv: (layer, te[j], 0, 0)
    grid_spec = pltpu.PrefetchScalarGridSpec(
        num_scalar_prefetch=3,
        grid=(m_pad // tm,),
        in_specs=[pl.BlockSpec((tm, half), rows),
                  pl.BlockSpec((None, None, d, 2 * d), wsel),
                  pl.BlockSpec((None, None, 1, 2 * d), wsel),
                  pl.BlockSpec((None, None, d, d), wsel),
                  pl.BlockSpec((None, None, 1, d), wsel)],
        out_specs=pl.BlockSpec((tm, half), lambda j, te, nu, nv: (j, 0)),
        scratch_shapes=[pltpu.VMEM((d, 2 * d), BF16), pltpu.VMEM((d, d), BF16)],
    )
    return pl.pallas_call(
        functools.partial(_expert_kernel, d=d),
        grid_spec=grid_spec,
        out_shape=jax.ShapeDtypeStruct((m_pad, half), I32),
        compiler_params=_params(("arbitrary",)),
        name="moe_experts",
    )(tile_expert, n_used, n_valid, xs, w_gu, b_gu.reshape(nl, ne, 1, 2 * d), w_dn, b_dn.reshape(nl, ne, 1, d))


def _combine_kernel(x_ref, y_ref, tw_ref, mod_ref, lng_ref, lnb_ref, o_ref, *, d, alpha):
    tw = tw_ref[...]
    m_hi = None
    m_lo = None
    for k in range(TOP_K):
        hi, lo = _unpack_pairs(y_ref[k])
        wk = tw[:, k:k + 1]
        m_hi = wk * hi if m_hi is None else m_hi + wk * hi
        m_lo = wk * lo if m_lo is None else m_lo + wk * lo
    m = jnp.concatenate([m_hi, m_lo], axis=1)
    gate = mod_ref[...][:, 2 * d:]
    o_ref[...] = _layer_norm(alpha * x_ref[...] + (1.0 + gate) * m, lng_ref[...], lnb_ref[...])


def _combine(x, y4, tw, mod, ln_g, ln_b, tm, alpha):
    bsz, s, d = x.shape
    nt = s // tm
    tok = pl.BlockSpec((None, tm, d), lambda b, t: (b, t, 0))
    return pl.pallas_call(
        functools.partial(_combine_kernel, d=d, alpha=alpha),
        grid=(bsz, nt),
        in_specs=[tok,
                  pl.BlockSpec((TOP_K, tm, d // 2), lambda b, t: (0, b * nt + t, 0)),
                  pl.BlockSpec((tm, LANES), lambda b, t: (b * nt + t, 0)),
                  pl.BlockSpec((None, 1, 3 * d), lambda b, t: (b, 0, 0)),
                  _full((1, d)), _full((1, d))],
        out_specs=tok,
        out_shape=jax.ShapeDtypeStruct((bsz, s, d), F32),
        compiler_params=_params(("arbitrary", "arbitrary")),
        name="moe_combine",
    )(x, y4, tw, mod, ln_g, ln_b)


def _moe(hm_p, hm_s, w_r, b_r, w_gu, b_gu, w_dn, b_dn, layer):
    half = hm_p.shape[-1]
    hm_p = hm_p.reshape(-1, half)
    hm_s = hm_s.reshape(-1, half)
    n_p, n_s = hm_p.shape[0], hm_s.shape[0]
    ne = w_r.shape[1]
    w_r = w_r.astype(BF16)
    b_r = b_r.reshape(1, ne)
    eid_p, tw_p, rank_p, cnt_p = _router(hm_p, w_r, b_r, jnp.zeros((1, ne), F32), PROMPT_TM)
    eid_s, tw_s, rank_s, cnt = _router(hm_s, w_r, b_r, cnt_p, n_s)
    tm = EXPERT_TM
    cnt = cnt.reshape(ne).astype(jnp.int32)
    gsz = ((cnt + tm - 1) // tm) * tm
    ends = jnp.cumsum(gsz)
    offs = ends - gsz
    group_off = lambda eid: jnp.sum(jnp.where(eid[:TOP_K, :, None] == jnp.arange(ne, dtype=jnp.int32), offs, 0), -1)
    pos_p = group_off(eid_p) + rank_p[:TOP_K]
    pos_s = (group_off(eid_s) + rank_s[:TOP_K]).T.reshape(-1)
    n_tiles = ((n_p + n_s) * TOP_K + ne * (tm - 1)) // tm + 1
    m_pad = n_tiles * tm
    tile_start = jnp.arange(n_tiles, dtype=jnp.int32) * tm
    tile_expert = jnp.minimum(jnp.sum((tile_start[:, None] >= ends[None, :]).astype(jnp.int32), axis=1), ne - 1)
    n_used = (ends[-1] // tm).astype(jnp.int32).reshape(1)
    n_valid = jnp.clip((offs + cnt)[tile_expert] - tile_start, 0, tm).astype(jnp.int32)
    idx_scatter = (pos_p.reshape(TOP_K, SC_WORKERS, -1, SC_CHUNK).transpose(1, 2, 0, 3)
                   .reshape(SC_WORKERS, -1, SC_CHUNK))
    xs = _sc_scatter_rows(hm_p, idx_scatter, m_pad)
    xs = _permute(hm_s, pos_s, xs, n_s)
    ys = _experts(xs, tile_expert, n_used, n_valid, w_gu, b_gu, w_dn, b_dn, layer)
    y4_p = _sc_gather_rows(ys, pos_p.reshape(-1)).reshape(TOP_K, n_p, half)
    return (y4_p, tw_p), (_unpermute(ys, pos_s, n_s, n_s), tw_s)


def _qkv_kernel(x_ref, modk_ref, modq_ref, cos_ref, sin_ref, wdkv_ref, gkv_ref, wdq_ref, gq_ref, wuq_ref,
                ckv_ref, kpe_ref, kc_ref, kp_ref, qh_ref, *, d, kvl, rope, nh, nope, q_scale):
    x = x_ref[...]
    cos = cos_ref[...]
    sin = sin_ref[...]
    modk = modk_ref[...]
    hk = (x * (1.0 + modk[:, d:2 * d]) + modk[:, :d]).astype(BF16)
    kv = jnp.dot(hk, wdkv_ref[...], preferred_element_type=F32)
    c = kv[:, :kvl]
    ckv = c * lax.rsqrt(jnp.mean(c * c, axis=-1, keepdims=True) + RMS_EPS) * gkv_ref[...]
    kp = kv[:, kvl:kvl + LANES] * cos + kv[:, kvl + LANES:kvl + 2 * LANES] * sin
    ckv_ref[...] = ckv
    kpe_ref[...] = kp[:, :rope]
    kc_ref[...] = ckv.astype(BF16)
    kp_ref[...] = kp[:, :rope].astype(BF16)

    modq = modq_ref[...]
    hq = (x * (1.0 + modq[:, d:2 * d]) + modq[:, :d]).astype(BF16)
    qd = jnp.dot(hq, wdq_ref[...], preferred_element_type=F32)
    qn = (qd * lax.rsqrt(jnp.mean(qd * qd, axis=-1, keepdims=True) + RMS_EPS) * gq_ref[...]).astype(BF16)
    q = jnp.dot(qn, wuq_ref[...], preferred_element_type=F32)
    for h in range(nh):
        qh_ref[h, :, :nope] = (q[:, h * nope:(h + 1) * nope] * q_scale).astype(BF16)
    pe0 = nh * nope
    sw0 = pe0 + nh * rope
    per = LANES // rope
    for j in range(nh // per):
        r2 = (q[:, pe0 + j * LANES:pe0 + (j + 1) * LANES] * cos
              + q[:, sw0 + j * LANES:sw0 + (j + 1) * LANES] * sin) * q_scale
        for i in range(per):
            qh_ref[j * per + i, :, nope:] = r2[:, i * rope:(i + 1) * rope].astype(BF16)


def _qkv(x, modk, modq, cos_t, sin_t, w, tm, dims):
    bsz, s, d = x.shape
    kvl, rope, nh, nope = dims["kvl"], dims["rope"], dims["nh"], dims["nope"]
    tok = lambda width: pl.BlockSpec((None, tm, width), lambda b, t: (b, t, 0))
    head = lambda width: pl.BlockSpec((None, nh, tm, width), lambda b, t: (b, 0, t, 0))
    tab = pl.BlockSpec((tm, LANES), lambda b, t: (t, 0))
    weights = [w["w_dkv"], w["g_kv"], w["w_dq"], w["g_q"], w["w_uq"]]
    kern = functools.partial(_qkv_kernel, d=d, kvl=kvl, rope=rope, nh=nh, nope=nope, q_scale=dims["q_scale"])
    return pl.pallas_call(
        kern,
        grid=(bsz, s // tm),
        in_specs=[tok(d),
                  pl.BlockSpec((None, 1, 2 * d), lambda b, t: (b, 0, 0)),
                  pl.BlockSpec((None, 1, 3 * d), lambda b, t: (b, 0, 0)),
                  tab, tab] + [_full(a.shape) for a in weights],
        out_specs=[tok(kvl), tok(rope), tok(kvl), tok(rope), head(nope + rope)],
        out_shape=[jax.ShapeDtypeStruct((bsz, s, kvl), F32), jax.ShapeDtypeStruct((bsz, s, rope), F32),
                   jax.ShapeDtypeStruct((bsz, s, kvl), BF16), jax.ShapeDtypeStruct((bsz, s, rope), BF16),
                   jax.ShapeDtypeStruct((bsz, nh, s, nope + rope), BF16)],
        compiler_params=_params(("arbitrary", "arbitrary")),
        name="mla_qkv",
    )(x, modk, modq, cos_t, sin_t, *weights)


def _kproj_kernel(kc_ref, kp_ref, wuk_ref, kh_ref, *, nh, nope):
    kc = kc_ref[...]
    kp = kp_ref[...]
    nt = (((1,), (1,)), ((), ()))
    for h in range(nh):
        kh_ref[h, :, :nope] = lax.dot_general(kc, wuk_ref[h], nt, preferred_element_type=F32).astype(BF16)
        kh_ref[h, :, nope:] = kp


def _kproj(kc, kp, w_uk, tm):
    bsz, s, kvl = kc.shape
    rope = kp.shape[-1]
    nh, nope, _ = w_uk.shape
    tok = lambda width: pl.BlockSpec((None, tm, width), lambda b, t: (b, t, 0))
    return pl.pallas_call(
        functools.partial(_kproj_kernel, nh=nh, nope=nope),
        grid=(bsz, s // tm),
        in_specs=[tok(kvl), tok(rope), _full(w_uk.shape)],
        out_specs=pl.BlockSpec((None, nh, tm, nope + rope), lambda b, t: (b, 0, t, 0)),
        out_shape=jax.ShapeDtypeStruct((bsz, nh, s, nope + rope), BF16),
        compiler_params=_params(("arbitrary", "arbitrary")),
        name="mla_kproj",
    )(kc, kp, w_uk)


def _attn_kernel(qi_ref, kj_ref, last_ref, edge_ref, qh_ref, kh_ref, kc_ref, o_ref, m_s, l_s, acc_s,
                 *, nh, tq, tk, kvl, causal, valid):
    s_id = pl.program_id(1)
    qi = qi_ref[s_id]
    kj = kj_ref[s_id]
    nt = (((1,), (1,)), ((), ()))

    @pl.when(kj == 0)
    def _():
        m_s[...] = jnp.full_like(m_s, -jnp.inf)
        l_s[...] = jnp.zeros_like(l_s)
        acc_s[...] = jnp.zeros_like(acc_s)

    def sweep(masked):
        kc = kc_ref[...]
        if masked:
            kidx = kj * tk + lax.broadcasted_iota(jnp.int32, (tq, tk), 1)
            if causal:
                qidx = qi * tq + lax.broadcasted_iota(jnp.int32, (tq, tk), 0)
                visible = (kidx // CHUNK) <= (qidx // CHUNK)
            else:
                visible = kidx < valid
        def scores(h):
            s = lax.dot_general(qh_ref[h], kh_ref[h], nt, preferred_element_type=F32)
            return jnp.where(visible, s, -jnp.inf) if masked else s

        s_next = scores(0)
        for h in range(nh):
            s = s_next
            if h + 1 < nh:
                s_next = scores(h + 1)
            slabs = [s[:, j * LANES:(j + 1) * LANES] for j in range(tk // LANES)]
            mx = slabs[0]
            for sj in slabs[1:]:
                mx = jnp.maximum(mx, sj)
            m_prev = m_s[h]
            m_new = jnp.maximum(m_prev, jnp.max(mx, axis=1, keepdims=True))
            scale = jnp.exp2(m_prev - m_new)
            ps = []
            rsum = None
            for sj in slabs:
                pj = jnp.exp2(sj - m_new)
                ps.append(pj.astype(BF16))
                rsum = pj if rsum is None else rsum + pj
            p = jnp.concatenate(ps, axis=1)
            l_s[h] = scale * l_s[h] + jnp.sum(rsum, axis=1, keepdims=True)
            pv = jnp.dot(p, kc, preferred_element_type=F32)
            acc_s[h] = acc_s[h] * jnp.concatenate([scale] * (kvl // LANES), axis=1) + pv
            m_s[h] = m_new

    edge = edge_ref[s_id]

    @pl.when(edge == 1)
    def _():
        sweep(True)

    @pl.when(edge == 0)
    def _():
        sweep(False)

    @pl.when(last_ref[s_id] == 1)
    def _():
        for h in range(nh):
            inv = jnp.concatenate([l_s[h]] * (kvl // LANES), axis=1)
            o_ref[h] = (acc_s[h] / inv).astype(BF16)


def _attention(qh, kh, kc, tq, tk, causal, valid):
    bsz, nh, s, qk = qh.shape
    skv, kvl = kc.shape[1], kc.shape[2]
    qi, kj, last, edge = [], [], [], []
    for i in range(s // tq):
        hi = ((i * tq + tq - 1) // tk) if causal else (skv // tk - 1)
        for j in range(hi + 1):
            qi.append(i)
            kj.append(j)
            last.append(1 if j == hi else 0)
            if causal:
                edge.append(1 if ((j + 1) * tk - 1) // CHUNK > (i * tq) // CHUNK else 0)
            else:
                edge.append(1 if (j + 1) * tk > valid else 0)
    tabs = [jnp.asarray(np.array(a, np.int32)) for a in (qi, kj, last, edge)]
    qspec = lambda width: pl.BlockSpec((None, nh, tq, width), lambda b, t, qi, kj, last, edge: (b, 0, qi[t], 0))
    khspec = pl.BlockSpec((None, nh, tk, qk), lambda b, t, qi, kj, last, edge: (b, 0, kj[t], 0))
    kcspec = pl.BlockSpec((None, tk, kvl), lambda b, t, qi, kj, last, edge: (b, kj[t], 0))
    grid_spec = pltpu.PrefetchScalarGridSpec(
        num_scalar_prefetch=4,
        grid=(bsz, len(qi)),
        in_specs=[qspec(qk), khspec, kcspec],
        out_specs=qspec(kvl),
        scratch_shapes=[pltpu.VMEM((nh, tq, LANES), F32), pltpu.VMEM((nh, tq, LANES), F32),
                        pltpu.VMEM((nh, tq, kvl), F32)],
    )
    kern = functools.partial(_attn_kernel, nh=nh, tq=tq, tk=tk, kvl=kvl, causal=causal, valid=valid)
    return pl.pallas_call(
        kern,
        grid_spec=grid_spec,
        out_shape=jax.ShapeDtypeStruct((bsz, nh, s, kvl), BF16),
        compiler_params=_params(("arbitrary", "arbitrary")),
        name="mla_attention",
    )(*tabs, qh, kh, kc)


def _attn_out_kernel(o_ref, x_ref, mod_ref, mod2_ref, wuv_ref, wo_ref, lng_ref, lnb_ref, x3_ref, hm_ref,
                     *, d, nh, alpha):
    parts = [jnp.dot(o_ref[h], wuv_ref[h], preferred_element_type=F32) for h in range(nh)]
    o = jnp.concatenate(parts, axis=1).astype(BF16)
    m = jnp.dot(o, wo_ref[...], preferred_element_type=F32)
    gate = mod_ref[...][:, 2 * d:]
    x3 = _layer_norm(alpha * x_ref[...] + (1.0 + gate) * m, lng_ref[...], lnb_ref[...])
    x3_ref[...] = x3
    mod2 = mod2_ref[...]
    hm_ref[...] = _pack_pairs(x3 * (1.0 + mod2[:, d:2 * d]) + mod2[:, :d])


def _attn_out(o_lat, x, mod, mod2, w_uv, w_o, ln_g, ln_b, tm, alpha):
    bsz, s, d = x.shape
    nh, kvl = o_lat.shape[1], o_lat.shape[3]
    tok = lambda width: pl.BlockSpec((None, tm, width), lambda b, t: (b, t, 0))
    modspec = pl.BlockSpec((None, 1, 3 * d), lambda b, t: (b, 0, 0))
    return pl.pallas_call(
        functools.partial(_attn_out_kernel, d=d, nh=nh, alpha=alpha),
        grid=(bsz, s // tm),
        in_specs=[pl.BlockSpec((None, nh, tm, kvl), lambda b, t: (b, 0, t, 0)), tok(d), modspec, modspec,
                  _full(w_uv.shape), _full(w_o.shape), _full((1, d)), _full((1, d))],
        out_specs=[tok(d), tok(d // 2)],
        out_shape=[jax.ShapeDtypeStruct((bsz, s, d), F32), jax.ShapeDtypeStruct((bsz, s, d // 2), I32)],
        compiler_params=_params(("arbitrary", "arbitrary")),
        name="mla_out",
    )(o_lat, x, mod, mod2, w_uv, w_o, ln_g, ln_b)


def _rope_tables(pos, rope):
    half = rope // 2
    inv = ROPE_THETA ** (-2.0 * jnp.arange(half, dtype=F32) / rope)
    ang = pos.astype(F32)[:, None] * inv[None, :]
    cos, sin = jnp.cos(ang), jnp.sin(ang)
    rep = LANES // rope
    return (jnp.concatenate([cos, cos] * rep, axis=1), jnp.concatenate([-sin, sin] * rep, axis=1))


def kernel(x_prompt, x_sample, state_conv, state_rglru, cache_ckv, cache_kpe, c_prompt, c_sample, w_ada, b_ada, ln_g, ln_b, lru_w_in, lru_b_in, lru_conv_w, lru_conv_b, lru_w_a, lru_b_a, lru_w_i, lru_b_i, lru_lambda, lru_w_out, lru_b_out, kv_w_ada, kv_b_ada, mla_w_dkv, mla_g_kv, mla_w_uk, mla_w_uv, mla_w_dq, mla_g_q, mla_w_uq, mla_w_o, moe_w_r, moe_b_r, moe_w_gu, moe_b_gu, moe_w_dn, moe_b_dn):
    bp, sp, d = x_prompt.shape
    bs, ss, _ = x_sample.shape
    depth = w_ada.shape[0]
    alpha = float((2.0 * depth) ** 0.25)
    nh, nope, kvl = mla_w_uk.shape
    rope = cache_kpe.shape[-1]
    past = cache_ckv.shape[1]
    hist = state_conv.shape[2]
    dims = dict(kvl=kvl, rope=rope, nh=nh, nope=nope, q_scale=float((nope + rope) ** -0.5 * LOG2E))
    row = lambda v: v.reshape(1, -1)

    nrow = bp + bs
    rpad = -nrow % (2 * SUBLANES)
    c_rows = jnp.concatenate([c_prompt, c_sample, jnp.zeros((rpad, d), F32)], axis=0)
    mods = _ada(c_rows, w_ada.reshape(depth * 2, d, 3 * d), b_ada.reshape(depth * 2, 1, 3 * d))
    modkv = _ada(c_rows, kv_w_ada.reshape(1, d, 2 * d), kv_b_ada.reshape(1, 1, 2 * d))[0]
    mod_p = lambda i: mods[i, :bp].reshape(bp, 1, 3 * d)
    mod_s = lambda i: mods[i, bp:nrow].reshape(bs, 1, 3 * d)

    l0w = dict(w_in=lru_w_in[0].astype(BF16), b_in=row(lru_b_in[0]), conv_w=lru_conv_w[0], conv_b=row(lru_conv_b[0]),
               w_a=lru_w_a[0].astype(BF16), b_a=row(lru_b_a[0]), w_i=lru_w_i[0].astype(BF16), b_i=row(lru_b_i[0]),
               lam=row(lru_lambda[0]), w_out=lru_w_out[0].astype(BF16), b_out=row(lru_b_out[0]),
               ln_g=row(ln_g[0, 0]), ln_b=row(ln_b[0, 0]))
    x1p, hmp, conv_p, h_p = _l0_mixer(x_prompt, mod_p(0), mod_p(1), jnp.zeros((bp, hist, d), F32),
                                      jnp.zeros((bp, 1, d), F32), l0w, L0_TB, alpha)
    x1s, hms, conv_s, h_s = _l0_mixer(x_sample, mod_s(0), mod_s(1), state_conv[0], state_rglru[0].reshape(bs, 1, d),
                                      l0w, ss, alpha)

    (y4p, twp), (y4s, tws) = _moe(hmp, hms, moe_w_r[0], moe_b_r[0], moe_w_gu, moe_b_gu, moe_w_dn, moe_b_dn, 0)
    x2p = _combine(x1p, y4p, twp, mod_p(1), row(ln_g[0, 1]), row(ln_b[0, 1]), PROMPT_TM, alpha)
    x2s = _combine(x1s, y4s, tws, mod_s(1), row(ln_g[0, 1]), row(ln_b[0, 1]), ss, alpha)

    pe_cols = jnp.arange(rope)
    sw_cols = jnp.concatenate([pe_cols[rope // 2:], pe_cols[:rope // 2]])
    zpad = jnp.zeros((d, LANES - rope), F32)
    w_dkv_ext = jnp.concatenate([mla_w_dkv[:, :kvl], mla_w_dkv[:, kvl:], zpad,
                                 mla_w_dkv[:, kvl:][:, sw_cols], zpad], axis=1).astype(BF16)
    wq = mla_w_uq[0].reshape(-1, nh, nope + rope)
    w_uq_ext = jnp.concatenate([wq[:, :, :nope].reshape(-1, nh * nope), wq[:, :, nope:].reshape(-1, nh * rope),
                                wq[:, :, nope:][:, :, sw_cols].reshape(-1, nh * rope)], axis=1).astype(BF16)
    qw = dict(w_dkv=w_dkv_ext, g_kv=row(mla_g_kv), w_dq=mla_w_dq[0].astype(BF16), g_q=row(mla_g_q[0]),
              w_uq=w_uq_ext)
    w_uk = mla_w_uk.astype(BF16)
    cos_p, sin_p = _rope_tables(jnp.arange(sp), rope)
    cos_s, sin_s = _rope_tables(past + jnp.arange(ss), rope)
    modkv_p = modkv[:bp].reshape(bp, 1, 2 * d)
    modkv_s = modkv[bp:nrow].reshape(bs, 1, 2 * d)
    ckv_p, kpe_p, kc_p, kp_p, qh_p = _qkv(x2p, modkv_p, mod_p(2), cos_p, sin_p, qw, PROMPT_TM, dims)
    ckv_s, kpe_s, kc_s, kp_s, qh_s = _qkv(x2s, modkv_s, mod_s(2), cos_s, sin_s, qw, ss, dims)

    o_p = _attention(qh_p, _kproj(kc_p, kp_p, w_uk, PROMPT_TM), kc_p, ATT_TQ, ATT_TK, True, sp)
    skv = past + ss
    kpad = -skv % LANES
    kc_all = jnp.concatenate([cache_ckv.astype(BF16), kc_s, jnp.zeros((bs, kpad, kvl), BF16)], axis=1)
    kp_all = jnp.concatenate([cache_kpe.astype(BF16), kp_s, jnp.zeros((bs, kpad, rope), BF16)], axis=1)
    o_s = _attention(qh_s, _kproj(kc_all, kp_all, w_uk, skv + kpad), kc_all, ss, skv + kpad, False, skv)
    w_uv = mla_w_uv.astype(BF16)
    w_o = mla_w_o[0].astype(BF16)
    x3p, hm3p = _attn_out(o_p, x2p, mod_p(2), mod_p(3), w_uv, w_o, row(ln_g[1, 0]), row(ln_b[1, 0]), PROMPT_TM, alpha)
    x3s, hm3s = _attn_out(o_s, x2s, mod_s(2), mod_s(3), w_uv, w_o, row(ln_g[1, 0]), row(ln_b[1, 0]), ss, alpha)

    (y4p, twp), (y4s, tws) = _moe(hm3p, hm3s, moe_w_r[1], moe_b_r[1], moe_w_gu, moe_b_gu, moe_w_dn, moe_b_dn, 1)
    y_p = _combine(x3p, y4p, twp, mod_p(3), row(ln_g[1, 1]), row(ln_b[1, 1]), PROMPT_TM, alpha)
    y_s = _combine(x3s, y4s, tws, mod_s(3), row(ln_g[1, 1]), row(ln_b[1, 1]), ss, alpha)

    return (y_p, y_s, conv_p[None], h_p.reshape(1, bp, d), ckv_p, kpe_p,
            conv_s[None], h_s.reshape(1, bs, d), ckv_s, kpe_s)
```

```python
import functools

import numpy as np
import jax
import jax.numpy as jnp
from jax import lax
from jax.experimental import pallas as pl
from jax.experimental.pallas import tpu as pltpu
from jax.experimental.pallas import tpu_sc as plsc

F32 = jnp.float32
BF16 = jnp.bfloat16
I32 = jnp.int32

CHUNK = 64
N_LRU_BLOCKS = 8
LRU_C = 8.0
ROPE_THETA = 10000.0
TOP_K = 4
SWIGLU_ALPHA = 1.702
SWIGLU_LIMIT = 7.0
LN_EPS = 1e-5
RMS_EPS = 1e-6
LOG2E = 1.4426950408889634

LANES = 128
SUBLANES = 8
VMEM_LIMIT = 56 * 1024 * 1024

PROMPT_TM = 512
L0_TB = 512
EXPERT_TM = 512
SC_CORES = 2
SC_WORKERS = SC_CORES * 16
SC_CHUNK = 64
ATT_TQ = 512
ATT_TK = 1024


def _sigmoid(x):
    return 0.5 * (jnp.tanh(0.5 * x) + 1.0)


def _layer_norm(v, g, b):
    mu = jnp.mean(v, axis=-1, keepdims=True)
    d = v - mu
    var = jnp.mean(d * d, axis=-1, keepdims=True)
    return d * lax.rsqrt(var + LN_EPS) * g + b


def _pack_pairs(x):
    w = x.shape[1] // 2
    hi = lax.bitcast_convert_type(x[:, :w].astype(BF16).astype(F32), I32)
    lo = lax.bitcast_convert_type(x[:, w:].astype(BF16).astype(F32), I32)
    return hi | lax.shift_right_logical(lo, jnp.int32(16))


def _unpack_pairs(words):
    hi = lax.bitcast_convert_type(words & jnp.int32(-65536), F32)
    lo = lax.bitcast_convert_type(words << 16, F32)
    return hi, lo


def _unpack_bf16(words):
    hi, lo = _unpack_pairs(words)
    return jnp.concatenate([hi.astype(BF16), lo.astype(BF16)], axis=1)


def _full(shape):
    nd = len(shape)
    return pl.BlockSpec(shape, lambda *_: (0,) * nd)


def _params(sem, vmem=VMEM_LIMIT):
    return pltpu.CompilerParams(dimension_semantics=sem, vmem_limit_bytes=vmem)


def _ada_kernel(c_ref, w_ref, b_ref, o_ref):
    c = c_ref[...]
    s = (c * _sigmoid(c)).astype(BF16)
    o_ref[...] = jnp.dot(s, w_ref[...].astype(BF16), preferred_element_type=F32) + b_ref[...]


def _ada(c_rows, w, b, tn=1024):
    g, d, n = w.shape
    r = c_rows.shape[0]
    return pl.pallas_call(
        _ada_kernel,
        grid=(g, n // tn),
        in_specs=[_full((r, d)),
                  pl.BlockSpec((None, d, tn), lambda i, j: (i, 0, j)),
                  pl.BlockSpec((None, 1, tn), lambda i, j: (i, 0, j))],
        out_specs=pl.BlockSpec((None, r, tn), lambda i, j: (i, 0, j)),
        out_shape=jax.ShapeDtypeStruct((g, r, n), F32),
        compiler_params=_params(("arbitrary", "arbitrary")),
        name="ada_mod",
    )(c_rows, w, b)


def _l0_kernel(x_ref, mod_ref, mod2_ref, conv0_ref, h0_ref, win_ref, bin_ref, cw_ref, cb_ref,
               wa_ref, ba_ref, wi_ref, bi_ref, lam_ref, wout_ref, bout_ref, lng_ref, lnb_ref,
               x1_ref, hm_ref, convn_ref, hlast_ref,
               ubuf, a_s, b_s, h_s, *, tb, d, alpha):
    t = pl.program_id(1)
    pad = SUBLANES
    hist = convn_ref.shape[0]

    @pl.when(t == 0)
    def _():
        ubuf[pad - hist:pad, :] = conv0_ref[...]
        h_s[...] = h0_ref[...]

    x = x_ref[...]
    mod = mod_ref[...]
    shift, scale, gate = mod[:, :d], mod[:, d:2 * d], mod[:, 2 * d:]
    h = (x * (1.0 + scale) + shift).astype(BF16)
    proj = jnp.dot(h, win_ref[...], preferred_element_type=F32) + bin_ref[...]
    gate_b = proj[:, :d]
    ubuf[pad:pad + tb, :] = proj[:, d:]

    cw = cw_ref[...]
    u = cb_ref[...]
    for k in range(hist + 1):
        u = u + ubuf[pad - hist + k:pad - hist + k + tb, :] * cw[k:k + 1, :]
    ubuf[pad - hist:pad, :] = ubuf[pad + tb - hist:pad + tb, :]

    ub = u.astype(BF16)
    blk = d // N_LRU_BLOCKS
    lam = lam_ref[...]
    neg = -lam
    softplus = jnp.maximum(neg, 0.0) + jnp.log1p(jnp.exp(-jnp.abs(neg)))
    for n in range(N_LRU_BLOCKS):
        sl = slice(n * blk, (n + 1) * blk)
        un = ub[:, sl]
        r = _sigmoid(jnp.dot(un, wa_ref[n], preferred_element_type=F32) + ba_ref[:, sl])
        ig = _sigmoid(jnp.dot(un, wi_ref[n], preferred_element_type=F32) + bi_ref[:, sl])
        log_a = -LRU_C * r * softplus[:, sl]
        a = jnp.exp(log_a)
        one_m_a2 = -jnp.tanh(log_a) * (a * a + 1.0)
        a_s[:, sl] = a
        b_s[:, sl] = jnp.sqrt(one_m_a2) * (ig * u[:, sl])

    row = lax.broadcasted_iota(jnp.int32, (SUBLANES, d), 0)

    def group(g, hprev):
        r0 = pl.multiple_of(g * SUBLANES, SUBLANES)
        a = a_s[pl.ds(r0, SUBLANES), :]
        b = b_s[pl.ds(r0, SUBLANES), :]
        for sh in (1, 2, 4):
            keep = row >= sh
            a_sh = pltpu.roll(a, sh, axis=0)
            b_sh = pltpu.roll(b, sh, axis=0)
            b = jnp.where(keep, a * b_sh + b, b)
            a = jnp.where(keep, a * a_sh, a)
        hh = a * hprev + b
        b_s[pl.ds(r0, SUBLANES), :] = hh
        return hh[SUBLANES - 1:SUBLANES, :]

    h_s[...] = lax.fori_loop(0, tb // SUBLANES, group, h_s[...])

    y = b_s[...]
    gl = 0.5 * gate_b * (1.0 + jnp.tanh(0.7978845608028654 * (gate_b + 0.044715 * gate_b * gate_b * gate_b)))
    out = jnp.dot((gl * y).astype(BF16), wout_ref[...], preferred_element_type=F32) + bout_ref[...]
    x1 = _layer_norm(alpha * x + (1.0 + gate) * out, lng_ref[...], lnb_ref[...])
    x1_ref[...] = x1
    mod2 = mod2_ref[...]
    hm_ref[...] = _pack_pairs(x1 * (1.0 + mod2[:, d:2 * d]) + mod2[:, :d])

    @pl.when(t == pl.num_programs(1) - 1)
    def _():
        convn_ref[...] = ubuf[pad - hist:pad, :]
        hlast_ref[...] = h_s[...]


def _l0_mixer(x, mod, mod2, conv0, h0, w, tb, alpha):
    bsz, s, d = x.shape
    hist = conv0.shape[1]
    tok = lambda width: pl.BlockSpec((None, tb, width), lambda b, t: (b, t, 0))
    per_b = lambda rows, width: pl.BlockSpec((None, rows, width), lambda b, t: (b, 0, 0))
    kern = functools.partial(_l0_kernel, tb=tb, d=d, alpha=alpha)
    weights = [w["w_in"], w["b_in"], w["conv_w"], w["conv_b"], w["w_a"], w["b_a"], w["w_i"], w["b_i"],
               w["lam"], w["w_out"], w["b_out"], w["ln_g"], w["ln_b"]]
    return pl.pallas_call(
        kern,
        grid=(bsz, s // tb),
        in_specs=[tok(d), per_b(1, 3 * d), per_b(1, 3 * d), per_b(hist, d), per_b(1, d)]
                 + [_full(a.shape) for a in weights],
        out_specs=[tok(d), tok(d // 2), per_b(hist, d), per_b(1, d)],
        out_shape=[jax.ShapeDtypeStruct((bsz, s, d), F32), jax.ShapeDtypeStruct((bsz, s, d // 2), I32),
                   jax.ShapeDtypeStruct((bsz, hist, d), F32), jax.ShapeDtypeStruct((bsz, 1, d), F32)],
        scratch_shapes=[pltpu.VMEM((tb + SUBLANES, d), F32), pltpu.VMEM((tb, d), F32),
                        pltpu.VMEM((tb, d), F32), pltpu.VMEM((1, d), F32)],
        compiler_params=_params(("arbitrary", "arbitrary")),
        name="l0_mixer",
    )(x, mod, mod2, conv0, h0, *weights)


def _router_kernel(hm_ref, wr_ref, br_ref, base0_ref, eid_ref, tw_ref, rank_ref, cnt_ref, base, *, tm, ne):
    i = pl.program_id(0)

    @pl.when(i == 0)
    def _():
        base[...] = base0_ref[...]

    logits = jnp.dot(_unpack_bf16(hm_ref[...]), wr_ref[...], preferred_element_type=F32) + br_ref[...]
    col = lax.broadcasted_iota(jnp.int32, (tm, ne), 1).astype(F32)
    l = logits
    vals, idxs = [], []
    for _ in range(TOP_K):
        m = jnp.max(l, axis=-1, keepdims=True)
        idx = jnp.min(jnp.where(l == m, col, float(ne)), axis=-1, keepdims=True)
        vals.append(m)
        idxs.append(idx)
        l = jnp.where(col == idx, -jnp.inf, l)
    es = [jnp.exp(v - vals[0]) for v in vals]
    tot = es[0] + es[1] + es[2] + es[3]

    ri = lax.broadcasted_iota(jnp.int32, (tm, tm), 0)
    ci = lax.broadcasted_iota(jnp.int32, (tm, tm), 1)
    tri = (ci < ri).astype(BF16)
    run = base[...]
    lane = lax.broadcasted_iota(jnp.int32, (tm, LANES), 1)
    eid_o = jnp.zeros((tm, LANES), F32)
    tw_o = jnp.zeros((tm, LANES), F32)
    rk_o = jnp.zeros((tm, LANES), F32)
    for k in range(TOP_K):
        oh = col == idxs[k]
        ohf = oh.astype(F32)
        before = jnp.dot(tri, ohf.astype(BF16), preferred_element_type=F32)
        rank = jnp.sum(jnp.where(oh, before + run, 0.0), axis=-1, keepdims=True)
        run = run + jnp.sum(ohf, axis=0, keepdims=True)
        sel = lane == k
        eid_o = jnp.where(sel, idxs[k], eid_o)
        tw_o = jnp.where(sel, es[k] / tot, tw_o)
        rk_o = jnp.where(sel, rank, rk_o)
    base[...] = run
    eid_ref[...] = eid_o.T[:SUBLANES, :].astype(jnp.int32)
    tw_ref[...] = tw_o
    rank_ref[...] = rk_o.T[:SUBLANES, :].astype(jnp.int32)
    cnt_ref[...] = run


def _router(hm, w_r, b_r, base0, tm):
    n, half = hm.shape
    ne = w_r.shape[1]
    kern = functools.partial(_router_kernel, tm=tm, ne=ne)
    tokrow = pl.BlockSpec((tm, LANES), lambda i: (i, 0))
    tokcol = pl.BlockSpec((SUBLANES, tm), lambda i: (0, i))
    return pl.pallas_call(
        kern,
        grid=(n // tm,),
        in_specs=[pl.BlockSpec((tm, half), lambda i: (i, 0)), _full((2 * half, ne)), _full((1, ne)), _full((1, ne))],
        out_specs=[tokcol, tokrow, tokcol, _full((1, ne))],
        out_shape=[jax.ShapeDtypeStruct((SUBLANES, n), jnp.int32), jax.ShapeDtypeStruct((n, LANES), F32),
                   jax.ShapeDtypeStruct((SUBLANES, n), jnp.int32), jax.ShapeDtypeStruct((1, ne), F32)],
        scratch_shapes=[pltpu.VMEM((1, ne), F32)],
        compiler_params=_params(("arbitrary",)),
        name="moe_router",
    )(hm, w_r, b_r, base0)


def _permute_kernel(pos_ref, hm_ref, init_ref, out_ref, sem, *, tm):
    del init_ref

    def copy(t, k):
        return pltpu.make_async_copy(hm_ref.at[pl.ds(t, 1)], out_ref.at[pl.ds(pos_ref[t * TOP_K + k], 1)], sem)

    def issue(t, c):
        for k in range(TOP_K):
            copy(t, k).start()
        return c

    def drain(t, c):
        for k in range(TOP_K):
            copy(t, k).wait()
        return c

    lax.fori_loop(0, tm, issue, 0)
    lax.fori_loop(0, tm, drain, 0)


def _permute(hm, pos_flat, dest, tm):
    n, w = hm.shape
    return pl.pallas_call(
        functools.partial(_permute_kernel, tm=tm),
        grid=(n // tm,),
        in_specs=[pl.BlockSpec((tm * TOP_K,), lambda i: (i,), memory_space=pltpu.SMEM),
                  pl.BlockSpec((tm, w), lambda i: (i, 0)),
                  pl.BlockSpec(memory_space=pl.ANY)],
        out_specs=pl.BlockSpec(memory_space=pl.ANY),
        out_shape=jax.ShapeDtypeStruct(dest.shape, dest.dtype),
        scratch_shapes=[pltpu.SemaphoreType.DMA],
        input_output_aliases={2: 0},
        compiler_params=_params(("arbitrary",)),
        name="moe_permute",
    )(pos_flat, hm, dest)


def _unpermute_kernel(pos_ref, ys_ref, out_ref, sem, *, tm):
    def copy(t, k):
        return pltpu.make_async_copy(ys_ref.at[pl.ds(pos_ref[t * TOP_K + k], 1)], out_ref.at[k, pl.ds(t, 1)], sem)

    def issue(t, c):
        for k in range(TOP_K):
            copy(t, k).start()
        return c

    def drain(t, c):
        for k in range(TOP_K):
            copy(t, k).wait()
        return c

    lax.fori_loop(0, tm, issue, 0)
    lax.fori_loop(0, tm, drain, 0)


def _unpermute(ys, pos_flat, n, tm):
    _, w = ys.shape
    return pl.pallas_call(
        functools.partial(_unpermute_kernel, tm=tm),
        grid=(n // tm,),
        in_specs=[pl.BlockSpec((tm * TOP_K,), lambda i: (i,), memory_space=pltpu.SMEM),
                  pl.BlockSpec(memory_space=pl.ANY)],
        out_specs=pl.BlockSpec((TOP_K, tm, w), lambda i: (0, i, 0)),
        out_shape=jax.ShapeDtypeStruct((TOP_K, n, w), ys.dtype),
        scratch_shapes=[pltpu.SemaphoreType.DMA],
        compiler_params=_params(("arbitrary",)),
        name="moe_unpermute",
    )(pos_flat, ys)


def _sc_mesh():
    return plsc.VectorSubcoreMesh(core_axis_name="c", subcore_axis_name="s")


def _sc_worker():
    return lax.axis_index("s") * SC_CORES + lax.axis_index("c")


def _sc_scatter_rows(rows, idx, m_pad):
    n, w = rows.shape
    per_w = n // SC_WORKERS
    n_chunks = per_w // SC_CHUNK

    def body(rows_hbm, idx_hbm, out_hbm, idx_v, buf, g0, g1, o0, o1):
        gsem, osem = (g0, g1), (o0, o1)
        wid = _sc_worker()
        base = wid * per_w
        pltpu.sync_copy(idx_hbm.at[wid], idx_v)

        def get(c, slot):
            return pltpu.make_async_copy(rows_hbm.at[pl.ds(base + c * SC_CHUNK, SC_CHUNK)], buf.at[slot], gsem[slot])

        def put(c, k, slot):
            return pltpu.make_async_copy(buf.at[slot], out_hbm.at[idx_v.at[c * TOP_K + k]], osem[slot])

        get(0, 0).start()
        for c in range(n_chunks):
            slot = c % 2
            get(c, slot).wait()
            if c + 1 < n_chunks:
                if c >= 1:
                    for k in range(TOP_K):
                        put(c - 1, k, 1 - slot).wait()
                get(c + 1, 1 - slot).start()
            for k in range(TOP_K):
                put(c, k, slot).start()
        for c in range(max(n_chunks - 2, 0), n_chunks):
            for k in range(TOP_K):
                put(c, k, c % 2).wait()

    return pl.kernel(
        body, mesh=_sc_mesh(),
        out_type=jax.ShapeDtypeStruct((m_pad, w), rows.dtype),
        scratch_types=[pltpu.VMEM((n_chunks * TOP_K, SC_CHUNK), jnp.int32), pltpu.VMEM((2, SC_CHUNK, w), rows.dtype)]
                      + [pltpu.SemaphoreType.DMA] * 4,
        name="moe_sc_scatter",
    )(rows, idx)


def _sc_gather_rows(table, idx):
    b = idx.shape[0]
    w = table.shape[1]
    per_w = b // SC_WORKERS
    n_chunks = per_w // SC_CHUNK

    def body(table_hbm, idx_hbm, out_hbm, idx_v, buf, g0, g1, o0, o1):
        gsem, osem = (g0, g1), (o0, o1)
        base = _sc_worker() * per_w
        pltpu.sync_copy(idx_hbm.at[pl.ds(base, per_w)], idx_v)

        def get(c, slot):
            return pltpu.make_async_copy(table_hbm.at[idx_v.at[pl.ds(c * SC_CHUNK, SC_CHUNK)]], buf.at[slot],
                                         gsem[slot])

        def put(c, slot):
            return pltpu.make_async_copy(buf.at[slot], out_hbm.at[pl.ds(base + c * SC_CHUNK, SC_CHUNK)], osem[slot])

        get(0, 0).start()
        for c in range(n_chunks):
            slot = c % 2
            get(c, slot).wait()
            if c + 1 < n_chunks:
                if c >= 1:
                    put(c - 1, 1 - slot).wait()
                get(c + 1, 1 - slot).start()
            put(c, slot).start()
        for c in range(max(n_chunks - 2, 0), n_chunks):
            put(c, c % 2).wait()

    return pl.kernel(
        body, mesh=_sc_mesh(),
        out_type=jax.ShapeDtypeStruct((b, w), table.dtype),
        scratch_types=[pltpu.VMEM((per_w,), jnp.int32), pltpu.VMEM((2, SC_CHUNK, w), table.dtype)]
                      + [pltpu.SemaphoreType.DMA] * 4,
        name="moe_sc_gather",
    )(table, idx)


def _expert_kernel(te_ref, nu_ref, nv_ref, xs_ref, wgu_ref, bgu_ref, wdn_ref, bdn_ref, ys_ref, wgu_bf, wdn_bf, *, d):
    j = pl.program_id(0)
    e = te_ref[j]
    prev = te_ref[jnp.maximum(j - 1, 0)]

    @pl.when(jnp.logical_or(j == 0, e != prev))
    def _():
        rows = 128

        def cast(c, carry):
            r0 = pl.multiple_of(c * rows, rows)
            wgu_bf[pl.ds(r0, rows), :] = wgu_ref[pl.ds(r0, rows), :].astype(BF16)
            wdn_bf[pl.ds(r0, rows), :] = wdn_ref[pl.ds(r0, rows), :].astype(BF16)
            return carry

        lax.fori_loop(0, d // rows, cast, 0)

    @pl.when(j < nu_ref[0])
    def _():
        rowi = lax.broadcasted_iota(jnp.int32, xs_ref.shape, 0)
        xw = jnp.where(rowi < nv_ref[j], xs_ref[...], jnp.int32(0))
        gu = jnp.dot(_unpack_bf16(xw), wgu_bf[...], preferred_element_type=F32) + bgu_ref[...]
        g = jnp.minimum(gu[:, :d], SWIGLU_LIMIT)
        u = jnp.clip(gu[:, d:], -SWIGLU_LIMIT, SWIGLU_LIMIT)
        act = (u + 1.0) * (g * _sigmoid(SWIGLU_ALPHA * g))
        y = jnp.dot(act.astype(BF16), wdn_bf[...], preferred_element_type=F32) + bdn_ref[...]
        ys_ref[...] = _pack_pairs(y)

    @pl.when(j >= nu_ref[0])
    def _():
        ys_ref[...] = jnp.zeros_like(ys_ref)


def _experts(xs, tile_expert, n_used, n_valid, w_gu, b_gu, w_dn, b_dn, layer, tm=EXPERT_TM):
    m_pad, half = xs.shape
    d = 2 * half
    nl, ne = w_gu.shape[:2]
    rows = lambda j, te, nu, nv: (jnp.minimum(j, nu[0] - 1), 0)
    wsel = lambda j, te, nu, nv: (layer, te[j], 0, 0)
    grid_spec = pltpu.PrefetchScalarGridSpec(
        num_scalar_prefetch=3,
        grid=(m_pad // tm,),
        in_specs=[pl.BlockSpec((tm, half), rows),
                  pl.BlockSpec((None, None, d, 2 * d), wsel),
                  pl.BlockSpec((None, None, 1, 2 * d), wsel),
                  pl.BlockSpec((None, None, d, d), wsel),
                  pl.BlockSpec((None, None, 1, d), wsel)],
        out_specs=pl.BlockSpec((tm, half), lambda j, te, nu, nv: (j, 0)),
        scratch_shapes=[pltpu.VMEM((d, 2 * d), BF16), pltpu.VMEM((d, d), BF16)],
    )
    return pl.pallas_call(
        functools.partial(_expert_kernel, d=d),
        grid_spec=grid_spec,
        out_shape=jax.ShapeDtypeStruct((m_pad, half), I32),
        compiler_params=_params(("arbitrary",)),
        name="moe_experts",
    )(tile_expert, n_used, n_valid, xs, w_gu, b_gu.reshape(nl, ne, 1, 2 * d), w_dn, b_dn.reshape(nl, ne, 1, d))


def _combine_kernel(x_ref, y_ref, tw_ref, mod_ref, lng_ref, lnb_ref, o_ref, *, d, alpha):
    tw = tw_ref[...]
    m_hi = None
    m_lo = None
    for k in range(TOP_K):
        hi, lo = _unpack_pairs(y_ref[k])
        wk = tw[:, k:k + 1]
        m_hi = wk * hi if m_hi is None else m_hi + wk * hi
        m_lo = wk * lo if m_lo is None else m_lo + wk * lo
    m = jnp.concatenate([m_hi, m_lo], axis=1)
    gate = mod_ref[...][:, 2 * d:]
    o_ref[...] = _layer_norm(alpha * x_ref[...] + (1.0 + gate) * m, lng_ref[...], lnb_ref[...])


def _combine(x, y4, tw, mod, ln_g, ln_b, tm, alpha):
    bsz, s, d = x.shape
    nt = s // tm
    tok = pl.BlockSpec((None, tm, d), lambda b, t: (b, t, 0))
    return pl.pallas_call(
        functools.partial(_combine_kernel, d=d, alpha=alpha),
        grid=(bsz, nt),
        in_specs=[tok,
                  pl.BlockSpec((TOP_K, tm, d // 2), lambda b, t: (0, b * nt + t, 0)),
                  pl.BlockSpec((tm, LANES), lambda b, t: (b * nt + t, 0)),
                  pl.BlockSpec((None, 1, 3 * d), lambda b, t: (b, 0, 0)),
                  _full((1, d)), _full((1, d))],
        out_specs=tok,
        out_shape=jax.ShapeDtypeStruct((bsz, s, d), F32),
        compiler_params=_params(("arbitrary", "arbitrary")),
        name="moe_combine",
    )(x, y4, tw, mod, ln_g, ln_b)


def _moe(hm_p, hm_s, w_r, b_r, w_gu, b_gu, w_dn, b_dn, layer):
    half = hm_p.shape[-1]
    hm_p = hm_p.reshape(-1, half)
    hm_s = hm_s.reshape(-1, half)
    n_p, n_s = hm_p.shape[0], hm_s.shape[0]
    ne = w_r.shape[1]
    w_r = w_r.astype(BF16)
    b_r = b_r.reshape(1, ne)
    eid_p, tw_p, rank_p, cnt_p = _router(hm_p, w_r, b_r, jnp.zeros((1, ne), F32), PROMPT_TM)
    eid_s, tw_s, rank_s, cnt = _router(hm_s, w_r, b_r, cnt_p, n_s)
    tm = EXPERT_TM
    cnt = cnt.reshape(ne).astype(jnp.int32)
    gsz = ((cnt + tm - 1) // tm) * tm
    ends = jnp.cumsum(gsz)
    offs = ends - gsz
    group_off = lambda eid: jnp.sum(jnp.where(eid[:TOP_K, :, None] == jnp.arange(ne, dtype=jnp.int32), offs, 0), -1)
    pos_p = group_off(eid_p) + rank_p[:TOP_K]
    pos_s = (group_off(eid_s) + rank_s[:TOP_K]).T.reshape(-1)
    n_tiles = ((n_p + n_s) * TOP_K + ne * (tm - 1)) // tm + 1
    m_pad = n_tiles * tm
    tile_start = jnp.arange(n_tiles, dtype=jnp.int32) * tm
    tile_expert = jnp.minimum(jnp.sum((tile_start[:, None] >= ends[None, :]).astype(jnp.int32), axis=1), ne - 1)
    n_used = (ends[-1] // tm).astype(jnp.int32).reshape(1)
    n_valid = jnp.clip((offs + cnt)[tile_expert] - tile_start, 0, tm).astype(jnp.int32)
    idx_scatter = (pos_p.reshape(TOP_K, SC_WORKERS, -1, SC_CHUNK).transpose(1, 2, 0, 3)
                   .reshape(SC_WORKERS, -1, SC_CHUNK))
    xs = _sc_scatter_rows(hm_p, idx_scatter, m_pad)
    xs = _permute(hm_s, pos_s, xs, n_s)
    ys = _experts(xs, tile_expert, n_used, n_valid, w_gu, b_gu, w_dn, b_dn, layer)
    y4_p = _sc_gather_rows(ys, pos_p.reshape(-1)).reshape(TOP_K, n_p, half)
    return (y4_p, tw_p), (_unpermute(ys, pos_s, n_s, n_s), tw_s)


def _qkv_kernel(x_ref, modk_ref, modq_ref, cos_ref, sin_ref, wdkv_ref, gkv_ref, wdq_ref, gq_ref, wuq_ref,
                ckv_ref, kpe_ref, kc_ref, kp_ref, qh_ref, *, d, kvl, rope, nh, nope, q_scale):
    x = x_ref[...]
    cos = cos_ref[...]
    sin = sin_ref[...]
    modk = modk_ref[...]
    hk = (x * (1.0 + modk[:, d:2 * d]) + modk[:, :d]).astype(BF16)
    kv = jnp.dot(hk, wdkv_ref[...], preferred_element_type=F32)
    c = kv[:, :kvl]
    ckv = c * lax.rsqrt(jnp.mean(c * c, axis=-1, keepdims=True) + RMS_EPS) * gkv_ref[...]
    kp = kv[:, kvl:kvl + LANES] * cos + kv[:, kvl + LANES:kvl + 2 * LANES] * sin
    ckv_ref[...] = ckv
    kpe_ref[...] = kp[:, :rope]
    kc_ref[...] = ckv.astype(BF16)
    kp_ref[...] = kp[:, :rope].astype(BF16)

    modq = modq_ref[...]
    hq = (x * (1.0 + modq[:, d:2 * d]) + modq[:, :d]).astype(BF16)
    qd = jnp.dot(hq, wdq_ref[...], preferred_element_type=F32)
    qn = (qd * lax.rsqrt(jnp.mean(qd * qd, axis=-1, keepdims=True) + RMS_EPS) * gq_ref[...]).astype(BF16)
    q = jnp.dot(qn, wuq_ref[...], preferred_element_type=F32)
    for h in range(nh):
        qh_ref[h, :, :nope] = (q[:, h * nope:(h + 1) * nope] * q_scale).astype(BF16)
    pe0 = nh * nope
    sw0 = pe0 + nh * rope
    per = LANES // rope
    for j in range(nh // per):
        r2 = (q[:, pe0 + j * LANES:pe0 + (j + 1) * LANES] * cos
              + q[:, sw0 + j * LANES:sw0 + (j + 1) * LANES] * sin) * q_scale
        for i in range(per):
            qh_ref[j * per + i, :, nope:] = r2[:, i * rope:(i + 1) * rope].astype(BF16)


def _qkv(x, modk, modq, cos_t, sin_t, w, tm, dims):
    bsz, s, d = x.shape
    kvl, rope, nh, nope = dims["kvl"], dims["rope"], dims["nh"], dims["nope"]
    tok = lambda width: pl.BlockSpec((None, tm, width), lambda b, t: (b, t, 0))
    head = lambda width: pl.BlockSpec((None, nh, tm, width), lambda b, t: (b, 0, t, 0))
    tab = pl.BlockSpec((tm, LANES), lambda b, t: (t, 0))
    weights = [w["w_dkv"], w["g_kv"], w["w_dq"], w["g_q"], w["w_uq"]]
    kern = functools.partial(_qkv_kernel, d=d, kvl=kvl, rope=rope, nh=nh, nope=nope, q_scale=dims["q_scale"])
    return pl.pallas_call(
        kern,
        grid=(bsz, s // tm),
        in_specs=[tok(d),
                  pl.BlockSpec((None, 1, 2 * d), lambda b, t: (b, 0, 0)),
                  pl.BlockSpec((None, 1, 3 * d), lambda b, t: (b, 0, 0)),
                  tab, tab] + [_full(a.shape) for a in weights],
        out_specs=[tok(kvl), tok(rope), tok(kvl), tok(rope), head(nope + rope)],
        out_shape=[jax.ShapeDtypeStruct((bsz, s, kvl), F32), jax.ShapeDtypeStruct((bsz, s, rope), F32),
                   jax.ShapeDtypeStruct((bsz, s, kvl), BF16), jax.ShapeDtypeStruct((bsz, s, rope), BF16),
                   jax.ShapeDtypeStruct((bsz, nh, s, nope + rope), BF16)],
        compiler_params=_params(("arbitrary", "arbitrary")),
        name="mla_qkv",
    )(x, modk, modq, cos_t, sin_t, *weights)


def _kproj_kernel(kc_ref, kp_ref, wuk_ref, kh_ref, *, nh, nope):
    kc = kc_ref[...]
    kp = kp_ref[...]
    nt = (((1,), (1,)), ((), ()))
    for h in range(nh):
        kh_ref[h, :, :nope] = lax.dot_general(kc, wuk_ref[h], nt, preferred_element_type=F32).astype(BF16)
        kh_ref[h, :, nope:] = kp


def _kproj(kc, kp, w_uk, tm):
    bsz, s, kvl = kc.shape
    rope = kp.shape[-1]
    nh, nope, _ = w_uk.shape
    tok = lambda width: pl.BlockSpec((None, tm, width), lambda b, t: (b, t, 0))
    return pl.pallas_call(
        functools.partial(_kproj_kernel, nh=nh, nope=nope),
        grid=(bsz, s // tm),
        in_specs=[tok(kvl), tok(rope), _full(w_uk.shape)],
        out_specs=pl.BlockSpec((None, nh, tm, nope + rope), lambda b, t: (b, 0, t, 0)),
        out_shape=jax.ShapeDtypeStruct((bsz, nh, s, nope + rope), BF16),
        compiler_params=_params(("arbitrary", "arbitrary")),
        name="mla_kproj",
    )(kc, kp, w_uk)


def _attn_kernel(qi_ref, kj_ref, last_ref, edge_ref, qh_ref, kh_ref, kc_ref, o_ref, m_s, l_s, acc_s,
                 *, nh, tq, tk, kvl, causal, valid):
    s_id = pl.program_id(1)
    qi = qi_ref[s_id]
    kj = kj_ref[s_id]
    nt = (((1,), (1,)), ((), ()))

    @pl.when(kj == 0)
    def _():
        m_s[...] = jnp.full_like(m_s, -jnp.inf)
        l_s[...] = jnp.zeros_like(l_s)
        acc_s[...] = jnp.zeros_like(acc_s)

    def sweep(masked, nk):
        kc = kc_ref[:nk, :]
        if masked:
            kidx = kj * tk + lax.broadcasted_iota(jnp.int32, (tq, nk), 1)
            if causal:
                qidx = qi * tq + lax.broadcasted_iota(jnp.int32, (tq, nk), 0)
                visible = (kidx // CHUNK) <= (qidx // CHUNK)
            else:
                visible = kidx < valid

        def scores(h):
            s = lax.dot_general(qh_ref[h], kh_ref[h, :nk, :], nt, preferred_element_type=F32)
            return jnp.where(visible, s, -jnp.inf) if masked else s

        s_next = scores(0)
        for h in range(nh):
            s = s_next
            if h + 1 < nh:
                s_next = scores(h + 1)
            slabs = [s[:, j * LANES:(j + 1) * LANES] for j in range(nk // LANES)]
            mx = slabs[0]
            for sj in slabs[1:]:
                mx = jnp.maximum(mx, sj)
            m_prev = m_s[h]
            m_new = jnp.maximum(m_prev, jnp.max(mx, axis=1, keepdims=True))
            scale = jnp.exp2(m_prev - m_new)
            ps = []
            rsum = None
            for sj in slabs:
                pj = jnp.exp2(sj - m_new)
                ps.append(pj.astype(BF16))
                rsum = pj if rsum is None else rsum + pj
            p = jnp.concatenate(ps, axis=1)
            l_s[h] = scale * l_s[h] + jnp.sum(rsum, axis=1, keepdims=True)
            pv = jnp.dot(p, kc, preferred_element_type=F32)
            acc_s[h] = acc_s[h] * jnp.concatenate([scale] * (kvl // LANES), axis=1) + pv
            m_s[h] = m_new

    edge = edge_ref[s_id]

    if causal:
        @pl.when(edge == 2)
        def _():
            sweep(True, tk // 2)

    @pl.when(edge == 1)
    def _():
        sweep(True, tk)

    @pl.when(edge == 0)
    def _():
        sweep(False, tk)

    @pl.when(last_ref[s_id] == 1)
    def _():
        for h in range(nh):
            inv = jnp.concatenate([l_s[h]] * (kvl // LANES), axis=1)
            o_ref[h] = (acc_s[h] / inv).astype(BF16)


def _attention(qh, kh, kc, tq, tk, causal, valid):
    bsz, nh, s, qk = qh.shape
    skv, kvl = kc.shape[1], kc.shape[2]
    qi, kj, last, edge = [], [], [], []
    for i in range(s // tq):
        hi = ((i * tq + tq - 1) // tk) if causal else (skv // tk - 1)
        for j in range(hi + 1):
            qi.append(i)
            kj.append(j)
            last.append(1 if j == hi else 0)
            if causal:
                if (i + 1) * tq <= j * tk + tk // 2:
                    edge.append(2)
                else:
                    edge.append(1 if ((j + 1) * tk - 1) // CHUNK > (i * tq) // CHUNK else 0)
            else:
                edge.append(1 if (j + 1) * tk > valid else 0)
    tabs = [jnp.asarray(np.array(a, np.int32)) for a in (qi, kj, last, edge)]
    qspec = lambda width: pl.BlockSpec((None, nh, tq, width), lambda b, t, qi, kj, last, edge: (b, 0, qi[t], 0))
    khspec = pl.BlockSpec((None, nh, tk, qk), lambda b, t, qi, kj, last, edge: (b, 0, kj[t], 0))
    kcspec = pl.BlockSpec((None, tk, kvl), lambda b, t, qi, kj, last, edge: (b, kj[t], 0))
    grid_spec = pltpu.PrefetchScalarGridSpec(
        num_scalar_prefetch=4,
        grid=(bsz, len(qi)),
        in_specs=[qspec(qk), khspec, kcspec],
        out_specs=qspec(kvl),
        scratch_shapes=[pltpu.VMEM((nh, tq, LANES), F32), pltpu.VMEM((nh, tq, LANES), F32),
                        pltpu.VMEM((nh, tq, kvl), F32)],
    )
    kern = functools.partial(_attn_kernel, nh=nh, tq=tq, tk=tk, kvl=kvl, causal=causal, valid=valid)
    return pl.pallas_call(
        kern,
        grid_spec=grid_spec,
        out_shape=jax.ShapeDtypeStruct((bsz, nh, s, kvl), BF16),
        compiler_params=_params(("arbitrary", "arbitrary")),
        name="mla_attention",
    )(*tabs, qh, kh, kc)


def _attn_out_kernel(o_ref, x_ref, mod_ref, mod2_ref, wuv_ref, wo_ref, lng_ref, lnb_ref, x3_ref, hm_ref,
                     *, d, nh, alpha):
    parts = [jnp.dot(o_ref[h], wuv_ref[h], preferred_element_type=F32) for h in range(nh)]
    o = jnp.concatenate(parts, axis=1).astype(BF16)
    m = jnp.dot(o, wo_ref[...], preferred_element_type=F32)
    gate = mod_ref[...][:, 2 * d:]
    x3 = _layer_norm(alpha * x_ref[...] + (1.0 + gate) * m, lng_ref[...], lnb_ref[...])
    x3_ref[...] = x3
    mod2 = mod2_ref[...]
    hm_ref[...] = _pack_pairs(x3 * (1.0 + mod2[:, d:2 * d]) + mod2[:, :d])


def _attn_out(o_lat, x, mod, mod2, w_uv, w_o, ln_g, ln_b, tm, alpha):
    bsz, s, d = x.shape
    nh, kvl = o_lat.shape[1], o_lat.shape[3]
    tok = lambda width: pl.BlockSpec((None, tm, width), lambda b, t: (b, t, 0))
    modspec = pl.BlockSpec((None, 1, 3 * d), lambda b, t: (b, 0, 0))
    return pl.pallas_call(
        functools.partial(_attn_out_kernel, d=d, nh=nh, alpha=alpha),
        grid=(bsz, s // tm),
        in_specs=[pl.BlockSpec((None, nh, tm, kvl), lambda b, t: (b, 0, t, 0)), tok(d), modspec, modspec,
                  _full(w_uv.shape), _full(w_o.shape), _full((1, d)), _full((1, d))],
        out_specs=[tok(d), tok(d // 2)],
        out_shape=[jax.ShapeDtypeStruct((bsz, s, d), F32), jax.ShapeDtypeStruct((bsz, s, d // 2), I32)],
        compiler_params=_params(("arbitrary", "arbitrary")),
        name="mla_out",
    )(o_lat, x, mod, mod2, w_uv, w_o, ln_g, ln_b)


def _rope_tables(pos, rope):
    half = rope // 2
    inv = ROPE_THETA ** (-2.0 * jnp.arange(half, dtype=F32) / rope)
    ang = pos.astype(F32)[:, None] * inv[None, :]
    cos, sin = jnp.cos(ang), jnp.sin(ang)
    rep = LANES // rope
    return (jnp.concatenate([cos, cos] * rep, axis=1), jnp.concatenate([-sin, sin] * rep, axis=1))


def kernel(x_prompt, x_sample, state_conv, state_rglru, cache_ckv, cache_kpe, c_prompt, c_sample, w_ada, b_ada, ln_g, ln_b, lru_w_in, lru_b_in, lru_conv_w, lru_conv_b, lru_w_a, lru_b_a, lru_w_i, lru_b_i, lru_lambda, lru_w_out, lru_b_out, kv_w_ada, kv_b_ada, mla_w_dkv, mla_g_kv, mla_w_uk, mla_w_uv, mla_w_dq, mla_g_q, mla_w_uq, mla_w_o, moe_w_r, moe_b_r, moe_w_gu, moe_b_gu, moe_w_dn, moe_b_dn):
    bp, sp, d = x_prompt.shape
    bs, ss, _ = x_sample.shape
    depth = w_ada.shape[0]
    alpha = float((2.0 * depth) ** 0.25)
    nh, nope, kvl = mla_w_uk.shape
    rope = cache_kpe.shape[-1]
    past = cache_ckv.shape[1]
    hist = state_conv.shape[2]
    dims = dict(kvl=kvl, rope=rope, nh=nh, nope=nope, q_scale=float((nope + rope) ** -0.5 * LOG2E))
    row = lambda v: v.reshape(1, -1)

    nrow = bp + bs
    rpad = -nrow % (2 * SUBLANES)
    c_rows = jnp.concatenate([c_prompt, c_sample, jnp.zeros((rpad, d), F32)], axis=0)
    mods = _ada(c_rows, w_ada.reshape(depth * 2, d, 3 * d), b_ada.reshape(depth * 2, 1, 3 * d))
    modkv = _ada(c_rows, kv_w_ada.reshape(1, d, 2 * d), kv_b_ada.reshape(1, 1, 2 * d))[0]
    mod_p = lambda i: mods[i, :bp].reshape(bp, 1, 3 * d)
    mod_s = lambda i: mods[i, bp:nrow].reshape(bs, 1, 3 * d)

    l0w = dict(w_in=lru_w_in[0].astype(BF16), b_in=row(lru_b_in[0]), conv_w=lru_conv_w[0], conv_b=row(lru_conv_b[0]),
               w_a=lru_w_a[0].astype(BF16), b_a=row(lru_b_a[0]), w_i=lru_w_i[0].astype(BF16), b_i=row(lru_b_i[0]),
               lam=row(lru_lambda[0]), w_out=lru_w_out[0].astype(BF16), b_out=row(lru_b_out[0]),
               ln_g=row(ln_g[0, 0]), ln_b=row(ln_b[0, 0]))
    x1p, hmp, conv_p, h_p = _l0_mixer(x_prompt, mod_p(0), mod_p(1), jnp.zeros((bp, hist, d), F32),
                                      jnp.zeros((bp, 1, d), F32), l0w, L0_TB, alpha)
    x1s, hms, conv_s, h_s = _l0_mixer(x_sample, mod_s(0), mod_s(1), state_conv[0], state_rglru[0].reshape(bs, 1, d),
                                      l0w, ss, alpha)

    (y4p, twp), (y4s, tws) = _moe(hmp, hms, moe_w_r[0], moe_b_r[0], moe_w_gu, moe_b_gu, moe_w_dn, moe_b_dn, 0)
    x2p = _combine(x1p, y4p, twp, mod_p(1), row(ln_g[0, 1]), row(ln_b[0, 1]), PROMPT_TM, alpha)
    x2s = _combine(x1s, y4s, tws, mod_s(1), row(ln_g[0, 1]), row(ln_b[0, 1]), ss, alpha)

    pe_cols = jnp.arange(rope)
    sw_cols = jnp.concatenate([pe_cols[rope // 2:], pe_cols[:rope // 2]])
    zpad = jnp.zeros((d, LANES - rope), F32)
    w_dkv_ext = jnp.concatenate([mla_w_dkv[:, :kvl], mla_w_dkv[:, kvl:], zpad,
                                 mla_w_dkv[:, kvl:][:, sw_cols], zpad], axis=1).astype(BF16)
    wq = mla_w_uq[0].reshape(-1, nh, nope + rope)
    w_uq_ext = jnp.concatenate([wq[:, :, :nope].reshape(-1, nh * nope), wq[:, :, nope:].reshape(-1, nh * rope),
                                wq[:, :, nope:][:, :, sw_cols].reshape(-1, nh * rope)], axis=1).astype(BF16)
    qw = dict(w_dkv=w_dkv_ext, g_kv=row(mla_g_kv), w_dq=mla_w_dq[0].astype(BF16), g_q=row(mla_g_q[0]),
              w_uq=w_uq_ext)
    w_uk = mla_w_uk.astype(BF16)
    cos_p, sin_p = _rope_tables(jnp.arange(sp), rope)
    cos_s, sin_s = _rope_tables(past + jnp.arange(ss), rope)
    modkv_p = modkv[:bp].reshape(bp, 1, 2 * d)
    modkv_s = modkv[bp:nrow].reshape(bs, 1, 2 * d)
    ckv_p, kpe_p, kc_p, kp_p, qh_p = _qkv(x2p, modkv_p, mod_p(2), cos_p, sin_p, qw, PROMPT_TM, dims)
    ckv_s, kpe_s, kc_s, kp_s, qh_s = _qkv(x2s, modkv_s, mod_s(2), cos_s, sin_s, qw, ss, dims)

    o_p = _attention(qh_p, _kproj(kc_p, kp_p, w_uk, PROMPT_TM), kc_p, ATT_TQ, ATT_TK, True, sp)
    skv = past + ss
    kpad = -skv % LANES
    kc_all = jnp.concatenate([cache_ckv.astype(BF16), kc_s, jnp.zeros((bs, kpad, kvl), BF16)], axis=1)
    kp_all = jnp.concatenate([cache_kpe.astype(BF16), kp_s, jnp.zeros((bs, kpad, rope), BF16)], axis=1)
    o_s = _attention(qh_s, _kproj(kc_all, kp_all, w_uk, skv + kpad), kc_all, ss, skv + kpad, False, skv)
    w_uv = mla_w_uv.astype(BF16)
    w_o = mla_w_o[0].astype(BF16)
    x3p, hm3p = _attn_out(o_p, x2p, mod_p(2), mod_p(3), w_uv, w_o, row(ln_g[1, 0]), row(ln_b[1, 0]), PROMPT_TM, alpha)
    x3s, hm3s = _attn_out(o_s, x2s, mod_s(2), mod_s(3), w_uv, w_o, row(ln_g[1, 0]), row(ln_b[1, 0]), ss, alpha)

    (y4p, twp), (y4s, tws) = _moe(hm3p, hm3s, moe_w_r[1], moe_b_r[1], moe_w_gu, moe_b_gu, moe_w_dn, moe_b_dn, 1)
    y_p = _combine(x3p, y4p, twp, mod_p(3), row(ln_g[1, 1]), row(ln_b[1, 1]), PROMPT_TM, alpha)
    y_s = _combine(x3s, y4s, tws, mod_s(3), row(ln_g[1, 1]), row(ln_b[1, 1]), ss, alpha)

    return (y_p, y_s, conv_p[None], h_p.reshape(1, bp, d), ckv_p, kpe_p,
            conv_s[None], h_s.reshape(1, bs, d), ckv_s, kpe_s)
```

```python
import functools

import numpy as np
import jax
import jax.numpy as jnp
from jax import lax
from jax.experimental import pallas as pl
from jax.experimental.pallas import tpu as pltpu
from jax.experimental.pallas import tpu_sc as plsc

F32 = jnp.float32
BF16 = jnp.bfloat16
I32 = jnp.int32

CHUNK = 64
N_LRU_BLOCKS = 8
LRU_C = 8.0
ROPE_THETA = 10000.0
TOP_K = 4
SWIGLU_ALPHA = 1.702
SWIGLU_LIMIT = 7.0
LN_EPS = 1e-5
RMS_EPS = 1e-6
LOG2E = 1.4426950408889634

LANES = 128
SUBLANES = 8
VMEM_LIMIT = 56 * 1024 * 1024

PROMPT_TM = 512
L0_TB = 512
EXPERT_TM = 384
SC_CORES = 2
SC_WORKERS = SC_CORES * 16
SC_CHUNK = 64
ATT_TQ = 512
ATT_TK = 1024


def _sigmoid(x):
    return 0.5 * (jnp.tanh(0.5 * x) + 1.0)


def _layer_norm(v, g, b):
    mu = jnp.mean(v, axis=-1, keepdims=True)
    d = v - mu
    var = jnp.mean(d * d, axis=-1, keepdims=True)
    return d * lax.rsqrt(var + LN_EPS) * g + b


def _pack_pairs(x):
    w = x.shape[1] // 2
    hi = lax.bitcast_convert_type(x[:, :w].astype(BF16).astype(F32), I32)
    lo = lax.bitcast_convert_type(x[:, w:].astype(BF16).astype(F32), I32)
    return hi | lax.shift_right_logical(lo, jnp.int32(16))


def _unpack_pairs(words):
    hi = lax.bitcast_convert_type(words & jnp.int32(-65536), F32)
    lo = lax.bitcast_convert_type(words << 16, F32)
    return hi, lo


def _unpack_bf16(words):
    hi, lo = _unpack_pairs(words)
    return jnp.concatenate([hi.astype(BF16), lo.astype(BF16)], axis=1)


def _full(shape):
    nd = len(shape)
    return pl.BlockSpec(shape, lambda *_: (0,) * nd)


def _params(sem, vmem=VMEM_LIMIT):
    return pltpu.CompilerParams(dimension_semantics=sem, vmem_limit_bytes=vmem)


def _ada_kernel(c_ref, w_ref, b_ref, o_ref):
    c = c_ref[...]
    s = (c * _sigmoid(c)).astype(BF16)
    o_ref[...] = jnp.dot(s, w_ref[...].astype(BF16), preferred_element_type=F32) + b_ref[...]


def _ada(c_rows, w, b, tn=1024):
    g, d, n = w.shape
    r = c_rows.shape[0]
    return pl.pallas_call(
        _ada_kernel,
        grid=(g, n // tn),
        in_specs=[_full((r, d)),
                  pl.BlockSpec((None, d, tn), lambda i, j: (i, 0, j)),
                  pl.BlockSpec((None, 1, tn), lambda i, j: (i, 0, j))],
        out_specs=pl.BlockSpec((None, r, tn), lambda i, j: (i, 0, j)),
        out_shape=jax.ShapeDtypeStruct((g, r, n), F32),
        compiler_params=_params(("arbitrary", "arbitrary")),
        name="ada_mod",
    )(c_rows, w, b)


def _l0_kernel(x_ref, mod_ref, mod2_ref, conv0_ref, h0_ref, win_ref, bin_ref, cw_ref, cb_ref,
               wa_ref, ba_ref, wi_ref, bi_ref, lam_ref, wout_ref, bout_ref, lng_ref, lnb_ref,
               x1_ref, hm_ref, convn_ref, hlast_ref,
               ubuf, a_s, b_s, h_s, *, tb, d, alpha):
    t = pl.program_id(1)
    pad = SUBLANES
    hist = convn_ref.shape[0]

    @pl.when(t == 0)
    def _():
        ubuf[pad - hist:pad, :] = conv0_ref[...]
        h_s[...] = h0_ref[...]

    x = x_ref[...]
    mod = mod_ref[...]
    shift, scale, gate = mod[:, :d], mod[:, d:2 * d], mod[:, 2 * d:]
    h = (x * (1.0 + scale) + shift).astype(BF16)
    proj = jnp.dot(h, win_ref[...], preferred_element_type=F32) + bin_ref[...]
    gate_b = proj[:, :d]
    ubuf[pad:pad + tb, :] = proj[:, d:]

    cw = cw_ref[...]
    u = cb_ref[...]
    for k in range(hist + 1):
        u = u + ubuf[pad - hist + k:pad - hist + k + tb, :] * cw[k:k + 1, :]
    ubuf[pad - hist:pad, :] = ubuf[pad + tb - hist:pad + tb, :]

    ub = u.astype(BF16)
    blk = d // N_LRU_BLOCKS
    lam = lam_ref[...]
    neg = -lam
    softplus = jnp.maximum(neg, 0.0) + jnp.log1p(jnp.exp(-jnp.abs(neg)))
    for n in range(N_LRU_BLOCKS):
        sl = slice(n * blk, (n + 1) * blk)
        un = ub[:, sl]
        r = _sigmoid(jnp.dot(un, wa_ref[n], preferred_element_type=F32) + ba_ref[:, sl])
        ig = _sigmoid(jnp.dot(un, wi_ref[n], preferred_element_type=F32) + bi_ref[:, sl])
        log_a = -LRU_C * r * softplus[:, sl]
        a = jnp.exp(log_a)
        one_m_a2 = -jnp.tanh(log_a) * (a * a + 1.0)
        a_s[:, sl] = a
        b_s[:, sl] = jnp.sqrt(one_m_a2) * (ig * u[:, sl])

    row = lax.broadcasted_iota(jnp.int32, (SUBLANES, d), 0)

    def group(g, hprev):
        r0 = pl.multiple_of(g * SUBLANES, SUBLANES)
        a = a_s[pl.ds(r0, SUBLANES), :]
        b = b_s[pl.ds(r0, SUBLANES), :]
        for sh in (1, 2, 4):
            keep = row >= sh
            a_sh = pltpu.roll(a, sh, axis=0)
            b_sh = pltpu.roll(b, sh, axis=0)
            b = jnp.where(keep, a * b_sh + b, b)
            a = jnp.where(keep, a * a_sh, a)
        hh = a * hprev + b
        b_s[pl.ds(r0, SUBLANES), :] = hh
        return hh[SUBLANES - 1:SUBLANES, :]

    h_s[...] = lax.fori_loop(0, tb // SUBLANES, group, h_s[...])

    y = b_s[...]
    gl = 0.5 * gate_b * (1.0 + jnp.tanh(0.7978845608028654 * (gate_b + 0.044715 * gate_b * gate_b * gate_b)))
    out = jnp.dot((gl * y).astype(BF16), wout_ref[...], preferred_element_type=F32) + bout_ref[...]
    x1 = _layer_norm(alpha * x + (1.0 + gate) * out, lng_ref[...], lnb_ref[...])
    x1_ref[...] = x1
    mod2 = mod2_ref[...]
    hm_ref[...] = _pack_pairs(x1 * (1.0 + mod2[:, d:2 * d]) + mod2[:, :d])

    @pl.when(t == pl.num_programs(1) - 1)
    def _():
        convn_ref[...] = ubuf[pad - hist:pad, :]
        hlast_ref[...] = h_s[...]


def _l0_mixer(x, mod, mod2, conv0, h0, w, tb, alpha):
    bsz, s, d = x.shape
    hist = conv0.shape[1]
    tok = lambda width: pl.BlockSpec((None, tb, width), lambda b, t: (b, t, 0))
    per_b = lambda rows, width: pl.BlockSpec((None, rows, width), lambda b, t: (b, 0, 0))
    kern = functools.partial(_l0_kernel, tb=tb, d=d, alpha=alpha)
    weights = [w["w_in"], w["b_in"], w["conv_w"], w["conv_b"], w["w_a"], w["b_a"], w["w_i"], w["b_i"],
               w["lam"], w["w_out"], w["b_out"], w["ln_g"], w["ln_b"]]
    return pl.pallas_call(
        kern,
        grid=(bsz, s // tb),
        in_specs=[tok(d), per_b(1, 3 * d), per_b(1, 3 * d), per_b(hist, d), per_b(1, d)]
                 + [_full(a.shape) for a in weights],
        out_specs=[tok(d), tok(d // 2), per_b(hist, d), per_b(1, d)],
        out_shape=[jax.ShapeDtypeStruct((bsz, s, d), F32), jax.ShapeDtypeStruct((bsz, s, d // 2), I32),
                   jax.ShapeDtypeStruct((bsz, hist, d), F32), jax.ShapeDtypeStruct((bsz, 1, d), F32)],
        scratch_shapes=[pltpu.VMEM((tb + SUBLANES, d), F32), pltpu.VMEM((tb, d), F32),
                        pltpu.VMEM((tb, d), F32), pltpu.VMEM((1, d), F32)],
        compiler_params=_params(("arbitrary", "arbitrary")),
        name="l0_mixer",
    )(x, mod, mod2, conv0, h0, *weights)


def _router_kernel(hm_ref, wr_ref, br_ref, base0_ref, eid_ref, tw_ref, rank_ref, cnt_ref, base, *, tm, ne):
    i = pl.program_id(0)

    @pl.when(i == 0)
    def _():
        base[...] = base0_ref[...]

    logits = jnp.dot(_unpack_bf16(hm_ref[...]), wr_ref[...], preferred_element_type=F32) + br_ref[...]
    col = lax.broadcasted_iota(jnp.int32, (tm, ne), 1).astype(F32)
    l = logits
    vals, idxs = [], []
    for _ in range(TOP_K):
        m = jnp.max(l, axis=-1, keepdims=True)
        idx = jnp.min(jnp.where(l == m, col, float(ne)), axis=-1, keepdims=True)
        vals.append(m)
        idxs.append(idx)
        l = jnp.where(col == idx, -jnp.inf, l)
    es = [jnp.exp(v - vals[0]) for v in vals]
    tot = es[0] + es[1] + es[2] + es[3]

    ri = lax.broadcasted_iota(jnp.int32, (tm, tm), 0)
    ci = lax.broadcasted_iota(jnp.int32, (tm, tm), 1)
    tri = (ci < ri).astype(BF16)
    run = base[...]
    lane = lax.broadcasted_iota(jnp.int32, (tm, LANES), 1)
    eid_o = jnp.zeros((tm, LANES), F32)
    tw_o = jnp.zeros((tm, LANES), F32)
    rk_o = jnp.zeros((tm, LANES), F32)
    for k in range(TOP_K):
        oh = col == idxs[k]
        ohf = oh.astype(F32)
        before = jnp.dot(tri, ohf.astype(BF16), preferred_element_type=F32)
        rank = jnp.sum(jnp.where(oh, before + run, 0.0), axis=-1, keepdims=True)
        run = run + jnp.sum(ohf, axis=0, keepdims=True)
        sel = lane == k
        eid_o = jnp.where(sel, idxs[k], eid_o)
        tw_o = jnp.where(sel, es[k] / tot, tw_o)
        rk_o = jnp.where(sel, rank, rk_o)
    base[...] = run
    eid_ref[...] = eid_o.T[:SUBLANES, :].astype(jnp.int32)
    tw_ref[...] = tw_o
    rank_ref[...] = rk_o.T[:SUBLANES, :].astype(jnp.int32)
    cnt_ref[...] = run


def _router(hm, w_r, b_r, base0, tm):
    n, half = hm.shape
    ne = w_r.shape[1]
    kern = functools.partial(_router_kernel, tm=tm, ne=ne)
    tokrow = pl.BlockSpec((tm, LANES), lambda i: (i, 0))
    tokcol = pl.BlockSpec((SUBLANES, tm), lambda i: (0, i))
    return pl.pallas_call(
        kern,
        grid=(n // tm,),
        in_specs=[pl.BlockSpec((tm, half), lambda i: (i, 0)), _full((2 * half, ne)), _full((1, ne)), _full((1, ne))],
        out_specs=[tokcol, tokrow, tokcol, _full((1, ne))],
        out_shape=[jax.ShapeDtypeStruct((SUBLANES, n), jnp.int32), jax.ShapeDtypeStruct((n, LANES), F32),
                   jax.ShapeDtypeStruct((SUBLANES, n), jnp.int32), jax.ShapeDtypeStruct((1, ne), F32)],
        scratch_shapes=[pltpu.VMEM((1, ne), F32)],
        compiler_params=_params(("arbitrary",)),
        name="moe_router",
    )(hm, w_r, b_r, base0)


def _permute_kernel(pos_ref, hm_ref, init_ref, out_ref, sem, *, tm):
    del init_ref

    def copy(t, k):
        return pltpu.make_async_copy(hm_ref.at[pl.ds(t, 1)], out_ref.at[pl.ds(pos_ref[t * TOP_K + k], 1)], sem)

    def issue(t, c):
        for k in range(TOP_K):
            copy(t, k).start()
        return c

    def drain(t, c):
        for k in range(TOP_K):
            copy(t, k).wait()
        return c

    lax.fori_loop(0, tm, issue, 0)
    lax.fori_loop(0, tm, drain, 0)


def _permute(hm, pos_flat, dest, tm):
    n, w = hm.shape
    return pl.pallas_call(
        functools.partial(_permute_kernel, tm=tm),
        grid=(n // tm,),
        in_specs=[pl.BlockSpec((tm * TOP_K,), lambda i: (i,), memory_space=pltpu.SMEM),
                  pl.BlockSpec((tm, w), lambda i: (i, 0)),
                  pl.BlockSpec(memory_space=pl.ANY)],
        out_specs=pl.BlockSpec(memory_space=pl.ANY),
        out_shape=jax.ShapeDtypeStruct(dest.shape, dest.dtype),
        scratch_shapes=[pltpu.SemaphoreType.DMA],
        input_output_aliases={2: 0},
        compiler_params=_params(("arbitrary",)),
        name="moe_permute",
    )(pos_flat, hm, dest)


def _unpermute_kernel(pos_ref, ys_ref, out_ref, sem, *, tm):
    def copy(t, k):
        return pltpu.make_async_copy(ys_ref.at[pl.ds(pos_ref[t * TOP_K + k], 1)], out_ref.at[k, pl.ds(t, 1)], sem)

    def issue(t, c):
        for k in range(TOP_K):
            copy(t, k).start()
        return c

    def drain(t, c):
        for k in range(TOP_K):
            copy(t, k).wait()
        return c

    lax.fori_loop(0, tm, issue, 0)
    lax.fori_loop(0, tm, drain, 0)


def _unpermute(ys, pos_flat, n, tm):
    _, w = ys.shape
    return pl.pallas_call(
        functools.partial(_unpermute_kernel, tm=tm),
        grid=(n // tm,),
        in_specs=[pl.BlockSpec((tm * TOP_K,), lambda i: (i,), memory_space=pltpu.SMEM),
                  pl.BlockSpec(memory_space=pl.ANY)],
        out_specs=pl.BlockSpec((TOP_K, tm, w), lambda i: (0, i, 0)),
        out_shape=jax.ShapeDtypeStruct((TOP_K, n, w), ys.dtype),
        scratch_shapes=[pltpu.SemaphoreType.DMA],
        compiler_params=_params(("arbitrary",)),
        name="moe_unpermute",
    )(pos_flat, ys)


def _sc_mesh():
    return plsc.VectorSubcoreMesh(core_axis_name="c", subcore_axis_name="s")


def _sc_worker():
    return lax.axis_index("s") * SC_CORES + lax.axis_index("c")


def _sc_scatter_rows(rows, idx, m_pad):
    n, w = rows.shape
    per_w = n // SC_WORKERS
    n_chunks = per_w // SC_CHUNK

    def body(rows_hbm, idx_hbm, out_hbm, idx_v, buf, g0, g1, o0, o1):
        gsem, osem = (g0, g1), (o0, o1)
        wid = _sc_worker()
        base = wid * per_w
        pltpu.sync_copy(idx_hbm.at[wid], idx_v)

        def get(c, slot):
            return pltpu.make_async_copy(rows_hbm.at[pl.ds(base + c * SC_CHUNK, SC_CHUNK)], buf.at[slot], gsem[slot])

        def put(c, k, slot):
            return pltpu.make_async_copy(buf.at[slot], out_hbm.at[idx_v.at[c * TOP_K + k]], osem[slot])

        get(0, 0).start()
        for c in range(n_chunks):
            slot = c % 2
            get(c, slot).wait()
            if c + 1 < n_chunks:
                if c >= 1:
                    for k in range(TOP_K):
                        put(c - 1, k, 1 - slot).wait()
                get(c + 1, 1 - slot).start()
            for k in range(TOP_K):
                put(c, k, slot).start()
        for c in range(max(n_chunks - 2, 0), n_chunks):
            for k in range(TOP_K):
                put(c, k, c % 2).wait()

    return pl.kernel(
        body, mesh=_sc_mesh(),
        out_type=jax.ShapeDtypeStruct((m_pad, w), rows.dtype),
        scratch_types=[pltpu.VMEM((n_chunks * TOP_K, SC_CHUNK), jnp.int32), pltpu.VMEM((2, SC_CHUNK, w), rows.dtype)]
                      + [pltpu.SemaphoreType.DMA] * 4,
        name="moe_sc_scatter",
    )(rows, idx)


def _sc_gather_rows(table, idx):
    b = idx.shape[0]
    w = table.shape[1]
    per_w = b // SC_WORKERS
    n_chunks = per_w // SC_CHUNK

    def body(table_hbm, idx_hbm, out_hbm, idx_v, buf, g0, g1, o0, o1):
        gsem, osem = (g0, g1), (o0, o1)
        base = _sc_worker() * per_w
        pltpu.sync_copy(idx_hbm.at[pl.ds(base, per_w)], idx_v)

        def get(c, slot):
            return pltpu.make_async_copy(table_hbm.at[idx_v.at[pl.ds(c * SC_CHUNK, SC_CHUNK)]], buf.at[slot],
                                         gsem[slot])

        def put(c, slot):
            return pltpu.make_async_copy(buf.at[slot], out_hbm.at[pl.ds(base + c * SC_CHUNK, SC_CHUNK)], osem[slot])

        get(0, 0).start()
        for c in range(n_chunks):
            slot = c % 2
            get(c, slot).wait()
            if c + 1 < n_chunks:
                if c >= 1:
                    put(c - 1, 1 - slot).wait()
                get(c + 1, 1 - slot).start()
            put(c, slot).start()
        for c in range(max(n_chunks - 2, 0), n_chunks):
            put(c, c % 2).wait()

    return pl.kernel(
        body, mesh=_sc_mesh(),
        out_type=jax.ShapeDtypeStruct((b, w), table.dtype),
        scratch_types=[pltpu.VMEM((per_w,), jnp.int32), pltpu.VMEM((2, SC_CHUNK, w), table.dtype)]
                      + [pltpu.SemaphoreType.DMA] * 4,
        name="moe_sc_gather",
    )(table, idx)


def _expert_kernel(te_ref, nu_ref, nv_ref, xs_ref, wgu_ref, bgu_ref, wdn_ref, bdn_ref, ys_ref, wgu_bf, wdn_bf, *, d):
    j = pl.program_id(0)
    e = te_ref[j]
    prev = te_ref[jnp.maximum(j - 1, 0)]

    @pl.when(jnp.logical_or(j == 0, e != prev))
    def _():
        rows = 128

        def cast(c, carry):
            r0 = pl.multiple_of(c * rows, rows)
            wgu_bf[pl.ds(r0, rows), :] = wgu_ref[pl.ds(r0, rows), :].astype(BF16)
            wdn_bf[pl.ds(r0, rows), :] = wdn_ref[pl.ds(r0, rows), :].astype(BF16)
            return carry

        lax.fori_loop(0, d // rows, cast, 0)

    @pl.when(j < nu_ref[0])
    def _():
        rowi = lax.broadcasted_iota(jnp.int32, xs_ref.shape, 0)
        xw = jnp.where(rowi < nv_ref[j], xs_ref[...], jnp.int32(0))
        gu = jnp.dot(_unpack_bf16(xw), wgu_bf[...], preferred_element_type=F32) + bgu_ref[...]
        g = jnp.minimum(gu[:, :d], SWIGLU_LIMIT)
        u = jnp.clip(gu[:, d:], -SWIGLU_LIMIT, SWIGLU_LIMIT)
        act = (u + 1.0) * (g * _sigmoid(SWIGLU_ALPHA * g))
        y = jnp.dot(act.astype(BF16), wdn_bf[...], preferred_element_type=F32) + bdn_ref[...]
        ys_ref[...] = _pack_pairs(y)

    @pl.when(j >= nu_ref[0])
    def _():
        ys_ref[...] = jnp.zeros_like(ys_ref)


def _experts(xs, tile_expert, n_used, n_valid, w_gu, b_gu, w_dn, b_dn, layer, tm=EXPERT_TM):
    m_pad, half = xs.shape
    d = 2 * half
    nl, ne = w_gu.shape[:2]
    rows = lambda j, te, nu, nv: (jnp.minimum(j, nu[0] - 1), 0)
    wsel = lambda j, te, nu, nv: (layer, te[j], 0, 0)
    grid_spec = pltpu.PrefetchScalarGridSpec(
        num_scalar_prefetch=3,
        grid=(m_pad // tm,),
        in_specs=[pl.BlockSpec((tm, half), rows),
                  pl.BlockSpec((None, None, d, 2 * d), wsel),
                  pl.BlockSpec((None, None, 1, 2 * d), wsel),
                  pl.BlockSpec((None, None, d, d), wsel),
                  pl.BlockSpec((None, None, 1, d), wsel)],
        out_specs=pl.BlockSpec((tm, half), lambda j, te, nu, nv: (j, 0)),
        scratch_shapes=[pltpu.VMEM((d, 2 * d), BF16), pltpu.VMEM((d, d), BF16)],
    )
    return pl.pallas_call(
        functools.partial(_expert_kernel, d=d),
        grid_spec=grid_spec,
        out_shape=jax.ShapeDtypeStruct((m_pad, half), I32),
        compiler_params=_params(("arbitrary",)),
        name="moe_experts",
    )(tile_expert, n_used, n_valid, xs, w_gu, b_gu.reshape(nl, ne, 1, 2 * d), w_dn, b_dn.reshape(nl, ne, 1, d))


def _combine_kernel(x_ref, y_ref, tw_ref, mod_ref, lng_ref, lnb_ref, o_ref, *, d, alpha):
    tw = tw_ref[...]
    m_hi = None
    m_lo = None
    for k in range(TOP_K):
        hi, lo = _unpack_pairs(y_ref[k])
        wk = tw[:, k:k + 1]
        m_hi = wk * hi if m_hi is None else m_hi + wk * hi
        m_lo = wk * lo if m_lo is None else m_lo + wk * lo
    m = jnp.concatenate([m_hi, m_lo], axis=1)
    gate = mod_ref[...][:, 2 * d:]
    o_ref[...] = _layer_norm(alpha * x_ref[...] + (1.0 + gate) * m, lng_ref[...], lnb_ref[...])


def _combine(x, y4, tw, mod, ln_g, ln_b, tm, alpha):
    bsz, s, d = x.shape
    nt = s // tm
    tok = pl.BlockSpec((None, tm, d), lambda b, t: (b, t, 0))
    return pl.pallas_call(
        functools.partial(_combine_kernel, d=d, alpha=alpha),
        grid=(bsz, nt),
        in_specs=[tok,
                  pl.BlockSpec((TOP_K, tm, d // 2), lambda b, t: (0, b * nt + t, 0)),
                  pl.BlockSpec((tm, LANES), lambda b, t: (b * nt + t, 0)),
                  pl.BlockSpec((None, 1, 3 * d), lambda b, t: (b, 0, 0)),
                  _full((1, d)), _full((1, d))],
        out_specs=tok,
        out_shape=jax.ShapeDtypeStruct((bsz, s, d), F32),
        compiler_params=_params(("arbitrary", "arbitrary")),
        name="moe_combine",
    )(x, y4, tw, mod, ln_g, ln_b)


def _moe(hm_p, hm_s, w_r, b_r, w_gu, b_gu, w_dn, b_dn, layer):
    half = hm_p.shape[-1]
    hm_p = hm_p.reshape(-1, half)
    hm_s = hm_s.reshape(-1, half)
    n_p, n_s = hm_p.shape[0], hm_s.shape[0]
    ne = w_r.shape[1]
    w_r = w_r.astype(BF16)
    b_r = b_r.reshape(1, ne)
    eid_p, tw_p, rank_p, cnt_p = _router(hm_p, w_r, b_r, jnp.zeros((1, ne), F32), PROMPT_TM)
    eid_s, tw_s, rank_s, cnt = _router(hm_s, w_r, b_r, cnt_p, n_s)
    tm = EXPERT_TM
    cnt = cnt.reshape(ne).astype(jnp.int32)
    gsz = ((cnt + tm - 1) // tm) * tm
    ends = jnp.cumsum(gsz)
    offs = ends - gsz
    group_off = lambda eid: jnp.sum(jnp.where(eid[:TOP_K, :, None] == jnp.arange(ne, dtype=jnp.int32), offs, 0), -1)
    pos_p = group_off(eid_p) + rank_p[:TOP_K]
    pos_s = (group_off(eid_s) + rank_s[:TOP_K]).T.reshape(-1)
    n_tiles = ((n_p + n_s) * TOP_K + ne * (tm - 1)) // tm + 1
    m_pad = n_tiles * tm
    tile_start = jnp.arange(n_tiles, dtype=jnp.int32) * tm
    tile_expert = jnp.minimum(jnp.sum((tile_start[:, None] >= ends[None, :]).astype(jnp.int32), axis=1), ne - 1)
    n_used = (ends[-1] // tm).astype(jnp.int32).reshape(1)
    n_valid = jnp.clip((offs + cnt)[tile_expert] - tile_start, 0, tm).astype(jnp.int32)
    idx_scatter = (pos_p.reshape(TOP_K, SC_WORKERS, -1, SC_CHUNK).transpose(1, 2, 0, 3)
                   .reshape(SC_WORKERS, -1, SC_CHUNK))
    xs = _sc_scatter_rows(hm_p, idx_scatter, m_pad)
    xs = _permute(hm_s, pos_s, xs, n_s)
    ys = _experts(xs, tile_expert, n_used, n_valid, w_gu, b_gu, w_dn, b_dn, layer)
    y4_p = _sc_gather_rows(ys, pos_p.reshape(-1)).reshape(TOP_K, n_p, half)
    return (y4_p, tw_p), (_unpermute(ys, pos_s, n_s, n_s), tw_s)


def _qkv_kernel(x_ref, modk_ref, modq_ref, cos_ref, sin_ref, wdkv_ref, gkv_ref, wdq_ref, gq_ref, wuq_ref,
                ckv_ref, kpe_ref, kc_ref, kp_ref, qh_ref, *, d, kvl, rope, nh, nope, q_scale):
    x = x_ref[...]
    cos = cos_ref[...]
    sin = sin_ref[...]
    modk = modk_ref[...]
    hk = (x * (1.0 + modk[:, d:2 * d]) + modk[:, :d]).astype(BF16)
    kv = jnp.dot(hk, wdkv_ref[...], preferred_element_type=F32)
    c = kv[:, :kvl]
    ckv = c * lax.rsqrt(jnp.mean(c * c, axis=-1, keepdims=True) + RMS_EPS) * gkv_ref[...]
    kp = kv[:, kvl:kvl + LANES] * cos + kv[:, kvl + LANES:kvl + 2 * LANES] * sin
    ckv_ref[...] = ckv
    kpe_ref[...] = kp[:, :rope]
    kc_ref[...] = ckv.astype(BF16)
    kp_ref[...] = kp[:, :rope].astype(BF16)

    modq = modq_ref[...]
    hq = (x * (1.0 + modq[:, d:2 * d]) + modq[:, :d]).astype(BF16)
    qd = jnp.dot(hq, wdq_ref[...], preferred_element_type=F32)
    qn = (qd * lax.rsqrt(jnp.mean(qd * qd, axis=-1, keepdims=True) + RMS_EPS) * gq_ref[...]).astype(BF16)
    q = jnp.dot(qn, wuq_ref[...], preferred_element_type=F32)
    for h in range(nh):
        qh_ref[h, :, :nope] = (q[:, h * nope:(h + 1) * nope] * q_scale).astype(BF16)
    pe0 = nh * nope
    sw0 = pe0 + nh * rope
    per = LANES // rope
    for j in range(nh // per):
        r2 = (q[:, pe0 + j * LANES:pe0 + (j + 1) * LANES] * cos
              + q[:, sw0 + j * LANES:sw0 + (j + 1) * LANES] * sin) * q_scale
        for i in range(per):
            qh_ref[j * per + i, :, nope:] = r2[:, i * rope:(i + 1) * rope].astype(BF16)


def _qkv(x, modk, modq, cos_t, sin_t, w, tm, dims):
    bsz, s, d = x.shape
    kvl, rope, nh, nope = dims["kvl"], dims["rope"], dims["nh"], dims["nope"]
    tok = lambda width: pl.BlockSpec((None, tm, width), lambda b, t: (b, t, 0))
    head = lambda width: pl.BlockSpec((None, nh, tm, width), lambda b, t: (b, 0, t, 0))
    tab = pl.BlockSpec((tm, LANES), lambda b, t: (t, 0))
    weights = [w["w_dkv"], w["g_kv"], w["w_dq"], w["g_q"], w["w_uq"]]
    kern = functools.partial(_qkv_kernel, d=d, kvl=kvl, rope=rope, nh=nh, nope=nope, q_scale=dims["q_scale"])
    return pl.pallas_call(
        kern,
        grid=(bsz, s // tm),
        in_specs=[tok(d),
                  pl.BlockSpec((None, 1, 2 * d), lambda b, t: (b, 0, 0)),
                  pl.BlockSpec((None, 1, 3 * d), lambda b, t: (b, 0, 0)),
                  tab, tab] + [_full(a.shape) for a in weights],
        out_specs=[tok(kvl), tok(rope), tok(kvl), tok(rope), head(nope + rope)],
        out_shape=[jax.ShapeDtypeStruct((bsz, s, kvl), F32), jax.ShapeDtypeStruct((bsz, s, rope), F32),
                   jax.ShapeDtypeStruct((bsz, s, kvl), BF16), jax.ShapeDtypeStruct((bsz, s, rope), BF16),
                   jax.ShapeDtypeStruct((bsz, nh, s, nope + rope), BF16)],
        compiler_params=_params(("arbitrary", "arbitrary")),
        name="mla_qkv",
    )(x, modk, modq, cos_t, sin_t, *weights)


def _kproj_kernel(kc_ref, kp_ref, wuk_ref, kh_ref, *, nh, nope):
    kc = kc_ref[...]
    kp = kp_ref[...]
    nt = (((1,), (1,)), ((), ()))
    for h in range(nh):
        kh_ref[h, :, :nope] = lax.dot_general(kc, wuk_ref[h], nt, preferred_element_type=F32).astype(BF16)
        kh_ref[h, :, nope:] = kp


def _kproj(kc, kp, w_uk, tm):
    bsz, s, kvl = kc.shape
    rope = kp.shape[-1]
    nh, nope, _ = w_uk.shape
    tok = lambda width: pl.BlockSpec((None, tm, width), lambda b, t: (b, t, 0))
    return pl.pallas_call(
        functools.partial(_kproj_kernel, nh=nh, nope=nope),
        grid=(bsz, s // tm),
        in_specs=[tok(kvl), tok(rope), _full(w_uk.shape)],
        out_specs=pl.BlockSpec((None, nh, tm, nope + rope), lambda b, t: (b, 0, t, 0)),
        out_shape=jax.ShapeDtypeStruct((bsz, nh, s, nope + rope), BF16),
        compiler_params=_params(("arbitrary", "arbitrary")),
        name="mla_kproj",
    )(kc, kp, w_uk)


def _attn_kernel(qi_ref, kj_ref, last_ref, edge_ref, qh_ref, kh_ref, kc_ref, o_ref, m_s, l_s, acc_s,
                 *, nh, tq, tk, kvl, causal, valid):
    s_id = pl.program_id(1)
    qi = qi_ref[s_id]
    kj = kj_ref[s_id]
    nt = (((1,), (1,)), ((), ()))

    @pl.when(kj == 0)
    def _():
        m_s[...] = jnp.full_like(m_s, -jnp.inf)
        l_s[...] = jnp.zeros_like(l_s)
        acc_s[...] = jnp.zeros_like(acc_s)

    def sweep(masked, nk):
        kc = kc_ref[:nk, :]
        if masked:
            kidx = kj * tk + lax.broadcasted_iota(jnp.int32, (tq, nk), 1)
            if causal:
                qidx = qi * tq + lax.broadcasted_iota(jnp.int32, (tq, nk), 0)
                visible = (kidx // CHUNK) <= (qidx // CHUNK)
            else:
                visible = kidx < valid

        def scores(h):
            s = lax.dot_general(qh_ref[h], kh_ref[h, :nk, :], nt, preferred_element_type=F32)
            return jnp.where(visible, s, -jnp.inf) if masked else s

        s_next = scores(0)
        for h in range(nh):
            s = s_next
            if h + 1 < nh:
                s_next = scores(h + 1)
            slabs = [s[:, j * LANES:(j + 1) * LANES] for j in range(nk // LANES)]
            mx = slabs[0]
            for sj in slabs[1:]:
                mx = jnp.maximum(mx, sj)
            m_prev = m_s[h]
            m_new = jnp.maximum(m_prev, jnp.max(mx, axis=1, keepdims=True))
            scale = jnp.exp2(m_prev - m_new)
            ps = []
            rsum = None
            for sj in slabs:
                pj = jnp.exp2(sj - m_new)
                ps.append(pj.astype(BF16))
                rsum = pj if rsum is None else rsum + pj
            p = jnp.concatenate(ps, axis=1)
            l_s[h] = scale * l_s[h] + jnp.sum(rsum, axis=1, keepdims=True)
            pv = jnp.dot(p, kc, preferred_element_type=F32)
            acc_s[h] = acc_s[h] * jnp.concatenate([scale] * (kvl // LANES), axis=1) + pv
            m_s[h] = m_new

    edge = edge_ref[s_id]

    if causal:
        @pl.when(edge == 2)
        def _():
            sweep(True, tk // 2)

    @pl.when(edge == 1)
    def _():
        sweep(True, tk)

    @pl.when(edge == 0)
    def _():
        sweep(False, tk)

    @pl.when(last_ref[s_id] == 1)
    def _():
        for h in range(nh):
            inv = jnp.concatenate([l_s[h]] * (kvl // LANES), axis=1)
            o_ref[h] = (acc_s[h] / inv).astype(BF16)


def _attention(qh, kh, kc, tq, tk, causal, valid):
    bsz, nh, s, qk = qh.shape
    skv, kvl = kc.shape[1], kc.shape[2]
    qi, kj, last, edge = [], [], [], []
    for i in range(s // tq):
        hi = ((i * tq + tq - 1) // tk) if causal else (skv // tk - 1)
        for j in range(hi + 1):
            qi.append(i)
            kj.append(j)
            last.append(1 if j == hi else 0)
            if causal:
                if (i + 1) * tq <= j * tk + tk // 2:
                    edge.append(2)
                else:
                    edge.append(1 if ((j + 1) * tk - 1) // CHUNK > (i * tq) // CHUNK else 0)
            else:
                edge.append(1 if (j + 1) * tk > valid else 0)
    tabs = [jnp.asarray(np.array(a, np.int32)) for a in (qi, kj, last, edge)]
    qspec = lambda width: pl.BlockSpec((None, nh, tq, width), lambda b, t, qi, kj, last, edge: (b, 0, qi[t], 0))
    khspec = pl.BlockSpec((None, nh, tk, qk), lambda b, t, qi, kj, last, edge: (b, 0, kj[t], 0))
    kcspec = pl.BlockSpec((None, tk, kvl), lambda b, t, qi, kj, last, edge: (b, kj[t], 0))
    grid_spec = pltpu.PrefetchScalarGridSpec(
        num_scalar_prefetch=4,
        grid=(bsz, len(qi)),
        in_specs=[qspec(qk), khspec, kcspec],
        out_specs=qspec(kvl),
        scratch_shapes=[pltpu.VMEM((nh, tq, LANES), F32), pltpu.VMEM((nh, tq, LANES), F32),
                        pltpu.VMEM((nh, tq, kvl), F32)],
    )
    kern = functools.partial(_attn_kernel, nh=nh, tq=tq, tk=tk, kvl=kvl, causal=causal, valid=valid)
    return pl.pallas_call(
        kern,
        grid_spec=grid_spec,
        out_shape=jax.ShapeDtypeStruct((bsz, nh, s, kvl), BF16),
        compiler_params=_params(("arbitrary", "arbitrary")),
        name="mla_attention",
    )(*tabs, qh, kh, kc)


def _attn_out_kernel(o_ref, x_ref, mod_ref, mod2_ref, wuv_ref, wo_ref, lng_ref, lnb_ref, x3_ref, hm_ref,
                     *, d, nh, alpha):
    parts = [jnp.dot(o_ref[h], wuv_ref[h], preferred_element_type=F32) for h in range(nh)]
    o = jnp.concatenate(parts, axis=1).astype(BF16)
    m = jnp.dot(o, wo_ref[...], preferred_element_type=F32)
    gate = mod_ref[...][:, 2 * d:]
    x3 = _layer_norm(alpha * x_ref[...] + (1.0 + gate) * m, lng_ref[...], lnb_ref[...])
    x3_ref[...] = x3
    mod2 = mod2_ref[...]
    hm_ref[...] = _pack_pairs(x3 * (1.0 + mod2[:, d:2 * d]) + mod2[:, :d])


def _attn_out(o_lat, x, mod, mod2, w_uv, w_o, ln_g, ln_b, tm, alpha):
    bsz, s, d = x.shape
    nh, kvl = o_lat.shape[1], o_lat.shape[3]
    tok = lambda width: pl.BlockSpec((None, tm, width), lambda b, t: (b, t, 0))
    modspec = pl.BlockSpec((None, 1, 3 * d), lambda b, t: (b, 0, 0))
    return pl.pallas_call(
        functools.partial(_attn_out_kernel, d=d, nh=nh, alpha=alpha),
        grid=(bsz, s // tm),
        in_specs=[pl.BlockSpec((None, nh, tm, kvl), lambda b, t: (b, 0, t, 0)), tok(d), modspec, modspec,
                  _full(w_uv.shape), _full(w_o.shape), _full((1, d)), _full((1, d))],
        out_specs=[tok(d), tok(d // 2)],
        out_shape=[jax.ShapeDtypeStruct((bsz, s, d), F32), jax.ShapeDtypeStruct((bsz, s, d // 2), I32)],
        compiler_params=_params(("arbitrary", "arbitrary")),
        name="mla_out",
    )(o_lat, x, mod, mod2, w_uv, w_o, ln_g, ln_b)


def _rope_tables(pos, rope):
    half = rope // 2
    inv = ROPE_THETA ** (-2.0 * jnp.arange(half, dtype=F32) / rope)
    ang = pos.astype(F32)[:, None] * inv[None, :]
    cos, sin = jnp.cos(ang), jnp.sin(ang)
    rep = LANES // rope
    return (jnp.concatenate([cos, cos] * rep, axis=1), jnp.concatenate([-sin, sin] * rep, axis=1))


def kernel(x_prompt, x_sample, state_conv, state_rglru, cache_ckv, cache_kpe, c_prompt, c_sample, w_ada, b_ada, ln_g, ln_b, lru_w_in, lru_b_in, lru_conv_w, lru_conv_b, lru_w_a, lru_b_a, lru_w_i, lru_b_i, lru_lambda, lru_w_out, lru_b_out, kv_w_ada, kv_b_ada, mla_w_dkv, mla_g_kv, mla_w_uk, mla_w_uv, mla_w_dq, mla_g_q, mla_w_uq, mla_w_o, moe_w_r, moe_b_r, moe_w_gu, moe_b_gu, moe_w_dn, moe_b_dn):
    bp, sp, d = x_prompt.shape
    bs, ss, _ = x_sample.shape
    depth = w_ada.shape[0]
    alpha = float((2.0 * depth) ** 0.25)
    nh, nope, kvl = mla_w_uk.shape
    rope = cache_kpe.shape[-1]
    past = cache_ckv.shape[1]
    hist = state_conv.shape[2]
    dims = dict(kvl=kvl, rope=rope, nh=nh, nope=nope, q_scale=float((nope + rope) ** -0.5 * LOG2E))
    row = lambda v: v.reshape(1, -1)

    nrow = bp + bs
    rpad = -nrow % (2 * SUBLANES)
    c_rows = jnp.concatenate([c_prompt, c_sample, jnp.zeros((rpad, d), F32)], axis=0)
    mods = _ada(c_rows, w_ada.reshape(depth * 2, d, 3 * d), b_ada.reshape(depth * 2, 1, 3 * d))
    modkv = _ada(c_rows, kv_w_ada.reshape(1, d, 2 * d), kv_b_ada.reshape(1, 1, 2 * d))[0]
    mod_p = lambda i: mods[i, :bp].reshape(bp, 1, 3 * d)
    mod_s = lambda i: mods[i, bp:nrow].reshape(bs, 1, 3 * d)

    l0w = dict(w_in=lru_w_in[0].astype(BF16), b_in=row(lru_b_in[0]), conv_w=lru_conv_w[0], conv_b=row(lru_conv_b[0]),
               w_a=lru_w_a[0].astype(BF16), b_a=row(lru_b_a[0]), w_i=lru_w_i[0].astype(BF16), b_i=row(lru_b_i[0]),
               lam=row(lru_lambda[0]), w_out=lru_w_out[0].astype(BF16), b_out=row(lru_b_out[0]),
               ln_g=row(ln_g[0, 0]), ln_b=row(ln_b[0, 0]))
    x1p, hmp, conv_p, h_p = _l0_mixer(x_prompt, mod_p(0), mod_p(1), jnp.zeros((bp, hist, d), F32),
                                      jnp.zeros((bp, 1, d), F32), l0w, L0_TB, alpha)
    x1s, hms, conv_s, h_s = _l0_mixer(x_sample, mod_s(0), mod_s(1), state_conv[0], state_rglru[0].reshape(bs, 1, d),
                                      l0w, ss, alpha)

    (y4p, twp), (y4s, tws) = _moe(hmp, hms, moe_w_r[0], moe_b_r[0], moe_w_gu, moe_b_gu, moe_w_dn, moe_b_dn, 0)
    x2p = _combine(x1p, y4p, twp, mod_p(1), row(ln_g[0, 1]), row(ln_b[0, 1]), PROMPT_TM, alpha)
    x2s = _combine(x1s, y4s, tws, mod_s(1), row(ln_g[0, 1]), row(ln_b[0, 1]), ss, alpha)

    pe_cols = jnp.arange(rope)
    sw_cols = jnp.concatenate([pe_cols[rope // 2:], pe_cols[:rope // 2]])
    zpad = jnp.zeros((d, LANES - rope), F32)
    w_dkv_ext = jnp.concatenate([mla_w_dkv[:, :kvl], mla_w_dkv[:, kvl:], zpad,
                                 mla_w_dkv[:, kvl:][:, sw_cols], zpad], axis=1).astype(BF16)
    wq = mla_w_uq[0].reshape(-1, nh, nope + rope)
    w_uq_ext = jnp.concatenate([wq[:, :, :nope].reshape(-1, nh * nope), wq[:, :, nope:].reshape(-1, nh * rope),
                                wq[:, :, nope:][:, :, sw_cols].reshape(-1, nh * rope)], axis=1).astype(BF16)
    qw = dict(w_dkv=w_dkv_ext, g_kv=row(mla_g_kv), w_dq=mla_w_dq[0].astype(BF16), g_q=row(mla_g_q[0]),
              w_uq=w_uq_ext)
    w_uk = mla_w_uk.astype(BF16)
    cos_p, sin_p = _rope_tables(jnp.arange(sp), rope)
    cos_s, sin_s = _rope_tables(past + jnp.arange(ss), rope)
    modkv_p = modkv[:bp].reshape(bp, 1, 2 * d)
    modkv_s = modkv[bp:nrow].reshape(bs, 1, 2 * d)
    ckv_p, kpe_p, kc_p, kp_p, qh_p = _qkv(x2p, modkv_p, mod_p(2), cos_p, sin_p, qw, PROMPT_TM, dims)
    ckv_s, kpe_s, kc_s, kp_s, qh_s = _qkv(x2s, modkv_s, mod_s(2), cos_s, sin_s, qw, ss, dims)

    o_p = _attention(qh_p, _kproj(kc_p, kp_p, w_uk, PROMPT_TM), kc_p, ATT_TQ, ATT_TK, True, sp)
    skv = past + ss
    kpad = -skv % LANES
    kc_all = jnp.concatenate([cache_ckv.astype(BF16), kc_s, jnp.zeros((bs, kpad, kvl), BF16)], axis=1)
    kp_all = jnp.concatenate([cache_kpe.astype(BF16), kp_s, jnp.zeros((bs, kpad, rope), BF16)], axis=1)
    o_s = _attention(qh_s, _kproj(kc_all, kp_all, w_uk, skv + kpad), kc_all, ss, skv + kpad, False, skv)
    w_uv = mla_w_uv.astype(BF16)
    w_o = mla_w_o[0].astype(BF16)
    x3p, hm3p = _attn_out(o_p, x2p, mod_p(2), mod_p(3), w_uv, w_o, row(ln_g[1, 0]), row(ln_b[1, 0]), PROMPT_TM, alpha)
    x3s, hm3s = _attn_out(o_s, x2s, mod_s(2), mod_s(3), w_uv, w_o, row(ln_g[1, 0]), row(ln_b[1, 0]), ss, alpha)

    (y4p, twp), (y4s, tws) = _moe(hm3p, hm3s, moe_w_r[1], moe_b_r[1], moe_w_gu, moe_b_gu, moe_w_dn, moe_b_dn, 1)
    y_p = _combine(x3p, y4p, twp, mod_p(3), row(ln_g[1, 1]), row(ln_b[1, 1]), PROMPT_TM, alpha)
    y_s = _combine(x3s, y4s, tws, mod_s(3), row(ln_g[1, 1]), row(ln_b[1, 1]), ss, alpha)

    return (y_p, y_s, conv_p[None], h_p.reshape(1, bp, d), ckv_p, kpe_p,
            conv_s[None], h_s.reshape(1, bs, d), ckv_s, kpe_s)
```
